```python
import jax, jax.numpy as jnp
from jax import lax
import numpy as np

D_MODEL = 1024
BATCH = 8
SEQ = 2048
DEPTH = 2
DEC_BATCH = 32
DEC_SEQ = 8
PAST_LEN = 16384
PAGE_SIZE = 128

N_FOX = (DEPTH + 1) // 2
N_RET = DEPTH // 2
MEM_HEADS = 4
MEM_HEAD_DIM = D_MODEL // 16
MEM_WIDTH = MEM_HEADS * MEM_HEAD_DIM
N_MEM = 256
SEQ_MIX_WIDTH = D_MODEL - MEM_WIDTH
FOX_HEAD_DIM = 64
FOX_HEADS = SEQ_MIX_WIDTH // FOX_HEAD_DIM
RET_HEAD_DIM = 128
RET_HEADS = SEQ_MIX_WIDTH // RET_HEAD_DIM
FOX_IN = 4 * SEQ_MIX_WIDTH + FOX_HEADS + MEM_WIDTH
RET_IN = 4 * SEQ_MIX_WIDTH + MEM_WIDTH
Q_BLOCK = 128
RET_CHUNK = 128
N_GROUPS = 4
EXPERTS_PER_GROUP = 8
N_EXPERTS = N_GROUPS * EXPERTS_PER_GROUP
TOP_K_IN_GROUP = 2
D_EXPERT = D_MODEL // 4
ROPE_BASE = 10000.0
EPS = 1e-6
NEG = -1e30
POOL_EXTRA = 4

kernel_name = 'fox_retention_hmoe_memory_step'

f32 = jnp.float32


def rms_norm(x, g):
    xf = x.astype(f32)
    y = xf * lax.rsqrt(jnp.mean(xf * xf, axis=-1, keepdims=True) + EPS)
    return (y * g.astype(f32)).astype(x.dtype)


def rotary(x, pos):
    half = x.shape[-1] // 2
    inv = ROPE_BASE ** (-jnp.arange(half, dtype=f32) / half)
    ang = pos.astype(f32)[:, None] * inv[None, :]
    cos = jnp.cos(ang)[None, :, None, :]
    sin = jnp.sin(ang)[None, :, None, :]
    xf = x.astype(f32)
    x1, x2 = xf[..., :half], xf[..., half:]
    return jnp.concatenate([x1 * cos - x2 * sin, x2 * cos + x1 * sin], axis=-1).astype(x.dtype)


def ret_log_decay():
    return jnp.log1p(-jnp.exp2(-5.0 - jnp.arange(RET_HEADS, dtype=f32)))


def fox_inputs(h, w_in, b_f, g_q, g_k):
    B, T, _ = h.shape
    W = SEQ_MIX_WIDTH
    proj = h @ w_in
    hd = (B, T, FOX_HEADS, FOX_HEAD_DIM)
    q = rms_norm(proj[..., :W].reshape(hd), g_q)
    k = rms_norm(proj[..., W:2 * W].reshape(hd), g_k)
    v = proj[..., 2 * W:3 * W].reshape(hd)
    og = proj[..., 3 * W:4 * W].reshape(hd)
    logf = jax.nn.log_sigmoid(proj[..., 4 * W:4 * W + FOX_HEADS].astype(f32) + b_f.astype(f32))
    qm = proj[..., 4 * W + FOX_HEADS:].reshape(B, T, MEM_HEADS, MEM_HEAD_DIM)
    return q, k, v, og, logf, qm


def fox_prompt(q, k, v, logf):
    B, S, H, d = q.shape
    scale = d ** -0.5
    c = jnp.cumsum(logf, axis=1).transpose(0, 2, 1)
    kpos = jnp.arange(S)

    def block(i):
        start = i * Q_BLOCK
        qb = lax.dynamic_slice_in_dim(q, start, Q_BLOCK, axis=1)
        cb = lax.dynamic_slice_in_dim(c, start, Q_BLOCK, axis=2)
        s = jnp.einsum('bqhd,bkhd->bhqk', qb, k, preferred_element_type=f32) * scale
        s = s + cb[..., None] - c[:, :, None, :]
        qpos = start + jnp.arange(Q_BLOCK)
        s = jnp.where(qpos[:, None] >= kpos[None, :], s, NEG)
        p = jax.nn.softmax(s, axis=-1)
        return jnp.einsum('bhqk,bkhd->bqhd', p.astype(v.dtype), v)

    o = lax.map(block, jnp.arange(S // Q_BLOCK))
    return o.transpose(1, 0, 2, 3, 4).reshape(B, S, H, d)


def fox_sample(q, k, v, logf, past_k, past_v, past_logf):
    T = q.shape[1]
    P = past_k.shape[1]
    scale = q.shape[-1] ** -0.5
    c_past = jnp.cumsum(past_logf.astype(f32), axis=1)
    c_new = c_past[:, -1:, :] + jnp.cumsum(logf, axis=1)
    cq = c_new.transpose(0, 2, 1)[..., None]
    s_past = jnp.einsum('bthd,bshd->bhts', q, past_k, preferred_element_type=f32) * scale
    s_past = s_past + cq - c_past.transpose(0, 2, 1)[:, :, None, :]
    s_new = jnp.einsum('bthd,bshd->bhts', q, k, preferred_element_type=f32) * scale
    s_new = s_new + cq - c_new.transpose(0, 2, 1)[:, :, None, :]
    tpos = jnp.arange(T)
    s_new = jnp.where(tpos[:, None] >= tpos[None, :], s_new, NEG)
    p = jax.nn.softmax(jnp.concatenate([s_past, s_new], axis=-1), axis=-1).astype(v.dtype)
    return (jnp.einsum('bhts,bshd->bthd', p[..., :P], past_v)
            + jnp.einsum('bhts,bshd->bthd', p[..., P:], v))


def fox_output(o, og):
    B, T = o.shape[:2]
    return (o * jax.nn.sigmoid(og)).reshape(B, T, SEQ_MIX_WIDTH)


def ret_inputs(h, w_in, pos):
    B, T, _ = h.shape
    W = SEQ_MIX_WIDTH
    proj = h @ w_in
    hd = (B, T, RET_HEADS, RET_HEAD_DIM)
    q = rotary(proj[..., :W].reshape(hd), pos)
    k = rotary(proj[..., W:2 * W].reshape(hd), pos) * (RET_HEAD_DIM ** -0.5)
    v = proj[..., 2 * W:3 * W].reshape(hd)
    gate = proj[..., 3 * W:4 * W]
    qm = proj[..., 4 * W:].reshape(B, T, MEM_HEADS, MEM_HEAD_DIM)
    return q, k, v, gate, qm


def retention_chunk(S, q, k, v, log_g):
    C = q.shape[2]
    n = jnp.arange(C, dtype=f32)
    diff = n[:, None] - n[None, :]
    decay = jnp.where(diff >= 0, jnp.exp(jnp.maximum(diff, 0.0)[None] * log_g[:, None, None]), 0.0)
    inner = jnp.einsum('bhnk,bhmk->bhnm', q, k) * decay
    q_dec = jnp.exp((n + 1.0)[None, :] * log_g[:, None])
    k_dec = jnp.exp((C - 1.0 - n)[None, :] * log_g[:, None])
    o = (jnp.einsum('bhnm,bhmv->bhnv', inner, v)
         + jnp.einsum('bhnk,bhkv->bhnv', q * q_dec[None, :, :, None], S))
    S_new = (jnp.exp(C * log_g)[None, :, None, None] * S
             + jnp.einsum('bhmk,bhmv->bhkv', k * k_dec[None, :, :, None], v))
    return o, S_new


def retention_prompt(q, k, v):
    B, S, H, dk = q.shape
    dv = v.shape[-1]
    nc = S // RET_CHUNK
    log_g = ret_log_decay()

    def to_chunks(a):
        return a.astype(f32).reshape(B, nc, RET_CHUNK, H, a.shape[-1]).transpose(1, 0, 3, 2, 4)

    def step(state, xs):
        qc, kc, vc = xs
        o, state = retention_chunk(state, qc, kc, vc, log_g)
        return state, o

    S_fin, o = lax.scan(step, jnp.zeros((B, H, dk, dv), f32), (to_chunks(q), to_chunks(k), to_chunks(v)))
    return o.transpose(1, 0, 3, 2, 4).reshape(B, S, H, dv), S_fin


def retention_sample(q, k, v, S):
    tr = lambda a: a.astype(f32).transpose(0, 2, 1, 3)
    o, S_new = retention_chunk(S.astype(f32), tr(q), tr(k), tr(v), ret_log_decay())
    return o.transpose(0, 2, 1, 3), S_new


def ret_output(o, gate, g_o):
    B, T = o.shape[:2]
    y = rms_norm(o, g_o).reshape(B, T, SEQ_MIX_WIDTH) * jax.nn.silu(gate.astype(f32))
    return y.astype(gate.dtype)


def memory_kv(mem, g, w, g_k):
    B, N, _ = mem.shape
    kv = rms_norm(mem, g) @ w
    k = rms_norm(kv[..., :MEM_WIDTH].reshape(B, N, MEM_HEADS, MEM_HEAD_DIM), g_k)
    v = kv[..., MEM_WIDTH:].reshape(B, N, MEM_HEADS, MEM_HEAD_DIM)
    return k, v


def memory_attend(qm, mk, mv):
    B, T = qm.shape[:2]
    s = jnp.einsum('bthd,bmhd->bhtm', qm, mk, preferred_element_type=f32) * (MEM_HEAD_DIM ** -0.5)
    p = jax.nn.softmax(s, axis=-1).astype(mv.dtype)
    return jnp.einsum('bhtm,bmhd->bthd', p, mv).reshape(B, T, MEM_WIDTH)


def hier_moe(x, g, w_rg, b_rg, w_re, b_re, w1, w3, w2):
    B, T, D = x.shape
    h = rms_norm(x, g).reshape(B * T, D)
    lg = (h @ w_rg).astype(f32) + b_rg.astype(f32)
    gi = jnp.argmax(lg, axis=-1)
    p_top = jnp.take_along_axis(jax.nn.softmax(lg, axis=-1), gi[:, None], axis=-1)
    le = ((h @ w_re).astype(f32) + b_re.astype(f32)).reshape(-1, N_GROUPS, EXPERTS_PER_GROUP)
    le_g = jnp.take_along_axis(le, gi[:, None, None], axis=1)[:, 0]
    vals, idx = lax.top_k(le_g, TOP_K_IN_GROUP)
    w = jax.nn.softmax(vals, axis=-1) * p_top
    eid = gi[:, None] * EXPERTS_PER_GROUP + idx
    gates = jnp.sum(w[..., None] * jax.nn.one_hot(eid, N_EXPERTS, dtype=f32), axis=1)
    a = jnp.einsum('nd,edf->nef', h, w1)
    b = jnp.einsum('nd,edf->nef', h, w3)
    hid = jax.nn.silu(a) * b * gates.astype(h.dtype)[..., None]
    return jnp.einsum('nef,efd->nd', hid, w2).reshape(B, T, D)


def setup_inputs(seed: int = 0) -> dict:
    key = jax.random.key(seed)
    k = jax.random.split(key, 32)

    def nrm(kk, shape, scale=1.0):
        return jax.random.normal(kk, shape, f32) * scale

    def gain(kk, shape):
        return 1.0 + 0.05 * jax.random.normal(kk, shape, f32)

    n_pages = PAST_LEN // PAGE_SIZE
    n_used = DEC_BATCH * n_pages
    n_pool = n_used + n_used // POOL_EXTRA
    page_table = jax.random.permutation(k[0], n_pool)[:n_used].reshape(DEC_BATCH, n_pages).astype(jnp.int32)
    out_scale = (D_MODEL ** -0.5) * 0.5
    return {
        'x_prompt': nrm(k[1], (BATCH, SEQ, D_MODEL)),
        'x_sample': nrm(k[2], (DEC_BATCH, DEC_SEQ, D_MODEL)),
        'mem_prompt': nrm(k[3], (BATCH, N_MEM, D_MODEL)),
        'cache_fox_k': nrm(k[4], (N_FOX, n_pool, PAGE_SIZE, FOX_HEADS, FOX_HEAD_DIM)),
        'cache_fox_v': nrm(k[5], (N_FOX, n_pool, PAGE_SIZE, FOX_HEADS, FOX_HEAD_DIM)),
        'cache_fox_logf': jax.nn.log_sigmoid(2.0 + nrm(k[6], (N_FOX, n_pool, PAGE_SIZE, FOX_HEADS))),
        'page_table': page_table,
        'state_ret': nrm(k[7], (N_RET, DEC_BATCH, RET_HEADS, RET_HEAD_DIM, RET_HEAD_DIM), 0.1),
        'cache_mem_k': nrm(k[8], (DEPTH, DEC_BATCH, N_MEM, MEM_HEADS, MEM_HEAD_DIM)),
        'cache_mem_v': nrm(k[9], (DEPTH, DEC_BATCH, N_MEM, MEM_HEADS, MEM_HEAD_DIM)),
        'w_in_fox': nrm(k[10], (N_FOX, D_MODEL, FOX_IN), D_MODEL ** -0.5),
        'b_f_fox': 2.0 + nrm(k[11], (N_FOX, FOX_HEADS), 0.5),
        'g_q_fox': gain(k[12], (N_FOX, FOX_HEAD_DIM)),
        'g_k_fox': gain(k[13], (N_FOX, FOX_HEAD_DIM)),
        'w_in_ret': nrm(k[14], (N_RET, D_MODEL, RET_IN), D_MODEL ** -0.5),
        'g_o_ret': gain(k[15], (N_RET, RET_HEAD_DIM)),
        'g_attn': gain(k[16], (DEPTH, D_MODEL)),
        'g_mem': gain(k[17], (DEPTH, D_MODEL)),
        'w_mem_kv': nrm(k[18], (DEPTH, D_MODEL, 2 * MEM_WIDTH), D_MODEL ** -0.5),
        'g_mq': gain(k[19], (DEPTH, MEM_HEAD_DIM)),
        'g_mk': gain(k[20], (DEPTH, MEM_HEAD_DIM)),
        'w_out': nrm(k[21], (DEPTH, D_MODEL, D_MODEL), out_scale),
        'g_ffn': gain(k[22], (DEPTH, D_MODEL)),
        'w_rg': nrm(k[23], (DEPTH, D_MODEL, N_GROUPS), D_MODEL ** -0.5),
        'b_rg': nrm(k[24], (DEPTH, N_GROUPS), 0.01),
        'w_re': nrm(k[25], (DEPTH, D_MODEL, N_EXPERTS), D_MODEL ** -0.5),
        'b_re': nrm(k[26], (DEPTH, N_EXPERTS), 0.01),
        'w1': nrm(k[27], (DEPTH, N_EXPERTS, D_MODEL, D_EXPERT), D_MODEL ** -0.5),
        'w3': nrm(k[28], (DEPTH, N_EXPERTS, D_MODEL, D_EXPERT), D_MODEL ** -0.5),
        'w2': nrm(k[29], (DEPTH, N_EXPERTS, D_EXPERT, D_MODEL), D_EXPERT ** -0.5),
    }


def reference(x_prompt, x_sample, mem_prompt, cache_fox_k, cache_fox_v, cache_fox_logf, page_table,
              state_ret, cache_mem_k, cache_mem_v, w_in_fox, b_f_fox, g_q_fox, g_k_fox, w_in_ret, g_o_ret,
              g_attn, g_mem, w_mem_kv, g_mq, g_mk, w_out, g_ffn, w_rg, b_rg, w_re, b_re, w1, w3, w2):
    Bp, Tp, _ = x_prompt.shape
    Bs, Ts, _ = x_sample.shape
    P = page_table.shape[1] * PAGE_SIZE
    pos_p = jnp.arange(Tp)
    pos_s = P + jnp.arange(Ts)
    xp, xs = x_prompt, x_sample
    fkp, fvp, flp, fks, fvs, fls = [], [], [], [], [], []
    rsp, rss, mkp_all, mvp_all = [], [], [], []
    for l in range(DEPTH):
        i = l // 2
        hp = rms_norm(xp, g_attn[l])
        hs = rms_norm(xs, g_attn[l])
        mk_p, mv_p = memory_kv(mem_prompt, g_mem[l], w_mem_kv[l], g_mk[l])
        mkp_all.append(mk_p)
        mvp_all.append(mv_p)
        if l % 2 == 0:
            qp, kp, vp, ogp, lfp, qmp = fox_inputs(hp, w_in_fox[i], b_f_fox[i], g_q_fox[i], g_k_fox[i])
            qs, ks_, vs, ogs, lfs, qms = fox_inputs(hs, w_in_fox[i], b_f_fox[i], g_q_fox[i], g_k_fox[i])
            past_k = cache_fox_k[i][page_table].reshape(Bs, P, FOX_HEADS, FOX_HEAD_DIM)
            past_v = cache_fox_v[i][page_table].reshape(Bs, P, FOX_HEADS, FOX_HEAD_DIM)
            past_lf = cache_fox_logf[i][page_table].reshape(Bs, P, FOX_HEADS)
            mix_p = fox_output(fox_prompt(qp, kp, vp, lfp), ogp)
            mix_s = fox_output(fox_sample(qs, ks_, vs, lfs, past_k, past_v, past_lf), ogs)
            fkp.append(kp)
            fvp.append(vp)
            flp.append(lfp.astype(xp.dtype))
            fks.append(ks_)
            fvs.append(vs)
            fls.append(lfs.astype(xs.dtype))
        else:
            qp, kp, vp, gp, qmp = ret_inputs(hp, w_in_ret[i], pos_p)
            qs, ks_, vs, gs, qms = ret_inputs(hs, w_in_ret[i], pos_s)
            op, sp = retention_prompt(qp, kp, vp)
            os_, ss = retention_sample(qs, ks_, vs, state_ret[i])
            mix_p = ret_output(op, gp, g_o_ret[i])
            mix_s = ret_output(os_, gs, g_o_ret[i])
            rsp.append(sp)
            rss.append(ss)
        memo_p = memory_attend(rms_norm(qmp, g_mq[l]), mk_p, mv_p)
        memo_s = memory_attend(rms_norm(qms, g_mq[l]), cache_mem_k[l], cache_mem_v[l])
        xp = xp + jnp.concatenate([mix_p, memo_p], axis=-1) @ w_out[l]
        xs = xs + jnp.concatenate([mix_s, memo_s], axis=-1) @ w_out[l]
        xp = xp + hier_moe(xp, g_ffn[l], w_rg[l], b_rg[l], w_re[l], b_re[l], w1[l], w3[l], w2[l])
        xs = xs + hier_moe(xs, g_ffn[l], w_rg[l], b_rg[l], w_re[l], b_re[l], w1[l], w3[l], w2[l])
    return (xp, xs, jnp.stack(fkp), jnp.stack(fvp), jnp.stack(flp), jnp.stack(fks), jnp.stack(fvs),
            jnp.stack(fls), jnp.stack(rsp), jnp.stack(rss), jnp.stack(mkp_all), jnp.stack(mvp_all))
```

```python
import functools

import jax
import jax.numpy as jnp
from jax import lax
from jax.experimental import pallas as pl
from jax.experimental.pallas import tpu as pltpu

F32 = jnp.float32
BF16 = jnp.bfloat16
I32 = jnp.int32

D_MODEL = 1024
MIX_W = 768
MEM_W = 256
HEAD_D = 64
FOX_HEADS = 12
RET_HEADS = 6
RET_D = 128
N_MEM = 256
N_EXPERTS = 32
N_GROUPS = 4
EPG = 8
D_EXPERT = 256
PAGE = 128
EPS = 1e-6
NEG = -1e30
LANES = 128
VMEM_LIMIT = 56 * 1024 * 1024
ROUTE_LANE0 = N_GROUPS


def _cparams(sem, vmem=VMEM_LIMIT):
    return pltpu.CompilerParams(dimension_semantics=sem, vmem_limit_bytes=vmem)


def _dot(a, b):
    return jnp.dot(a, b, preferred_element_type=F32)


def _dot_nt(a, b):
    return lax.dot_general(a, b, (((1,), (1,)), ((), ())), preferred_element_type=F32)


def _split3(x):
    a = x.astype(BF16)
    r = x - a.astype(F32)
    b = r.astype(BF16)
    c = (r - b.astype(F32)).astype(BF16)
    return a, b, c


def _dot_exact_rhs01(m01, x):
    a, b, c = _split3(x)
    return _dot(m01, a) + _dot(m01, b) + _dot(m01, c)


def _rms_rows(x, gain):
    return x * lax.rsqrt(jnp.mean(x * x, axis=-1, keepdims=True) + EPS) * gain


def _head_norm(y, bd, gain):
    ss = _dot((y * y).astype(BF16), bd)
    return y * lax.rsqrt(ss * (1.0 / HEAD_D) + EPS) * gain


def _sigmoid(x):
    return 1.0 / (1.0 + jnp.exp(-x))


def _fox_inproj_body(x_ref, g_ref, w_ref, bd_ref, gq_ref, gk_ref, gmq_ref, bf_ref, tri_ref,
                     q_ref, k_ref, kb_ref, v_ref, vb_ref, og_ref, qm_ref, lf_ref, c_ref,
                     carry_ref, *, tiles_per_seq):
    t = x_ref.shape[0]
    h = _rms_rows(x_ref[...], g_ref[...]).astype(BF16)
    bd = bd_ref[...]
    for j in range(3):
        sl = slice(j * 256, (j + 1) * 256)
        y = _dot(h, w_ref[:, sl])
        q_ref[:, sl] = (_head_norm(y, bd, gq_ref[...]) * 0.125).astype(q_ref.dtype)
    for j in range(3):
        sl = slice(j * 256, (j + 1) * 256)
        y = _dot(h, w_ref[:, MIX_W + j * 256:MIX_W + (j + 1) * 256])
        kn = _head_norm(y, bd, gk_ref[...])
        k_ref[:, sl] = kn
        kb_ref[:, sl] = kn.astype(BF16)
    for j in range(3):
        sl = slice(j * 256, (j + 1) * 256)
        y = _dot(h, w_ref[:, 2 * MIX_W + j * 256:2 * MIX_W + (j + 1) * 256])
        v_ref[:, sl] = y
        vb_ref[:, sl] = y.astype(BF16)
    for j in range(3):
        sl = slice(j * 256, (j + 1) * 256)
        og_ref[:, sl] = _dot(h, w_ref[:, 3 * MIX_W + j * 256:3 * MIX_W + (j + 1) * 256])
    y = _dot(h, w_ref[:, 4 * MIX_W:4 * MIX_W + MEM_W])
    qm_ref[...] = (_head_norm(y, bd, gmq_ref[...]) * 0.125).astype(qm_ref.dtype)
    f = _dot(h, w_ref[:, 4 * MIX_W + MEM_W:]) + bf_ref[...]
    lf = jnp.minimum(f, 0.0) - jnp.log1p(jnp.exp(-jnp.abs(f)))
    lane = lax.broadcasted_iota(I32, (t, LANES), 1)
    lf = jnp.where(lane < FOX_HEADS, lf, 0.0)
    lf_ref[...] = lf[:, :FOX_HEADS]

    @pl.when(pl.program_id(0) % tiles_per_seq == 0)
    def _():
        carry_ref[...] = jnp.zeros_like(carry_ref)

    c = _dot_exact_rhs01(tri_ref[...], lf) + carry_ref[...]
    c_ref[...] = c[:, :FOX_HEADS]
    carry_ref[...] = c[t - 1:t, :]


def _fox_inproj(x, g, w, bd, gq, gk, gmq, bf, *, tile, tiles_per_seq, act_dtype):
    n = x.shape[0]
    tri = jnp.tril(jnp.ones((tile, tile), BF16))
    full = lambda shape: pl.BlockSpec(shape, lambda i: (0,) * len(shape))
    row = lambda width: pl.BlockSpec((tile, width), lambda i: (i, 0))
    outs = [
        jax.ShapeDtypeStruct((n, MIX_W), act_dtype),
        jax.ShapeDtypeStruct((n, MIX_W), F32),
        jax.ShapeDtypeStruct((n, MIX_W), BF16),
        jax.ShapeDtypeStruct((n, MIX_W), F32),
        jax.ShapeDtypeStruct((n, MIX_W), BF16),
        jax.ShapeDtypeStruct((n, MIX_W), F32),
        jax.ShapeDtypeStruct((n, MEM_W), act_dtype),
        jax.ShapeDtypeStruct((n, FOX_HEADS), F32),
        jax.ShapeDtypeStruct((n, FOX_HEADS), F32),
    ]
    return pl.pallas_call(
        functools.partial(_fox_inproj_body, tiles_per_seq=tiles_per_seq),
        out_shape=outs,
        grid=(n // tile,),
        in_specs=[row(D_MODEL), full((1, D_MODEL)), full(w.shape), full((256, 256)),
                  full((1, 256)), full((1, 256)), full((1, 256)), full((1, LANES)),
                  full((tile, tile))],
        out_specs=[row(MIX_W), row(MIX_W), row(MIX_W), row(MIX_W), row(MIX_W), row(MIX_W),
                   row(MEM_W), row(FOX_HEADS), row(FOX_HEADS)],
        scratch_shapes=[pltpu.VMEM((1, LANES), F32)],
        compiler_params=_cparams(("arbitrary",)),
        name="fox_inproj",
    )(x, g, w, bd, gq, gk, gmq, bf, tri)


def _ret_inproj_body(x_ref, g_ref, w_ref, bd_ref, gmq_ref, cos_ref, sin_ref,
                     q_ref, k_ref, v_ref, gate_ref, qm_ref):
    h = _rms_rows(x_ref[...], g_ref[...]).astype(BF16)
    cos = cos_ref[...]
    sin = sin_ref[...]
    for j in range(RET_HEADS):
        sl = slice(j * RET_D, (j + 1) * RET_D)
        y = _dot(h, w_ref[:, sl])
        q_ref[:, sl] = y * cos + pltpu.roll(y, RET_D // 2, 1) * sin
    for j in range(RET_HEADS):
        sl = slice(j * RET_D, (j + 1) * RET_D)
        y = _dot(h, w_ref[:, MIX_W + j * RET_D:MIX_W + (j + 1) * RET_D])
        k_ref[:, sl] = (y * cos + pltpu.roll(y, RET_D // 2, 1) * sin) * (RET_D ** -0.5)
    for j in range(3):
        sl = slice(j * 256, (j + 1) * 256)
        v_ref[:, sl] = _dot(h, w_ref[:, 2 * MIX_W + j * 256:2 * MIX_W + (j + 1) * 256])
    for j in range(3):
        sl = slice(j * 256, (j + 1) * 256)
        gate_ref[:, sl] = _dot(h, w_ref[:, 3 * MIX_W + j * 256:3 * MIX_W + (j + 1) * 256])
    y = _dot(h, w_ref[:, 4 * MIX_W:])
    qm_ref[...] = (_head_norm(y, bd_ref[...], gmq_ref[...]) * 0.125).astype(qm_ref.dtype)


def _ret_inproj(x, g, w, bd, gmq, cos, sin, *, tile, tiles_per_seq, act_dtype):
    n = x.shape[0]
    full = lambda shape: pl.BlockSpec(shape, lambda i: (0,) * len(shape))
    row = lambda width: pl.BlockSpec((tile, width), lambda i: (i, 0))
    tab = pl.BlockSpec((tile, RET_D), lambda i: (i % tiles_per_seq, 0))
    outs = [jax.ShapeDtypeStruct((n, MIX_W), F32)] * 4 + [jax.ShapeDtypeStruct((n, MEM_W), act_dtype)]
    return pl.pallas_call(
        _ret_inproj_body,
        out_shape=outs,
        grid=(n // tile,),
        in_specs=[row(D_MODEL), full((1, D_MODEL)), full(w.shape), full((256, 256)),
                  full((1, 256)), tab, tab],
        out_specs=[row(MIX_W)] * 4 + [row(MEM_W)],
        compiler_params=_cparams(("arbitrary",)),
        name="ret_inproj",
    )(x, g, w, bd, gmq, cos, sin)


def _fox_flash_body(q_ref, k_ref, v_ref, ccol_ref, crow0_ref, crow1_ref, og_ref, o_ref, *, blk):
    hp = pl.program_id(1)
    qi = pl.program_id(2)
    q = q_ref[...]
    lane = lax.broadcasted_iota(I32, (blk, LANES), 1)
    lo = lane < HEAD_D
    zq = jnp.zeros_like(q)
    qs = (jnp.where(lo, q, zq), jnp.where(lo, zq, q))
    cc = ccol_ref[...]
    lane12 = lax.broadcasted_iota(I32, cc.shape, 1)
    ccol = tuple(jnp.sum(jnp.where(lane12 == 2 * hp + j, cc, 0.0), axis=-1, keepdims=True)
                 for j in range(2))
    crow_refs = (crow0_ref, crow1_ref)
    rows = lax.broadcasted_iota(I32, (blk, blk), 0)
    cols = lax.broadcasted_iota(I32, (blk, blk), 1)
    causal = rows >= cols

    def step(start, carry, diag):
        ms, ls, acc = carry
        k = k_ref[pl.ds(start, blk), :]
        v = v_ref[pl.ds(start, blk), :]
        new_m, new_l, alphas, pvs = [], [], [], []
        for j in range(2):
            s = _dot_nt(qs[j], k) + ccol[j] - crow_refs[j][:, pl.ds(start, blk)]
            if diag:
                s = jnp.where(causal, s, NEG)
            m_new = jnp.maximum(ms[j], jnp.max(s, axis=-1, keepdims=True))
            alpha = jnp.exp(ms[j] - m_new)
            p = jnp.exp(s - m_new)
            new_l.append(alpha * ls[j] + jnp.sum(p, axis=-1, keepdims=True))
            new_m.append(m_new)
            alphas.append(alpha)
            pvs.append(_dot(p.astype(BF16), v))
        acc = acc * jnp.where(lo, alphas[0], alphas[1]) + jnp.where(lo, pvs[0], pvs[1])
        return tuple(new_m), tuple(new_l), acc

    init = ((jnp.full((blk, 1), NEG, F32),) * 2, (jnp.zeros((blk, 1), F32),) * 2,
            jnp.zeros((blk, LANES), F32))
    carry = lax.fori_loop(
        0, qi, lambda i, c: step(pl.multiple_of(i * blk, blk), c, False), init)
    _, ls, acc = step(pl.multiple_of(qi * blk, blk), carry, True)
    o = acc / jnp.where(lo, ls[0], ls[1])
    o_ref[...] = (o * _sigmoid(og_ref[...])).astype(o_ref.dtype)


def _fox_flash(q, kb, vb, c, c_row, og, *, batch, seq, blk):
    n = q.shape[0]
    nq = seq // blk
    return pl.pallas_call(
        functools.partial(_fox_flash_body, blk=blk),
        out_shape=jax.ShapeDtypeStruct((n, MIX_W), BF16),
        grid=(batch, FOX_HEADS // 2, nq),
        in_specs=[
            pl.BlockSpec((blk, LANES), lambda b, h, i: (b * nq + i, h)),
            pl.BlockSpec((seq, LANES), lambda b, h, i: (b, h)),
            pl.BlockSpec((seq, LANES), lambda b, h, i: (b, h)),
            pl.BlockSpec((blk, FOX_HEADS), lambda b, h, i: (b * nq + i, 0)),
            pl.BlockSpec((None, None, 1, seq), lambda b, h, i: (b, 2 * h, 0, 0)),
            pl.BlockSpec((None, None, 1, seq), lambda b, h, i: (b, 2 * h + 1, 0, 0)),
            pl.BlockSpec((blk, LANES), lambda b, h, i: (b * nq + i, h)),
        ],
        out_specs=pl.BlockSpec((blk, LANES), lambda b, h, i: (b * nq + i, h)),
        compiler_params=_cparams(("arbitrary",) * 3),
        name="fox_flash",
    )(q, kb, vb, c, c_row, c_row, og)


def _fox_decode_body(pt_ref, q_ref, kn_ref, vn_ref, lfn_ref, og_ref, *rest, pages_per_step):
    del pt_ref
    pp = pages_per_step
    k_refs = rest[:pp]
    v_refs = rest[pp:2 * pp]
    lf_refs = rest[2 * pp:3 * pp]
    tri_ref = rest[3 * pp]
    o_ref = rest[3 * pp + 1]
    qbd_ref, m_ref, l_ref, acc_ref, carry_ref = rest[3 * pp + 2:]
    j = pl.program_id(1)
    nrow = FOX_HEADS * 8
    rowh = lax.broadcasted_iota(I32, (nrow, MIX_W), 0) // 8
    colh = lax.broadcasted_iota(I32, (nrow, MIX_W), 1) // HEAD_D
    diag = rowh == colh

    @pl.when(j == 0)
    def _():
        q = q_ref[...]
        qbd_ref[...] = jnp.where(diag, jnp.concatenate([q] * FOX_HEADS, axis=0), 0.0)
        m_ref[...] = jnp.full_like(m_ref, NEG)
        l_ref[...] = jnp.zeros_like(l_ref)
        acc_ref[...] = jnp.zeros_like(acc_ref)
        carry_ref[...] = jnp.zeros_like(carry_ref)

    qbd = qbd_ref[...]
    tri = tri_ref[...]

    def scores(kt, lf, carry):
        c = _dot_exact_rhs01_left(lf, tri) + carry
        bias = jnp.concatenate(
            [jnp.broadcast_to(c[h:h + 1, :], (8, PAGE)) for h in range(FOX_HEADS)], axis=0)
        return _dot(qbd, kt) - bias, c[:, PAGE - 1:PAGE]

    def update(ss, vts):
        m_old = m_ref[...]
        m_new = m_old
        for s in ss:
            m_new = jnp.maximum(m_new, jnp.max(s, axis=-1, keepdims=True))
        alpha = jnp.exp(m_old - m_new)
        l_new = alpha * l_ref[...]
        pv = None
        for s, vt in zip(ss, vts):
            p = jnp.exp(s - m_new)
            l_new = l_new + jnp.sum(p, axis=-1, keepdims=True)
            d = _dot_nt(p, vt)
            pv = d if pv is None else pv + d
        m_ref[...] = m_new
        l_ref[...] = l_new
        acc_ref[...] = alpha * acc_ref[...] + pv

    carry = carry_ref[...]
    ss, vts = [], []
    for i in range(pp):
        s, carry = scores(k_refs[i][...].reshape(MIX_W, PAGE), lf_refs[i][...], carry)
        ss.append(s)
        vts.append(v_refs[i][...].reshape(MIX_W, PAGE))
    carry_ref[...] = carry
    update(ss, vts)

    @pl.when(j == pl.num_programs(1) - 1)
    def _():
        s, _ = scores(kn_ref[...].reshape(MIX_W, PAGE), lfn_ref[...], carry_ref[...])
        qpos = lax.broadcasted_iota(I32, (nrow, PAGE), 0) % 8
        kpos = lax.broadcasted_iota(I32, (nrow, PAGE), 1)
        s = jnp.where(qpos >= kpos, s, NEG)
        update([s], [vn_ref[...].reshape(MIX_W, PAGE)])
        o = jnp.where(diag, acc_ref[...] / l_ref[...], 0.0)
        out = o[0:8, :]
        for h in range(1, FOX_HEADS):
            out = out + o[h * 8:(h + 1) * 8, :]
        o_ref[...] = (out * _sigmoid(og_ref[...])).astype(o_ref.dtype)


def _dot_exact_rhs01_left(x, m01):
    a, b, c = _split3(x)
    return _dot(a, m01) + _dot(b, m01) + _dot(c, m01)


def _fox_decode(page_table, q, knt, vnt, lfnt, og, kt_pool, vt_pool, lf_pool, *, pages_per_step):
    nb, npages = page_table.shape
    pp = pages_per_step
    steps = npages // pp
    tri = jnp.triu(jnp.ones((PAGE, PAGE), BF16))
    pt = page_table.reshape(-1)

    def pool_spec(shape, i):
        return pl.BlockSpec((None,) + shape,
                            lambda b, j, pt_ref: (pt_ref[b * npages + j * pp + i],) + (0,) * len(shape))

    per_b = lambda shape: pl.BlockSpec((None,) + shape, lambda b, j, pt_ref: (b,) + (0,) * len(shape))
    rows8 = lambda width: pl.BlockSpec((8, width), lambda b, j, pt_ref: (b, 0))
    in_specs = ([rows8(MIX_W), per_b((FOX_HEADS, HEAD_D, PAGE)), per_b((FOX_HEADS, HEAD_D, PAGE)),
                 per_b((FOX_HEADS, PAGE)), rows8(MIX_W)]
                + [pool_spec((FOX_HEADS, HEAD_D, PAGE), i) for i in range(pp)]
                + [pool_spec((FOX_HEADS, HEAD_D, PAGE), i) for i in range(pp)]
                + [pool_spec((FOX_HEADS, PAGE), i) for i in range(pp)]
                + [pl.BlockSpec((PAGE, PAGE), lambda b, j, pt_ref: (0, 0))])
    nrow = FOX_HEADS * 8
    return pl.pallas_call(
        functools.partial(_fox_decode_body, pages_per_step=pp),
        out_shape=jax.ShapeDtypeStruct((nb * 8, MIX_W), F32),
        grid_spec=pltpu.PrefetchScalarGridSpec(
            num_scalar_prefetch=1,
            grid=(nb, steps),
            in_specs=in_specs,
            out_specs=pl.BlockSpec((8, MIX_W), lambda b, j, pt_ref: (b, 0)),
            scratch_shapes=[pltpu.VMEM((nrow, MIX_W), F32), pltpu.VMEM((nrow, 1), F32),
                            pltpu.VMEM((nrow, 1), F32), pltpu.VMEM((nrow, MIX_W), F32),
                            pltpu.VMEM((FOX_HEADS, 1), F32)]),
        compiler_params=_cparams(("arbitrary", "arbitrary")),
        name="fox_decode",
    )(pt, q, knt, vnt, lfnt, og, *([kt_pool] * pp), *([vt_pool] * pp), *([lf_pool] * pp), tri)


def _retention_body(q_ref, k_ref, v_ref, gate_ref, s0_ref, dmat_ref, qdec_ref, kdec_ref, gc_ref,
                    go_ref, mix_ref, state_ref, *, n_chunks):
    dmat = dmat_ref[...]
    qdec = qdec_ref[...]
    kdec = kdec_ref[...]
    gc = gc_ref[...]
    go = go_ref[...]

    def chunk(n, state):
        sl = pl.ds(pl.multiple_of(n * RET_D, RET_D), RET_D)
        q = q_ref[sl, :]
        k = k_ref[sl, :]
        v = v_ref[sl, :].astype(BF16)
        inner = _dot_nt(q.astype(BF16), k.astype(BF16)) * dmat
        o = _dot(inner.astype(BF16), v) + _dot((q * qdec).astype(BF16), state.astype(BF16))
        new_state = gc * state + _dot((k * kdec).T.astype(BF16), v)
        g = gate_ref[sl, :]
        y = _rms_rows(o, go) * (g * _sigmoid(g))
        mix_ref[sl, :] = y.astype(mix_ref.dtype)
        return new_state

    state_ref[...] = lax.fori_loop(0, n_chunks, chunk, s0_ref[...])


def _retention(q, k, v, gate, s0, tabs, go, *, batch, seq, out_dtype):
    n = q.shape[0]
    dmat, qdec, kdec, gc = tabs
    seq_blk = pl.BlockSpec((seq, RET_D), lambda b, h: (b, h))
    tab = pl.BlockSpec((None, RET_D, RET_D), lambda b, h: (h, 0, 0))
    st = pl.BlockSpec((None, None, RET_D, RET_D), lambda b, h: (b, h, 0, 0))
    return pl.pallas_call(
        functools.partial(_retention_body, n_chunks=seq // RET_D),
        out_shape=[jax.ShapeDtypeStruct((n, MIX_W), out_dtype),
                   jax.ShapeDtypeStruct((batch, RET_HEADS, RET_D, RET_D), F32)],
        grid=(batch, RET_HEADS),
        in_specs=[seq_blk, seq_blk, seq_blk, seq_blk, st, tab, tab, tab, tab,
                  pl.BlockSpec((1, RET_D), lambda b, h: (0, 0))],
        out_specs=[seq_blk, st],
        compiler_params=_cparams(("arbitrary", "arbitrary")),
        name="retention",
    )(q, k, v, gate, s0, dmat, qdec, kdec, gc, go)


def _retention_tables(chunk):
    log_g = jnp.log1p(-jnp.exp2(-5.0 - jnp.arange(RET_HEADS, dtype=F32)))
    n = jnp.arange(RET_D, dtype=F32)
    diff = n[:, None] - n[None, :]
    valid = (n[:, None] < chunk) & (n[None, :] < chunk)
    dmat = jnp.where((diff >= 0) & valid,
                     jnp.exp(jnp.maximum(diff, 0.0)[None] * log_g[:, None, None]), 0.0)
    qdec = jnp.exp((n + 1.0)[None, :] * log_g[:, None])
    kdec = jnp.where(n[None, :] < chunk, jnp.exp((chunk - 1.0 - n)[None, :] * log_g[:, None]), 0.0)
    gc = jnp.exp(chunk * log_g)
    bc = lambda a: jnp.broadcast_to(a[:, :, None], (RET_HEADS, RET_D, RET_D))
    return (dmat, bc(qdec), bc(kdec),
            jnp.broadcast_to(gc[:, None, None], (RET_HEADS, RET_D, RET_D)))


def _mem_kv_body(mem_ref, g_ref, w_ref, bd_ref, gk_ref, k_ref, v_ref):
    h = _rms_rows(mem_ref[...], g_ref[...]).astype(BF16)
    k_ref[...] = _head_norm(_dot(h, w_ref[:, :MEM_W]), bd_ref[...], gk_ref[...])
    v_ref[...] = _dot(h, w_ref[:, MEM_W:])


def _mem_kv(mem, g, w, bd, gk, *, tile):
    depth = w.shape[0]
    n = mem.shape[0]
    per_layer = lambda shape: pl.BlockSpec((None,) + shape, lambda l, i: (l,) + (0,) * len(shape))
    out = pl.BlockSpec((None, tile, MEM_W), lambda l, i: (l, i, 0))
    return pl.pallas_call(
        _mem_kv_body,
        out_shape=[jax.ShapeDtypeStruct((depth, n, MEM_W), F32)] * 2,
        grid=(depth, n // tile),
        in_specs=[pl.BlockSpec((tile, D_MODEL), lambda l, i: (i, 0)),
                  per_layer((1, D_MODEL)), per_layer((D_MODEL, 2 * MEM_W)),
                  pl.BlockSpec((256, 256), lambda l, i: (0, 0)), per_layer((1, MEM_W))],
        out_specs=[out, out],
        compiler_params=_cparams(("arbitrary", "arbitrary")),
        name="mem_kv",
    )(mem, g, w, bd, gk)


def _mem_attend_body(q_ref, k_ref, v_ref, o_ref):
    t = q_ref.shape[0]
    lane = lax.broadcasted_iota(I32, (t, LANES), 1)
    lo = lane < HEAD_D
    for slab in range(MEM_W // LANES):
        sl = slice(slab * LANES, (slab + 1) * LANES)
        q = q_ref[:, sl]
        k = k_ref[:, sl].astype(BF16)
        v = v_ref[:, sl].astype(BF16)
        zq = jnp.zeros_like(q)
        outs = []
        for qh in (jnp.where(lo, q, zq), jnp.where(lo, zq, q)):
            s = _dot_nt(qh, k)
            p = jnp.exp(s - jnp.max(s, axis=-1, keepdims=True))
            p = p / jnp.sum(p, axis=-1, keepdims=True)
            outs.append(_dot(p.astype(BF16), v))
        o_ref[:, sl] = jnp.where(lo, outs[0], outs[1]).astype(o_ref.dtype)


def _mem_attend(qm, mk, mv, *, batch, seq, tile):
    n = qm.shape[0]
    nt = seq // tile
    kv = pl.BlockSpec((N_MEM, MEM_W), lambda b, i: (b, 0))
    return pl.pallas_call(
        _mem_attend_body,
        out_shape=jax.ShapeDtypeStruct((n, MEM_W), BF16),
        grid=(batch, nt),
        in_specs=[pl.BlockSpec((tile, MEM_W), lambda b, i: (b * nt + i, 0)), kv, kv],
        out_specs=pl.BlockSpec((tile, MEM_W), lambda b, i: (b * nt + i, 0)),
        compiler_params=_cparams(("arbitrary", "arbitrary")),
        name="mem_attend",
    )(qm, mk, mv)


def _mem_attend_t_body(q_ref, kt_ref, vt_ref, o_ref):
    t = q_ref.shape[0]
    lane = lax.broadcasted_iota(I32, (t, LANES), 1)
    lo = lane < HEAD_D
    for slab in range(MEM_W // LANES):
        sl = slice(slab * LANES, (slab + 1) * LANES)
        q = q_ref[:, sl]
        kt = kt_ref[2 * slab:2 * slab + 2].reshape(LANES, N_MEM)
        vt = vt_ref[2 * slab:2 * slab + 2].reshape(LANES, N_MEM)
        zq = jnp.zeros_like(q)
        outs = []
        for qh in (jnp.where(lo, q, zq), jnp.where(lo, zq, q)):
            s = _dot(qh, kt)
            p = jnp.exp(s - jnp.max(s, axis=-1, keepdims=True))
            p = p / jnp.sum(p, axis=-1, keepdims=True)
            outs.append(_dot_nt(p, vt))
        o_ref[:, sl] = jnp.where(lo, outs[0], outs[1])


def _mem_attend_t(qm, mkt, mvt, *, batch, tokens):
    kv = pl.BlockSpec((None, MEM_W // HEAD_D, HEAD_D, N_MEM), lambda b: (b, 0, 0, 0))
    return pl.pallas_call(
        _mem_attend_t_body,
        out_shape=jax.ShapeDtypeStruct((batch * tokens, MEM_W), F32),
        grid=(batch,),
        in_specs=[pl.BlockSpec((tokens, MEM_W), lambda b: (b, 0)), kv, kv],
        out_specs=pl.BlockSpec((tokens, MEM_W), lambda b: (b, 0)),
        compiler_params=_cparams(("arbitrary",)),
        name="mem_attend_t",
    )(qm, mkt, mvt)


def _outproj_router_body(x_ref, mix_ref, memo_ref, wa_ref, wb_ref, g_ref, wr_hi_ref, wr_lo_ref,
                         br_ref, tri_ref, xo_ref, h_ref, ri_ref, rw_ref, cnt_ref, carry_ref):
    t = x_ref.shape[0]
    x = (x_ref[...] + _dot(mix_ref[...].astype(BF16), wa_ref[...])
         + _dot(memo_ref[...].astype(BF16), wb_ref[...]))
    xo_ref[...] = x
    h = _rms_rows(x, g_ref[...])
    h_ref[...] = h
    h_hi = h.astype(BF16)
    h_lo = (h - h_hi.astype(F32)).astype(BF16)
    logits = (_dot(h_hi, wr_hi_ref[...]) + _dot(h_lo, wr_hi_ref[...]) + _dot(h_hi, wr_lo_ref[...])
              + br_ref[...])
    lane = lax.broadcasted_iota(I32, (t, LANES), 1)
    big = jnp.int32(1 << 20)
    rmax = lambda a: jnp.max(a, axis=-1, keepdims=True)
    rmin = lambda a: jnp.min(a, axis=-1, keepdims=True)
    rsum = lambda a: jnp.sum(a, axis=-1, keepdims=True)
    gmask = lane < N_GROUPS
    lg = jnp.where(gmask, logits, -jnp.inf)
    gmax = rmax(lg)
    gi = rmin(jnp.where(lg == gmax, lane, big))
    p_top = 1.0 / rsum(jnp.where(gmask, jnp.exp(logits - gmax), 0.0))
    first = ROUTE_LANE0 + EPG * gi
    emask = (lane >= first) & (lane < first + EPG)
    le = jnp.where(emask, logits, -jnp.inf)
    v1 = rmax(le)
    i1 = rmin(jnp.where(le == v1, lane, big))
    le2 = jnp.where(lane == i1, -jnp.inf, le)
    v2 = rmax(le2)
    i2 = rmin(jnp.where(le2 == v2, lane, big))
    e2 = jnp.exp(v2 - v1)
    w1 = p_top / (1.0 + e2)
    w2 = p_top * e2 / (1.0 + e2)

    @pl.when(pl.program_id(0) == 0)
    def _():
        carry_ref[...] = jnp.zeros_like(carry_ref)

    onehot = ((lane == i1) | (lane == i2)).astype(F32)
    before = _dot(tri_ref[...], onehot.astype(BF16)) + carry_ref[...]
    r1 = rsum(jnp.where(lane == i1, before, 0.0)).astype(I32)
    r2 = rsum(jnp.where(lane == i2, before, 0.0)).astype(I32)
    carry_ref[...] = carry_ref[...] + jnp.sum(onehot, axis=0, keepdims=True)
    cnt_ref[...] = carry_ref[...]
    lane4 = lax.broadcasted_iota(I32, (t, 4), 1)
    ri_ref[...] = jnp.where(lane4 == 0, i1 - ROUTE_LANE0,
                            jnp.where(lane4 == 1, i2 - ROUTE_LANE0, jnp.where(lane4 == 2, r1, r2)))
    lane2 = lax.broadcasted_iota(I32, (t, 2), 1)
    rw_ref[...] = jnp.where(lane2 == 0, w1, w2)


def _outproj_router(x, mix, memo, wa, wb, g, wr_hi, wr_lo, br, *, tile):
    n = x.shape[0]
    tri = jnp.tril(jnp.ones((tile, tile), BF16), -1)
    full = lambda shape: pl.BlockSpec(shape, lambda i: (0,) * len(shape))
    row = lambda width: pl.BlockSpec((tile, width), lambda i: (i, 0))
    return pl.pallas_call(
        _outproj_router_body,
        out_shape=[jax.ShapeDtypeStruct((n, D_MODEL), F32), jax.ShapeDtypeStruct((n, D_MODEL), F32),
                   jax.ShapeDtypeStruct((n, 4), I32), jax.ShapeDtypeStruct((n, 2), F32),
                   jax.ShapeDtypeStruct((1, LANES), F32)],
        grid=(n // tile,),
        in_specs=[row(D_MODEL), row(MIX_W), row(MEM_W), full((MIX_W, D_MODEL)), full((MEM_W, D_MODEL)),
                  full((1, D_MODEL)), full((D_MODEL, LANES)), full((D_MODEL, LANES)), full((1, LANES)),
                  full((tile, tile))],
        out_specs=[row(D_MODEL), row(D_MODEL), row(4), row(2), full((1, LANES))],
        scratch_shapes=[pltpu.VMEM((1, LANES), F32)],
        compiler_params=_cparams(("arbitrary",)),
        name="outproj_router",
    )(x, mix, memo, wa, wb, g, wr_hi, wr_lo, br, tri)


def _row_copy(src_hbm, src_row, dst_hbm, dst_row, sem):
    return pltpu.make_async_copy(src_hbm.at[pl.ds(src_row, 1), :], dst_hbm.at[pl.ds(dst_row, 1), :], sem)


def _moe_dispatch_body(eid_ref, rank_ref, off_ref, h_hbm, zeros_hbm, xs_hbm, sem, *, tile):
    del zeros_hbm
    base = pl.program_id(0) * tile

    def issue(t, c):
        tok = base + t
        for slot in range(2):
            a = 2 * tok + slot
            _row_copy(h_hbm, tok, xs_hbm, off_ref[eid_ref[a]] + rank_ref[a], sem).start()
        return c

    lax.fori_loop(0, tile, issue, 0)

    def drain(t, c):
        _row_copy(h_hbm, 0, xs_hbm, 0, sem).wait()
        return c

    lax.fori_loop(0, 2 * tile, drain, 0)


def _moe_dispatch(eid, rank, off, h, *, rows, tile):
    n = h.shape[0]
    any_spec = pl.BlockSpec(memory_space=pl.ANY)
    return pl.pallas_call(
        functools.partial(_moe_dispatch_body, tile=tile),
        out_shape=jax.ShapeDtypeStruct((rows, D_MODEL), F32),
        grid_spec=pltpu.PrefetchScalarGridSpec(
            num_scalar_prefetch=3, grid=(n // tile,),
            in_specs=[any_spec, any_spec], out_specs=any_spec,
            scratch_shapes=[pltpu.SemaphoreType.DMA(())]),
        input_output_aliases={4: 0},
        compiler_params=_cparams(("arbitrary",)),
        name="moe_dispatch",
    )(eid, rank, off, h, jnp.zeros((rows, D_MODEL), F32))


def _moe_experts_body(te_ref, nu_ref, x_ref, w1_ref, w3_ref, w2_ref, y_ref):
    del te_ref

    @pl.when(pl.program_id(0) < nu_ref[0])
    def _():
        x = x_ref[...].astype(BF16)
        a = _dot(x, w1_ref[...].astype(BF16))
        b = _dot(x, w3_ref[...].astype(BF16))
        hid = a * _sigmoid(a) * b
        y_ref[...] = _dot(hid.astype(BF16), w2_ref[...].astype(BF16))

    @pl.when(pl.program_id(0) >= nu_ref[0])
    def _():
        y_ref[...] = jnp.zeros_like(y_ref)


def _moe_experts(tile_expert, n_used, xs, w1, w3, w2, *, tile):
    rows = xs.shape[0]
    wspec = lambda shape: pl.BlockSpec((None,) + shape, lambda i, te, nu: (te[i], 0, 0))
    return pl.pallas_call(
        _moe_experts_body,
        out_shape=jax.ShapeDtypeStruct((rows, D_MODEL), F32),
        grid_spec=pltpu.PrefetchScalarGridSpec(
            num_scalar_prefetch=2, grid=(rows // tile,),
            in_specs=[pl.BlockSpec((tile, D_MODEL), lambda i, te, nu: (i, 0)),
                      wspec((D_MODEL, D_EXPERT)), wspec((D_MODEL, D_EXPERT)),
                      wspec((D_EXPERT, D_MODEL))],
            out_specs=pl.BlockSpec((tile, D_MODEL), lambda i, te, nu: (i, 0))),
        compiler_params=_cparams(("arbitrary",)),
        name="moe_experts",
    )(tile_expert, n_used, xs, w1, w3, w2)


def _moe_combine_body(eid_ref, rank_ref, off_ref, x_ref, rw_ref, ys_hbm, o_ref, buf_ref, sem, *, tile):
    base = pl.program_id(0) * tile

    def issue(t, c):
        for slot in range(2):
            a = 2 * (base + t) + slot
            src = off_ref[eid_ref[a]] + rank_ref[a]
            pltpu.make_async_copy(ys_hbm.at[pl.ds(src, 1), :], buf_ref.at[slot, pl.ds(t, 1), :], sem).start()
        return c

    lax.fori_loop(0, tile, issue, 0)

    def drain(t, c):
        pltpu.make_async_copy(ys_hbm.at[pl.ds(0, 1), :], buf_ref.at[0, pl.ds(0, 1), :], sem).wait()
        return c

    lax.fori_loop(0, 2 * tile, drain, 0)
    rw = rw_ref[...]
    o_ref[...] = x_ref[...] + rw[:, 0:1] * buf_ref[0] + rw[:, 1:2] * buf_ref[1]


def _moe_combine(eid, rank, off, x, rw, ys, *, tile):
    n = x.shape[0]
    row = lambda width: pl.BlockSpec((tile, width), lambda i, *_: (i, 0))
    return pl.pallas_call(
        functools.partial(_moe_combine_body, tile=tile),
        out_shape=jax.ShapeDtypeStruct((n, D_MODEL), F32),
        grid_spec=pltpu.PrefetchScalarGridSpec(
            num_scalar_prefetch=3, grid=(n // tile,),
            in_specs=[row(D_MODEL), row(2), pl.BlockSpec(memory_space=pl.ANY)],
            out_specs=row(D_MODEL),
            scratch_shapes=[pltpu.VMEM((2, tile, D_MODEL), F32), pltpu.SemaphoreType.DMA(())]),
        compiler_params=_cparams(("arbitrary",)),
        name="moe_combine",
    )(eid, rank, off, x, rw, ys)


def _moe(x, h, ri, rw, cnt, w1, w3, w2, *, expert_tile, token_tile):
    n = x.shape[0]
    n_tiles = (2 * n + N_EXPERTS * (expert_tile - 1)) // expert_tile + 1
    rows = n_tiles * expert_tile
    counts = cnt[0, ROUTE_LANE0:ROUTE_LANE0 + N_EXPERTS].astype(I32)
    padded = (counts + expert_tile - 1) // expert_tile * expert_tile
    ends = jnp.cumsum(padded)
    off = ends - padded
    n_used = (ends[-1] // expert_tile).reshape(1)
    tile_expert = jnp.minimum(
        jnp.searchsorted(ends, jnp.arange(n_tiles, dtype=I32) * expert_tile, side="right"),
        N_EXPERTS - 1).astype(I32)
    eid = ri[:, 0:2].reshape(-1)
    rank = ri[:, 2:4].reshape(-1)
    xs = _moe_dispatch(eid, rank, off, h, rows=rows, tile=token_tile)
    ys = _moe_experts(tile_expert, n_used, xs, w1, w3, w2, tile=expert_tile)
    return _moe_combine(eid, rank, off, x, rw, ys, tile=token_tile)


def _block_diag_ones():
    r = jnp.arange(256) // HEAD_D
    return (r[:, None] == r[None, :]).astype(BF16)


def _tile_gain(g, reps):
    return jnp.tile(g.astype(F32), reps)[None, :]


def _rotary_tables(pos):
    half = RET_D // 2
    inv = 10000.0 ** (-jnp.arange(half, dtype=F32) / half)
    ang = pos.astype(F32)[:, None] * inv[None, :]
    cos, sin = jnp.cos(ang), jnp.sin(ang)
    return jnp.concatenate([cos, cos], axis=-1), jnp.concatenate([-sin, sin], axis=-1)


def kernel(x_prompt, x_sample, mem_prompt, cache_fox_k, cache_fox_v, cache_fox_logf, page_table,
           state_ret, cache_mem_k, cache_mem_v, w_in_fox, b_f_fox, g_q_fox, g_k_fox, w_in_ret, g_o_ret,
           g_attn, g_mem, w_mem_kv, g_mq, g_mk, w_out, g_ffn, w_rg, b_rg, w_re, b_re, w1, w3, w2):
    bp, sp, _ = x_prompt.shape
    bs, ts, _ = x_sample.shape
    depth = w_out.shape[0]
    n_pages = page_table.shape[1]
    past = n_pages * PAGE
    np_, ns = bp * sp, bs * ts
    tile_p, tile_s = 512, ns
    bd = _block_diag_ones()

    xp = x_prompt.reshape(np_, D_MODEL)
    xs = x_sample.reshape(ns, D_MODEL)
    mem = mem_prompt.reshape(bp * N_MEM, D_MODEL)
    mk_all, mv_all = _mem_kv(mem, g_mem[:, None, :], w_mem_kv.astype(BF16), bd,
                             jnp.tile(g_mk, (1, MEM_W // HEAD_D))[:, None, :], tile=512)

    outs = {}
    for l in range(depth):
        i = l // 2
        g_l = g_attn[l][None, :]
        gmq = _tile_gain(g_mq[l], 256 // HEAD_D)
        if l % 2 == 0:
            w = w_in_fox[i]
            w_pack = jnp.concatenate(
                [w[:, :4 * MIX_W], w[:, 4 * MIX_W + FOX_HEADS:], w[:, 4 * MIX_W:4 * MIX_W + FOX_HEADS],
                 jnp.zeros((D_MODEL, LANES - FOX_HEADS), F32)], axis=1).astype(BF16)
            bf = jnp.concatenate([b_f_fox[i], jnp.zeros((LANES - FOX_HEADS,), F32)])[None, :]
            gq = _tile_gain(g_q_fox[i], 256 // HEAD_D)
            gk = _tile_gain(g_k_fox[i], 256 // HEAD_D)
            fox = functools.partial(_fox_inproj, g=g_l, w=w_pack, bd=bd, gq=gq, gk=gk, gmq=gmq, bf=bf)
            qp, kp, kbp, vp, vbp, ogp, qmp, lfp, cp = fox(
                xp, tile=tile_p, tiles_per_seq=sp // tile_p, act_dtype=BF16)
            qs, ks, _, vs, _, ogs, qms, lfs, _ = fox(xs, tile=tile_s, tiles_per_seq=1, act_dtype=F32)
            c_row = jnp.transpose(cp.reshape(bp, sp, FOX_HEADS), (0, 2, 1))[:, :, None, :]
            mix_p = _fox_flash(qp, kbp, vbp, cp, c_row, ogp, batch=bp, seq=sp, blk=256)

            def pad_t(a):
                a = jnp.transpose(a.reshape(bs, ts, -1), (0, 2, 1))
                return jnp.pad(a, ((0, 0), (0, 0), (0, PAGE - ts)))

            knt = pad_t(ks).reshape(bs, FOX_HEADS, HEAD_D, PAGE)
            vnt = pad_t(vs).reshape(bs, FOX_HEADS, HEAD_D, PAGE)
            lfnt = pad_t(lfs)
            kt_pool = jnp.transpose(cache_fox_k[i], (0, 2, 3, 1))
            vt_pool = jnp.transpose(cache_fox_v[i], (0, 2, 3, 1))
            lf_pool = jnp.transpose(cache_fox_logf[i], (0, 2, 1))
            mix_s = _fox_decode(page_table, qs, knt, vnt, lfnt, ogs, kt_pool, vt_pool, lf_pool,
                                pages_per_step=4)
            outs.setdefault("fkp", []).append(kp.reshape(bp, sp, FOX_HEADS, HEAD_D))
            outs.setdefault("fvp", []).append(vp.reshape(bp, sp, FOX_HEADS, HEAD_D))
            outs.setdefault("flp", []).append(lfp.reshape(bp, sp, FOX_HEADS))
            outs.setdefault("fks", []).append(ks.reshape(bs, ts, FOX_HEADS, HEAD_D))
            outs.setdefault("fvs", []).append(vs.reshape(bs, ts, FOX_HEADS, HEAD_D))
            outs.setdefault("fls", []).append(lfs.reshape(bs, ts, FOX_HEADS))
        else:
            w = w_in_ret[i].astype(BF16)
            go = g_o_ret[i][None, :]
            cos_p, sin_p = _rotary_tables(jnp.arange(sp))
            cos_s, sin_s = _rotary_tables(past + jnp.arange(ts))
            ret = functools.partial(_ret_inproj, g=g_l, w=w, bd=bd, gmq=gmq)
            qp, kp, vp, gp, qmp = ret(xp, cos=cos_p, sin=sin_p, tile=tile_p,
                                      tiles_per_seq=sp // tile_p, act_dtype=BF16)
            qs, ks, vs, gs, qms = ret(xs, cos=jnp.tile(cos_s, (bs, 1)), sin=jnp.tile(sin_s, (bs, 1)),
                                      tile=tile_s, tiles_per_seq=1, act_dtype=F32)
            mix_p, sp_state = _retention(
                qp, kp, vp, gp, jnp.zeros((bp, RET_HEADS, RET_D, RET_D), F32),
                _retention_tables(RET_D), go, batch=bp, seq=sp, out_dtype=BF16)
            pad_rows = lambda a: jnp.pad(a.reshape(bs, ts, MIX_W),
                                         ((0, 0), (0, RET_D - ts), (0, 0))).reshape(bs * RET_D, MIX_W)
            mix_s_pad, ss_state = _retention(
                pad_rows(qs), pad_rows(ks), pad_rows(vs), pad_rows(gs), state_ret[i],
                _retention_tables(ts), go, batch=bs, seq=RET_D, out_dtype=F32)
            mix_s = mix_s_pad.reshape(bs, RET_D, MIX_W)[:, :ts].reshape(ns, MIX_W)
            outs.setdefault("rsp", []).append(sp_state)
            outs.setdefault("rss", []).append(ss_state)

        memo_p = _mem_attend(qmp, mk_all[l], mv_all[l], batch=bp, seq=sp, tile=tile_p)
        memo_s = _mem_attend_t(qms, jnp.transpose(cache_mem_k[l], (0, 2, 3, 1)),
                               jnp.transpose(cache_mem_v[l], (0, 2, 3, 1)), batch=bs, tokens=ts)

        wo = w_out[l].astype(BF16)
        wr = jnp.concatenate([w_rg[l], w_re[l], jnp.zeros((D_MODEL, LANES - N_GROUPS - N_EXPERTS), F32)], axis=1)
        wr_hi = wr.astype(BF16)
        wr_lo = (wr - wr_hi.astype(F32)).astype(BF16)
        br = jnp.concatenate([b_rg[l], b_re[l], jnp.zeros((LANES - N_GROUPS - N_EXPERTS,), F32)])[None, :]
        route = functools.partial(_outproj_router, wa=wo[:MIX_W], wb=wo[MIX_W:], g=g_ffn[l][None, :],
                                  wr_hi=wr_hi, wr_lo=wr_lo, br=br)
        xp, hp, rip, rwp, cntp = route(xp, mix_p, memo_p, tile=tile_p)
        xs, hs, ris, rws, cnts = route(xs, mix_s, memo_s, tile=tile_s)
        xp = _moe(xp, hp, rip, rwp, cntp, w1[l], w3[l], w2[l], expert_tile=256, token_tile=tile_p)
        xs = _moe(xs, hs, ris, rws, cnts, w1[l], w3[l], w2[l], expert_tile=128, token_tile=tile_s)

    mem_shape = (depth, bp, N_MEM, MEM_W // HEAD_D, HEAD_D)
    return (xp.reshape(bp, sp, D_MODEL), xs.reshape(bs, ts, D_MODEL),
            jnp.stack(outs["fkp"]), jnp.stack(outs["fvp"]), jnp.stack(outs["flp"]),
            jnp.stack(outs["fks"]), jnp.stack(outs["fvs"]), jnp.stack(outs["fls"]),
            jnp.stack(outs["rsp"]), jnp.stack(outs["rss"]),
            mk_all.reshape(mem_shape), mv_all.reshape(mem_shape))
```

```python
import functools

import jax
import jax.numpy as jnp
from jax import lax
from jax.experimental import pallas as pl
from jax.experimental.pallas import tpu as pltpu

F32 = jnp.float32
BF16 = jnp.bfloat16
I32 = jnp.int32

D_MODEL = 1024
MIX_W = 768
MEM_W = 256
HEAD_D = 64
FOX_HEADS = 12
RET_HEADS = 6
RET_D = 128
N_MEM = 256
N_EXPERTS = 32
N_GROUPS = 4
EPG = 8
D_EXPERT = 256
PAGE = 128
EPS = 1e-6
NEG = -1e30
LANES = 128
VMEM_LIMIT = 56 * 1024 * 1024
ROUTE_LANE0 = N_GROUPS
LOG2E = 1.4426950408889634
SUBLANES = 8


def _cparams(sem, vmem=VMEM_LIMIT):
    return pltpu.CompilerParams(dimension_semantics=sem, vmem_limit_bytes=vmem)


def _dot(a, b):
    return jnp.dot(a, b, preferred_element_type=F32)


def _dot_nt(a, b):
    return lax.dot_general(a, b, (((1,), (1,)), ((), ())), preferred_element_type=F32)


def _split3(x):
    a = x.astype(BF16)
    r = x - a.astype(F32)
    b = r.astype(BF16)
    c = (r - b.astype(F32)).astype(BF16)
    return a, b, c


def _dot_exact_rhs01(m01, x):
    a, b, c = _split3(x)
    return _dot(m01, a) + _dot(m01, b) + _dot(m01, c)


def _rms_rows(x, gain):
    return x * lax.rsqrt(jnp.mean(x * x, axis=-1, keepdims=True) + EPS) * gain


def _head_norm(y, bd, gain):
    ss = _dot((y * y).astype(BF16), bd)
    return y * lax.rsqrt(ss * (1.0 / HEAD_D) + EPS) * gain


def _sigmoid(x):
    return 1.0 / (1.0 + jnp.exp(-x))


def _fox_inproj_body(x_ref, g_ref, w_ref, bd_ref, gq_ref, gk_ref, gmq_ref, bf_ref, tri_ref,
                     q_ref, k_ref, kb_ref, v_ref, vb_ref, og_ref, qm_ref, lf_ref, c_ref,
                     carry_ref, *, tiles_per_seq, q_scale):
    t = x_ref.shape[0]
    h = _rms_rows(x_ref[...], g_ref[...]).astype(BF16)
    bd = bd_ref[...]
    for j in range(3):
        sl = slice(j * 256, (j + 1) * 256)
        y = _dot(h, w_ref[:, sl])
        q_ref[:, sl] = (_head_norm(y, bd, gq_ref[...]) * q_scale).astype(q_ref.dtype)
    for j in range(3):
        sl = slice(j * 256, (j + 1) * 256)
        y = _dot(h, w_ref[:, MIX_W + j * 256:MIX_W + (j + 1) * 256])
        kn = _head_norm(y, bd, gk_ref[...])
        k_ref[:, sl] = kn
        kb_ref[:, sl] = kn.astype(BF16)
    for j in range(3):
        sl = slice(j * 256, (j + 1) * 256)
        y = _dot(h, w_ref[:, 2 * MIX_W + j * 256:2 * MIX_W + (j + 1) * 256])
        v_ref[:, sl] = y
        vb_ref[:, sl] = y.astype(BF16)
    for j in range(3):
        sl = slice(j * 256, (j + 1) * 256)
        og_ref[:, sl] = _dot(h, w_ref[:, 3 * MIX_W + j * 256:3 * MIX_W + (j + 1) * 256])
    y = _dot(h, w_ref[:, 4 * MIX_W:4 * MIX_W + MEM_W])
    qm_ref[...] = (_head_norm(y, bd, gmq_ref[...]) * 0.125).astype(qm_ref.dtype)
    f = _dot(h, w_ref[:, 4 * MIX_W + MEM_W:]) + bf_ref[...]
    lf = jnp.minimum(f, 0.0) - jnp.log1p(jnp.exp(-jnp.abs(f)))
    lane = lax.broadcasted_iota(I32, (t, LANES), 1)
    lf = jnp.where(lane < FOX_HEADS, lf, 0.0)
    lf_ref[...] = lf[:, :FOX_HEADS]

    @pl.when(pl.program_id(0) % tiles_per_seq == 0)
    def _():
        carry_ref[...] = jnp.zeros_like(carry_ref)

    c = _dot_exact_rhs01(tri_ref[...], lf) + carry_ref[...]
    c_ref[...] = c[:, :FOX_HEADS]
    carry_ref[...] = c[t - 1:t, :]


def _fox_inproj(x, g, w, bd, gq, gk, gmq, bf, *, tile, tiles_per_seq, act_dtype, q_scale):
    n = x.shape[0]
    tri = jnp.tril(jnp.ones((tile, tile), BF16))
    full = lambda shape: pl.BlockSpec(shape, lambda i: (0,) * len(shape))
    row = lambda width: pl.BlockSpec((tile, width), lambda i: (i, 0))
    outs = [
        jax.ShapeDtypeStruct((n, MIX_W), act_dtype),
        jax.ShapeDtypeStruct((n, MIX_W), F32),
        jax.ShapeDtypeStruct((n, MIX_W), BF16),
        jax.ShapeDtypeStruct((n, MIX_W), F32),
        jax.ShapeDtypeStruct((n, MIX_W), BF16),
        jax.ShapeDtypeStruct((n, MIX_W), F32),
        jax.ShapeDtypeStruct((n, MEM_W), act_dtype),
        jax.ShapeDtypeStruct((n, FOX_HEADS), F32),
        jax.ShapeDtypeStruct((n, FOX_HEADS), F32),
    ]
    return pl.pallas_call(
        functools.partial(_fox_inproj_body, tiles_per_seq=tiles_per_seq, q_scale=q_scale),
        out_shape=outs,
        grid=(n // tile,),
        in_specs=[row(D_MODEL), full((1, D_MODEL)), full(w.shape), full((256, 256)),
                  full((1, 256)), full((1, 256)), full((1, 256)), full((1, LANES)),
                  full((tile, tile))],
        out_specs=[row(MIX_W), row(MIX_W), row(MIX_W), row(MIX_W), row(MIX_W), row(MIX_W),
                   row(MEM_W), row(FOX_HEADS), row(FOX_HEADS)],
        scratch_shapes=[pltpu.VMEM((1, LANES), F32)],
        compiler_params=_cparams(("arbitrary",)),
        name="fox_inproj",
    )(x, g, w, bd, gq, gk, gmq, bf, tri)


def _ret_inproj_body(x_ref, g_ref, w_ref, bd_ref, gmq_ref, cos_ref, sin_ref,
                     q_ref, k_ref, v_ref, gate_ref, qm_ref):
    h = _rms_rows(x_ref[...], g_ref[...]).astype(BF16)
    cos = cos_ref[...]
    sin = sin_ref[...]
    for j in range(RET_HEADS):
        sl = slice(j * RET_D, (j + 1) * RET_D)
        y = _dot(h, w_ref[:, sl])
        q_ref[:, sl] = y * cos + pltpu.roll(y, RET_D // 2, 1) * sin
    for j in range(RET_HEADS):
        sl = slice(j * RET_D, (j + 1) * RET_D)
        y = _dot(h, w_ref[:, MIX_W + j * RET_D:MIX_W + (j + 1) * RET_D])
        k_ref[:, sl] = (y * cos + pltpu.roll(y, RET_D // 2, 1) * sin) * (RET_D ** -0.5)
    for j in range(3):
        sl = slice(j * 256, (j + 1) * 256)
        v_ref[:, sl] = _dot(h, w_ref[:, 2 * MIX_W + j * 256:2 * MIX_W + (j + 1) * 256])
    for j in range(3):
        sl = slice(j * 256, (j + 1) * 256)
        gate_ref[:, sl] = _dot(h, w_ref[:, 3 * MIX_W + j * 256:3 * MIX_W + (j + 1) * 256])
    y = _dot(h, w_ref[:, 4 * MIX_W:])
    qm_ref[...] = (_head_norm(y, bd_ref[...], gmq_ref[...]) * 0.125).astype(qm_ref.dtype)


def _ret_inproj(x, g, w, bd, gmq, cos, sin, *, tile, tiles_per_seq, act_dtype):
    n = x.shape[0]
    full = lambda shape: pl.BlockSpec(shape, lambda i: (0,) * len(shape))
    row = lambda width: pl.BlockSpec((tile, width), lambda i: (i, 0))
    tab = pl.BlockSpec((tile, RET_D), lambda i: (i % tiles_per_seq, 0))
    outs = [jax.ShapeDtypeStruct((n, MIX_W), F32)] * 4 + [jax.ShapeDtypeStruct((n, MEM_W), act_dtype)]
    return pl.pallas_call(
        _ret_inproj_body,
        out_shape=outs,
        grid=(n // tile,),
        in_specs=[row(D_MODEL), full((1, D_MODEL)), full(w.shape), full((256, 256)),
                  full((1, 256)), tab, tab],
        out_specs=[row(MIX_W)] * 4 + [row(MEM_W)],
        compiler_params=_cparams(("arbitrary",)),
        name="ret_inproj",
    )(x, g, w, bd, gmq, cos, sin)


def _fox_flash_body(q_ref, k_ref, v_ref, ccol_ref, crow_ref, og_ref, o_ref, *, blk, pairs):
    g = pl.program_id(1)
    qi = pl.program_id(2)
    heads = 2 * pairs
    lane = lax.broadcasted_iota(I32, (blk, LANES), 1)
    lo = lane < HEAD_D
    cc = ccol_ref[...] * LOG2E
    lane12 = lax.broadcasted_iota(I32, cc.shape, 1)
    qs, ccol = [], []
    for h in range(heads):
        q = q_ref[:, (h // 2) * LANES:(h // 2 + 1) * LANES]
        keep = lo if h % 2 == 0 else jnp.logical_not(lo)
        qs.append(jnp.where(keep, q, jnp.zeros_like(q)))
        ccol.append(jnp.sum(jnp.where(lane12 == g * heads + h, cc, 0.0), axis=-1, keepdims=True))
    rows = lax.broadcasted_iota(I32, (blk, blk), 0)
    cols = lax.broadcasted_iota(I32, (blk, blk), 1)
    causal = rows >= cols

    def step(start, carry, diag):
        ms, ls, accs = carry
        new_m, new_l, new_acc = [], [], []
        for p in range(pairs):
            sl = slice(p * LANES, (p + 1) * LANES)
            k = k_ref[pl.ds(start, blk), sl]
            v = v_ref[pl.ds(start, blk), sl]
            alphas, pvs = [], []
            for j in range(2):
                h = 2 * p + j
                s = _dot_nt(qs[h], k) + ccol[h] - crow_ref[h:h + 1, pl.ds(start, blk)] * LOG2E
                if diag:
                    s = jnp.where(causal, s, NEG)
                m_new = jnp.maximum(ms[h], jnp.max(s, axis=-1, keepdims=True))
                alpha = jnp.exp2(ms[h] - m_new)
                e = jnp.exp2(s - m_new)
                new_l.append(alpha * ls[h] + jnp.sum(e, axis=-1, keepdims=True))
                new_m.append(m_new)
                alphas.append(alpha)
                pvs.append(_dot(e.astype(BF16), v))
            new_acc.append(accs[p] * jnp.where(lo, alphas[0], alphas[1]) + jnp.where(lo, pvs[0], pvs[1]))
        return tuple(new_m), tuple(new_l), tuple(new_acc)

    init = ((jnp.full((blk, 1), NEG, F32),) * heads, (jnp.zeros((blk, 1), F32),) * heads,
            (jnp.zeros((blk, LANES), F32),) * pairs)
    carry = lax.fori_loop(
        0, qi, lambda i, c: step(pl.multiple_of(i * blk, blk), c, False), init)
    _, ls, accs = step(pl.multiple_of(qi * blk, blk), carry, True)
    for p in range(pairs):
        sl = slice(p * LANES, (p + 1) * LANES)
        o = accs[p] / jnp.where(lo, ls[2 * p], ls[2 * p + 1])
        o_ref[:, sl] = (o * _sigmoid(og_ref[:, sl])).astype(o_ref.dtype)


def _fox_flash(q, kb, vb, c, c_row, og, *, batch, seq, blk, pairs):
    n = q.shape[0]
    nq = seq // blk
    w = pairs * LANES
    qblk = pl.BlockSpec((blk, w), lambda b, g, i: (b * nq + i, g))
    kvblk = pl.BlockSpec((seq, w), lambda b, g, i: (b, g))
    return pl.pallas_call(
        functools.partial(_fox_flash_body, blk=blk, pairs=pairs),
        out_shape=jax.ShapeDtypeStruct((n, MIX_W), BF16),
        grid=(batch, FOX_HEADS // (2 * pairs), nq),
        in_specs=[
            qblk, kvblk, kvblk,
            pl.BlockSpec((blk, FOX_HEADS), lambda b, g, i: (b * nq + i, 0)),
            pl.BlockSpec((None, None, 2 * pairs, seq), lambda b, g, i: (b, g, 0, 0)),
            qblk,
        ],
        out_specs=qblk,
        compiler_params=_cparams(("arbitrary",) * 3),
        name="fox_flash",
    )(q, kb, vb, c, c_row, og)


def _fox_decode_body(pt_ref, q_ref, kn_ref, vn_ref, lfn_ref, og_ref, *rest, pages_per_step):
    del pt_ref
    pp = pages_per_step
    k_refs = rest[:pp]
    v_refs = rest[pp:2 * pp]
    lf_refs = rest[2 * pp:3 * pp]
    tri_ref = rest[3 * pp]
    o_ref = rest[3 * pp + 1]
    qbd_ref, m_ref, l_ref, acc_ref, carry_ref = rest[3 * pp + 2:]
    j = pl.program_id(1)
    nrow = FOX_HEADS * 8
    rowh = lax.broadcasted_iota(I32, (nrow, MIX_W), 0) // 8
    colh = lax.broadcasted_iota(I32, (nrow, MIX_W), 1) // HEAD_D
    diag = rowh == colh

    @pl.when(j == 0)
    def _():
        q = q_ref[...]
        qbd_ref[...] = jnp.where(diag, jnp.concatenate([q] * FOX_HEADS, axis=0), 0.0)
        m_ref[...] = jnp.full_like(m_ref, NEG)
        l_ref[...] = jnp.zeros_like(l_ref)
        acc_ref[...] = jnp.zeros_like(acc_ref)
        carry_ref[...] = jnp.zeros_like(carry_ref)

    qbd = qbd_ref[...]
    tri = tri_ref[...]

    def scores(kt, lf, carry):
        c = _dot_exact_rhs01_left(lf, tri) + carry
        bias = jnp.concatenate(
            [jnp.broadcast_to(c[h:h + 1, :], (8, PAGE)) for h in range(FOX_HEADS)], axis=0)
        return _dot(qbd, kt) - bias, c[:, PAGE - 1:PAGE]

    def update(ss, vts):
        m_old = m_ref[...]
        m_new = m_old
        for s in ss:
            m_new = jnp.maximum(m_new, jnp.max(s, axis=-1, keepdims=True))
        alpha = jnp.exp(m_old - m_new)
        l_new = alpha * l_ref[...]
        pv = None
        for s, vt in zip(ss, vts):
            p = jnp.exp(s - m_new)
            l_new = l_new + jnp.sum(p, axis=-1, keepdims=True)
            d = _dot_nt(p, vt)
            pv = d if pv is None else pv + d
        m_ref[...] = m_new
        l_ref[...] = l_new
        acc_ref[...] = alpha * acc_ref[...] + pv

    carry = carry_ref[...]
    ss, vts = [], []
    for i in range(pp):
        s, carry = scores(k_refs[i][...].reshape(MIX_W, PAGE), lf_refs[i][...], carry)
        ss.append(s)
        vts.append(v_refs[i][...].reshape(MIX_W, PAGE))
    carry_ref[...] = carry
    update(ss, vts)

    @pl.when(j == pl.num_programs(1) - 1)
    def _():
        s, _ = scores(kn_ref[...].reshape(MIX_W, PAGE), lfn_ref[...], carry_ref[...])
        qpos = lax.broadcasted_iota(I32, (nrow, PAGE), 0) % 8
        kpos = lax.broadcasted_iota(I32, (nrow, PAGE), 1)
        s = jnp.where(qpos >= kpos, s, NEG)
        update([s], [vn_ref[...].reshape(MIX_W, PAGE)])
        o = jnp.where(diag, acc_ref[...] / l_ref[...], 0.0)
        out = o[0:8, :]
        for h in range(1, FOX_HEADS):
            out = out + o[h * 8:(h + 1) * 8, :]
        o_ref[...] = (out * _sigmoid(og_ref[...])).astype(o_ref.dtype)


def _dot_exact_rhs01_left(x, m01):
    a, b, c = _split3(x)
    return _dot(a, m01) + _dot(b, m01) + _dot(c, m01)


def _fox_decode(page_table, q, knt, vnt, lfnt, og, kt_pool, vt_pool, lf_pool, *, pages_per_step):
    nb, npages = page_table.shape
    pp = pages_per_step
    steps = npages // pp
    tri = jnp.triu(jnp.ones((PAGE, PAGE), BF16))
    pt = page_table.reshape(-1)

    def pool_spec(shape, i):
        return pl.BlockSpec((None,) + shape,
                            lambda b, j, pt_ref: (pt_ref[b * npages + j * pp + i],) + (0,) * len(shape))

    per_b = lambda shape: pl.BlockSpec((None,) + shape, lambda b, j, pt_ref: (b,) + (0,) * len(shape))
    rows8 = lambda width: pl.BlockSpec((8, width), lambda b, j, pt_ref: (b, 0))
    in_specs = ([rows8(MIX_W), per_b((FOX_HEADS, HEAD_D, PAGE)), per_b((FOX_HEADS, HEAD_D, PAGE)),
                 per_b((FOX_HEADS, PAGE)), rows8(MIX_W)]
                + [pool_spec((FOX_HEADS, HEAD_D, PAGE), i) for i in range(pp)]
                + [pool_spec((FOX_HEADS, HEAD_D, PAGE), i) for i in range(pp)]
                + [pool_spec((FOX_HEADS, PAGE), i) for i in range(pp)]
                + [pl.BlockSpec((PAGE, PAGE), lambda b, j, pt_ref: (0, 0))])
    nrow = FOX_HEADS * 8
    return pl.pallas_call(
        functools.partial(_fox_decode_body, pages_per_step=pp),
        out_shape=jax.ShapeDtypeStruct((nb * 8, MIX_W), F32),
        grid_spec=pltpu.PrefetchScalarGridSpec(
            num_scalar_prefetch=1,
            grid=(nb, steps),
            in_specs=in_specs,
            out_specs=pl.BlockSpec((8, MIX_W), lambda b, j, pt_ref: (b, 0)),
            scratch_shapes=[pltpu.VMEM((nrow, MIX_W), F32), pltpu.VMEM((nrow, 1), F32),
                            pltpu.VMEM((nrow, 1), F32), pltpu.VMEM((nrow, MIX_W), F32),
                            pltpu.VMEM((FOX_HEADS, 1), F32)]),
        compiler_params=_cparams(("arbitrary", "arbitrary")),
        name="fox_decode",
    )(pt, q, knt, vnt, lfnt, og, *([kt_pool] * pp), *([vt_pool] * pp), *([lf_pool] * pp), tri)


def _retention_body(q_ref, k_ref, v_ref, gate_ref, s0_ref, dmat_ref, qdec_ref, kdec_ref, gc_ref,
                    go_ref, mix_ref, state_ref, *, n_chunks):
    dmat = dmat_ref[...]
    qdec = qdec_ref[...]
    kdec = kdec_ref[...]
    gc = gc_ref[...]
    go = go_ref[...]

    def chunk(n, state):
        sl = pl.ds(pl.multiple_of(n * RET_D, RET_D), RET_D)
        q = q_ref[sl, :]
        k = k_ref[sl, :]
        v = v_ref[sl, :].astype(BF16)
        inner = _dot_nt(q.astype(BF16), k.astype(BF16)) * dmat
        o = _dot(inner.astype(BF16), v) + _dot((q * qdec).astype(BF16), state.astype(BF16))
        new_state = gc * state + _dot((k * kdec).T.astype(BF16), v)
        g = gate_ref[sl, :]
        y = _rms_rows(o, go) * (g * _sigmoid(g))
        mix_ref[sl, :] = y.astype(mix_ref.dtype)
        return new_state

    state_ref[...] = lax.fori_loop(0, n_chunks, chunk, s0_ref[...])


def _retention(q, k, v, gate, s0, tabs, go, *, batch, seq, out_dtype):
    n = q.shape[0]
    dmat, qdec, kdec, gc = tabs
    seq_blk = pl.BlockSpec((seq, RET_D), lambda b, h: (b, h))
    tab = pl.BlockSpec((None, RET_D, RET_D), lambda b, h: (h, 0, 0))
    st = pl.BlockSpec((None, None, RET_D, RET_D), lambda b, h: (b, h, 0, 0))
    return pl.pallas_call(
        functools.partial(_retention_body, n_chunks=seq // RET_D),
        out_shape=[jax.ShapeDtypeStruct((n, MIX_W), out_dtype),
                   jax.ShapeDtypeStruct((batch, RET_HEADS, RET_D, RET_D), F32)],
        grid=(batch, RET_HEADS),
        in_specs=[seq_blk, seq_blk, seq_blk, seq_blk, st, tab, tab, tab, tab,
                  pl.BlockSpec((1, RET_D), lambda b, h: (0, 0))],
        out_specs=[seq_blk, st],
        compiler_params=_cparams(("arbitrary", "arbitrary")),
        name="retention",
    )(q, k, v, gate, s0, dmat, qdec, kdec, gc, go)


def _retention_tables(chunk):
    log_g = jnp.log1p(-jnp.exp2(-5.0 - jnp.arange(RET_HEADS, dtype=F32)))
    n = jnp.arange(RET_D, dtype=F32)
    diff = n[:, None] - n[None, :]
    valid = (n[:, None] < chunk) & (n[None, :] < chunk)
    dmat = jnp.where((diff >= 0) & valid,
                     jnp.exp(jnp.maximum(diff, 0.0)[None] * log_g[:, None, None]), 0.0)
    qdec = jnp.exp((n + 1.0)[None, :] * log_g[:, None])
    kdec = jnp.where(n[None, :] < chunk, jnp.exp((chunk - 1.0 - n)[None, :] * log_g[:, None]), 0.0)
    gc = jnp.exp(chunk * log_g)
    bc = lambda a: jnp.broadcast_to(a[:, :, None], (RET_HEADS, RET_D, RET_D))
    return (dmat, bc(qdec), bc(kdec),
            jnp.broadcast_to(gc[:, None, None], (RET_HEADS, RET_D, RET_D)))


def _mem_kv_body(mem_ref, g_ref, w_ref, bd_ref, gk_ref, k_ref, v_ref):
    h = _rms_rows(mem_ref[...], g_ref[...]).astype(BF16)
    k_ref[...] = _head_norm(_dot(h, w_ref[:, :MEM_W]), bd_ref[...], gk_ref[...])
    v_ref[...] = _dot(h, w_ref[:, MEM_W:])


def _mem_kv(mem, g, w, bd, gk, *, tile):
    depth = w.shape[0]
    n = mem.shape[0]
    per_layer = lambda shape: pl.BlockSpec((None,) + shape, lambda l, i: (l,) + (0,) * len(shape))
    out = pl.BlockSpec((None, tile, MEM_W), lambda l, i: (l, i, 0))
    return pl.pallas_call(
        _mem_kv_body,
        out_shape=[jax.ShapeDtypeStruct((depth, n, MEM_W), F32)] * 2,
        grid=(depth, n // tile),
        in_specs=[pl.BlockSpec((tile, D_MODEL), lambda l, i: (i, 0)),
                  per_layer((1, D_MODEL)), per_layer((D_MODEL, 2 * MEM_W)),
                  pl.BlockSpec((256, 256), lambda l, i: (0, 0)), per_layer((1, MEM_W))],
        out_specs=[out, out],
        compiler_params=_cparams(("arbitrary", "arbitrary")),
        name="mem_kv",
    )(mem, g, w, bd, gk)


def _mem_attend_body(q_ref, k_ref, v_ref, o_ref):
    t = q_ref.shape[0]
    lane = lax.broadcasted_iota(I32, (t, LANES), 1)
    lo = lane < HEAD_D
    for slab in range(MEM_W // LANES):
        sl = slice(slab * LANES, (slab + 1) * LANES)
        q = q_ref[:, sl]
        k = k_ref[:, sl].astype(BF16)
        v = v_ref[:, sl].astype(BF16)
        zq = jnp.zeros_like(q)
        outs = []
        for qh in (jnp.where(lo, q, zq), jnp.where(lo, zq, q)):
            s = _dot_nt(qh, k)
            p = jnp.exp(s - jnp.max(s, axis=-1, keepdims=True))
            p = p / jnp.sum(p, axis=-1, keepdims=True)
            outs.append(_dot(p.astype(BF16), v))
        o_ref[:, sl] = jnp.where(lo, outs[0], outs[1]).astype(o_ref.dtype)


def _mem_attend(qm, mk, mv, *, batch, seq, tile):
    n = qm.shape[0]
    nt = seq // tile
    kv = pl.BlockSpec((N_MEM, MEM_W), lambda b, i: (b, 0))
    return pl.pallas_call(
        _mem_attend_body,
        out_shape=jax.ShapeDtypeStruct((n, MEM_W), BF16),
        grid=(batch, nt),
        in_specs=[pl.BlockSpec((tile, MEM_W), lambda b, i: (b * nt + i, 0)), kv, kv],
        out_specs=pl.BlockSpec((tile, MEM_W), lambda b, i: (b * nt + i, 0)),
        compiler_params=_cparams(("arbitrary", "arbitrary")),
        name="mem_attend",
    )(qm, mk, mv)


def _mem_attend_t_body(q_ref, kt_ref, vt_ref, o_ref):
    t = q_ref.shape[0]
    lane = lax.broadcasted_iota(I32, (t, LANES), 1)
    lo = lane < HEAD_D
    for slab in range(MEM_W // LANES):
        sl = slice(slab * LANES, (slab + 1) * LANES)
        q = q_ref[:, sl]
        kt = kt_ref[2 * slab:2 * slab + 2].reshape(LANES, N_MEM)
        vt = vt_ref[2 * slab:2 * slab + 2].reshape(LANES, N_MEM)
        zq = jnp.zeros_like(q)
        outs = []
        for qh in (jnp.where(lo, q, zq), jnp.where(lo, zq, q)):
            s = _dot(qh, kt)
            p = jnp.exp(s - jnp.max(s, axis=-1, keepdims=True))
            p = p / jnp.sum(p, axis=-1, keepdims=True)
            outs.append(_dot_nt(p, vt))
        o_ref[:, sl] = jnp.where(lo, outs[0], outs[1])


def _mem_attend_t(qm, mkt, mvt, *, batch, tokens):
    kv = pl.BlockSpec((None, MEM_W // HEAD_D, HEAD_D, N_MEM), lambda b: (b, 0, 0, 0))
    return pl.pallas_call(
        _mem_attend_t_body,
        out_shape=jax.ShapeDtypeStruct((batch * tokens, MEM_W), F32),
        grid=(batch,),
        in_specs=[pl.BlockSpec((tokens, MEM_W), lambda b: (b, 0)), kv, kv],
        out_specs=pl.BlockSpec((tokens, MEM_W), lambda b: (b, 0)),
        compiler_params=_cparams(("arbitrary",)),
        name="mem_attend_t",
    )(qm, mkt, mvt)


def _outproj_router_body(x_ref, mix_ref, memo_ref, wa_ref, wb_ref, g_ref, wr_hi_ref, wr_lo_ref,
                         br_ref, tri_ref, xo_ref, h_ref, ri_ref, rw_ref, cnt_ref, carry_ref):
    t = x_ref.shape[0]
    x = (x_ref[...] + _dot(mix_ref[...].astype(BF16), wa_ref[...])
         + _dot(memo_ref[...].astype(BF16), wb_ref[...]))
    xo_ref[...] = x
    h = _rms_rows(x, g_ref[...])
    _to_token_major(h_ref, h)
    h_hi = h.astype(BF16)
    h_lo = (h - h_hi.astype(F32)).astype(BF16)
    logits = (_dot(h_hi, wr_hi_ref[...]) + _dot(h_lo, wr_hi_ref[...]) + _dot(h_hi, wr_lo_ref[...])
              + br_ref[...])
    lane = lax.broadcasted_iota(I32, (t, LANES), 1)
    big = jnp.int32(1 << 20)
    rmax = lambda a: jnp.max(a, axis=-1, keepdims=True)
    rmin = lambda a: jnp.min(a, axis=-1, keepdims=True)
    rsum = lambda a: jnp.sum(a, axis=-1, keepdims=True)
    gmask = lane < N_GROUPS
    lg = jnp.where(gmask, logits, -jnp.inf)
    gmax = rmax(lg)
    gi = rmin(jnp.where(lg == gmax, lane, big))
    p_top = 1.0 / rsum(jnp.where(gmask, jnp.exp(logits - gmax), 0.0))
    first = ROUTE_LANE0 + EPG * gi
    emask = (lane >= first) & (lane < first + EPG)
    le = jnp.where(emask, logits, -jnp.inf)
    v1 = rmax(le)
    i1 = rmin(jnp.where(le == v1, lane, big))
    le2 = jnp.where(lane == i1, -jnp.inf, le)
    v2 = rmax(le2)
    i2 = rmin(jnp.where(le2 == v2, lane, big))
    e2 = jnp.exp(v2 - v1)
    w1 = p_top / (1.0 + e2)
    w2 = p_top * e2 / (1.0 + e2)

    @pl.when(pl.program_id(0) == 0)
    def _():
        carry_ref[...] = jnp.zeros_like(carry_ref)

    onehot = ((lane == i1) | (lane == i2)).astype(F32)
    before = _dot(tri_ref[...], onehot.astype(BF16)) + carry_ref[...]
    r1 = rsum(jnp.where(lane == i1, before, 0.0)).astype(I32)
    r2 = rsum(jnp.where(lane == i2, before, 0.0)).astype(I32)
    carry_ref[...] = carry_ref[...] + jnp.sum(onehot, axis=0, keepdims=True)
    cnt_ref[...] = carry_ref[...]
    lane4 = lax.broadcasted_iota(I32, (t, 4), 1)
    ri_ref[...] = jnp.where(lane4 == 0, i1 - ROUTE_LANE0,
                            jnp.where(lane4 == 1, i2 - ROUTE_LANE0, jnp.where(lane4 == 2, r1, r2)))
    lane2 = lax.broadcasted_iota(I32, (t, 2), 1)
    rw_ref[...] = jnp.where(lane2 == 0, w1, w2)


def _outproj_router(x, mix, memo, wa, wb, g, wr_hi, wr_lo, br, *, tile):
    n = x.shape[0]
    tri = jnp.tril(jnp.ones((tile, tile), BF16), -1)
    full = lambda shape: pl.BlockSpec(shape, lambda i: (0,) * len(shape))
    row = lambda width: pl.BlockSpec((tile, width), lambda i: (i, 0))
    return pl.pallas_call(
        _outproj_router_body,
        out_shape=[jax.ShapeDtypeStruct((n, D_MODEL), F32),
                   jax.ShapeDtypeStruct((n * TOK_ROWS, LANES), F32),
                   jax.ShapeDtypeStruct((n, 4), I32), jax.ShapeDtypeStruct((n, 2), F32),
                   jax.ShapeDtypeStruct((1, LANES), F32)],
        grid=(n // tile,),
        in_specs=[row(D_MODEL), row(MIX_W), row(MEM_W), full((MIX_W, D_MODEL)), full((MEM_W, D_MODEL)),
                  full((1, D_MODEL)), full((D_MODEL, LANES)), full((D_MODEL, LANES)), full((1, LANES)),
                  full((tile, tile))],
        out_specs=[row(D_MODEL), pl.BlockSpec((tile * TOK_ROWS, LANES), lambda i: (i, 0)),
                   row(4), row(2), full((1, LANES))],
        scratch_shapes=[pltpu.VMEM((1, LANES), F32)],
        compiler_params=_cparams(("arbitrary",)),
        name="outproj_router",
    )(x, mix, memo, wa, wb, g, wr_hi, wr_lo, br, tri)


TOK_ROWS = D_MODEL // LANES
DMA_UNROLL = 8


def _to_token_major(ref, x):
    t = x.shape[0]
    for s in range(TOK_ROWS):
        ref[pl.ds(s, t, stride=TOK_ROWS), :] = x[:, s * LANES:(s + 1) * LANES]


def _from_token_major(ref, t):
    return [ref[pl.ds(s, t, stride=TOK_ROWS), :] for s in range(TOK_ROWS)]


def _token_tile(ref, row):
    return ref.at[pl.ds(pl.multiple_of(row * TOK_ROWS, TOK_ROWS), TOK_ROWS), :]


def _moe_dispatch_body(eid_ref, rank_ref, off_ref, h_ref, zeros_hbm, xs_hbm, sem, *, tile):
    del zeros_hbm
    base = pl.program_id(0) * tile

    def issue(t, c):
        for slot in range(2):
            a = 2 * (base + t) + slot
            dst = off_ref[eid_ref[a]] + rank_ref[a]
            pltpu.make_async_copy(_token_tile(h_ref, t), _token_tile(xs_hbm, dst), sem).start()
        return c

    lax.fori_loop(0, tile, issue, 0, unroll=DMA_UNROLL)

    def drain(t, c):
        pltpu.make_async_copy(_token_tile(h_ref, 0), _token_tile(xs_hbm, 0), sem).wait()
        return c

    lax.fori_loop(0, 2 * tile, drain, 0, unroll=DMA_UNROLL)


def _moe_dispatch(eid, rank, off, h_tm, *, rows, tile):
    n = h_tm.shape[0] // TOK_ROWS
    any_spec = pl.BlockSpec(memory_space=pl.ANY)
    return pl.pallas_call(
        functools.partial(_moe_dispatch_body, tile=tile),
        out_shape=jax.ShapeDtypeStruct((rows * TOK_ROWS, LANES), F32),
        grid_spec=pltpu.PrefetchScalarGridSpec(
            num_scalar_prefetch=3, grid=(n // tile,),
            in_specs=[pl.BlockSpec((tile * TOK_ROWS, LANES), lambda i, *_: (i, 0)), any_spec],
            out_specs=any_spec,
            scratch_shapes=[pltpu.SemaphoreType.DMA(())]),
        input_output_aliases={4: 0},
        compiler_params=_cparams(("arbitrary",)),
        name="moe_dispatch",
    )(eid, rank, off, h_tm, jnp.zeros((rows * TOK_ROWS, LANES), F32))


def _moe_experts_body(te_ref, nu_ref, x_ref, w1_ref, w3_ref, w2_ref, y_ref, *, tile):
    del te_ref

    @pl.when(pl.program_id(0) < nu_ref[0])
    def _():
        x = jnp.concatenate(_from_token_major(x_ref, tile), axis=1).astype(BF16)
        a = _dot(x, w1_ref[...].astype(BF16))
        b = _dot(x, w3_ref[...].astype(BF16))
        hid = a * _sigmoid(a) * b
        _to_token_major(y_ref, _dot(hid.astype(BF16), w2_ref[...].astype(BF16)))

    @pl.when(pl.program_id(0) >= nu_ref[0])
    def _():
        y_ref[...] = jnp.zeros_like(y_ref)


def _moe_experts(tile_expert, n_used, xs, w1, w3, w2, *, tile):
    rows = xs.shape[0] // TOK_ROWS
    wspec = lambda shape: pl.BlockSpec((None,) + shape, lambda i, te, nu: (te[i], 0, 0))
    blk = pl.BlockSpec((tile * TOK_ROWS, LANES), lambda i, te, nu: (i, 0))
    return pl.pallas_call(
        functools.partial(_moe_experts_body, tile=tile),
        out_shape=jax.ShapeDtypeStruct(xs.shape, F32),
        grid_spec=pltpu.PrefetchScalarGridSpec(
            num_scalar_prefetch=2, grid=(rows // tile,),
            in_specs=[blk, wspec((D_MODEL, D_EXPERT)), wspec((D_MODEL, D_EXPERT)),
                      wspec((D_EXPERT, D_MODEL))],
            out_specs=blk),
        compiler_params=_cparams(("arbitrary",)),
        name="moe_experts",
    )(tile_expert, n_used, xs, w1, w3, w2)


def _moe_combine_body(eid_ref, rank_ref, off_ref, x_ref, rw_ref, ys_hbm, o_ref, buf_ref, sem, *, tile):
    base = pl.program_id(0) * tile

    def issue(t, c):
        for slot in range(2):
            a = 2 * (base + t) + slot
            src = off_ref[eid_ref[a]] + rank_ref[a]
            pltpu.make_async_copy(_token_tile(ys_hbm, src), _token_tile(buf_ref.at[slot], t), sem).start()
        return c

    lax.fori_loop(0, tile, issue, 0, unroll=DMA_UNROLL)

    def drain(t, c):
        pltpu.make_async_copy(_token_tile(ys_hbm, 0), _token_tile(buf_ref.at[0], 0), sem).wait()
        return c

    lax.fori_loop(0, 2 * tile, drain, 0, unroll=DMA_UNROLL)
    rw = rw_ref[...]
    y0 = _from_token_major(buf_ref.at[0], tile)
    y1 = _from_token_major(buf_ref.at[1], tile)
    for s in range(TOK_ROWS):
        sl = slice(s * LANES, (s + 1) * LANES)
        o_ref[:, sl] = x_ref[:, sl] + rw[:, 0:1] * y0[s] + rw[:, 1:2] * y1[s]


def _moe_combine(eid, rank, off, x, rw, ys, *, tile):
    n = x.shape[0]
    row = lambda width: pl.BlockSpec((tile, width), lambda i, *_: (i, 0))
    return pl.pallas_call(
        functools.partial(_moe_combine_body, tile=tile),
        out_shape=jax.ShapeDtypeStruct((n, D_MODEL), F32),
        grid_spec=pltpu.PrefetchScalarGridSpec(
            num_scalar_prefetch=3, grid=(n // tile,),
            in_specs=[row(D_MODEL), row(2), pl.BlockSpec(memory_space=pl.ANY)],
            out_specs=row(D_MODEL),
            scratch_shapes=[pltpu.VMEM((2, tile * TOK_ROWS, LANES), F32), pltpu.SemaphoreType.DMA(())]),
        compiler_params=_cparams(("arbitrary",)),
        name="moe_combine",
    )(eid, rank, off, x, rw, ys)


def _moe(x, h, ri, rw, cnt, w1, w3, w2, *, expert_tile, token_tile):
    n = x.shape[0]
    n_tiles = (2 * n + N_EXPERTS * (expert_tile - 1)) // expert_tile + 1
    rows = n_tiles * expert_tile
    counts = cnt[0, ROUTE_LANE0:ROUTE_LANE0 + N_EXPERTS].astype(I32)
    padded = (counts + expert_tile - 1) // expert_tile * expert_tile
    ends = jnp.cumsum(padded)
    off = ends - padded
    n_used = (ends[-1] // expert_tile).reshape(1)
    tile_start = jnp.arange(n_tiles, dtype=I32) * expert_tile
    tile_expert = jnp.minimum(
        jnp.sum((tile_start[:, None] >= ends[None, :]).astype(I32), axis=1), N_EXPERTS - 1)
    eid = ri[:, 0:2].reshape(-1)
    rank = ri[:, 2:4].reshape(-1)
    xs = _moe_dispatch(eid, rank, off, h, rows=rows, tile=token_tile)
    ys = _moe_experts(tile_expert, n_used, xs, w1, w3, w2, tile=expert_tile)
    return _moe_combine(eid, rank, off, x, rw, ys, tile=token_tile)


def _block_diag_ones():
    r = jnp.arange(256) // HEAD_D
    return (r[:, None] == r[None, :]).astype(BF16)


def _tile_gain(g, reps):
    return jnp.tile(g.astype(F32), reps)[None, :]


def _rotary_tables(pos):
    half = RET_D // 2
    inv = 10000.0 ** (-jnp.arange(half, dtype=F32) / half)
    ang = pos.astype(F32)[:, None] * inv[None, :]
    cos, sin = jnp.cos(ang), jnp.sin(ang)
    return jnp.concatenate([cos, cos], axis=-1), jnp.concatenate([-sin, sin], axis=-1)


def kernel(x_prompt, x_sample, mem_prompt, cache_fox_k, cache_fox_v, cache_fox_logf, page_table,
           state_ret, cache_mem_k, cache_mem_v, w_in_fox, b_f_fox, g_q_fox, g_k_fox, w_in_ret, g_o_ret,
           g_attn, g_mem, w_mem_kv, g_mq, g_mk, w_out, g_ffn, w_rg, b_rg, w_re, b_re, w1, w3, w2):
    bp, sp, _ = x_prompt.shape
    bs, ts, _ = x_sample.shape
    depth = w_out.shape[0]
    n_pages = page_table.shape[1]
    past = n_pages * PAGE
    np_, ns = bp * sp, bs * ts
    tile_p, tile_s = 512, ns
    bd = _block_diag_ones()

    xp = x_prompt.reshape(np_, D_MODEL)
    xs = x_sample.reshape(ns, D_MODEL)
    mem = mem_prompt.reshape(bp * N_MEM, D_MODEL)
    mk_all, mv_all = _mem_kv(mem, g_mem[:, None, :], w_mem_kv.astype(BF16), bd,
                             jnp.tile(g_mk, (1, MEM_W // HEAD_D))[:, None, :], tile=512)

    outs = {}
    for l in range(depth):
        i = l // 2
        g_l = g_attn[l][None, :]
        gmq = _tile_gain(g_mq[l], 256 // HEAD_D)
        if l % 2 == 0:
            w = w_in_fox[i]
            w_pack = jnp.concatenate(
                [w[:, :4 * MIX_W], w[:, 4 * MIX_W + FOX_HEADS:], w[:, 4 * MIX_W:4 * MIX_W + FOX_HEADS],
                 jnp.zeros((D_MODEL, LANES - FOX_HEADS), F32)], axis=1).astype(BF16)
            bf = jnp.concatenate([b_f_fox[i], jnp.zeros((LANES - FOX_HEADS,), F32)])[None, :]
            gq = _tile_gain(g_q_fox[i], 256 // HEAD_D)
            gk = _tile_gain(g_k_fox[i], 256 // HEAD_D)
            fox = functools.partial(_fox_inproj, g=g_l, w=w_pack, bd=bd, gq=gq, gk=gk, gmq=gmq, bf=bf)
            qp, kp, kbp, vp, vbp, ogp, qmp, lfp, cp = fox(
                xp, tile=tile_p, tiles_per_seq=sp // tile_p, act_dtype=BF16,
                q_scale=HEAD_D ** -0.5 * LOG2E)
            qs, ks, _, vs, _, ogs, qms, lfs, _ = fox(xs, tile=tile_s, tiles_per_seq=1, act_dtype=F32,
                                                     q_scale=HEAD_D ** -0.5)
            pairs = 3
            c_row = jnp.transpose(cp.reshape(bp, sp, FOX_HEADS), (0, 2, 1)).reshape(
                bp, FOX_HEADS // (2 * pairs), 2 * pairs, sp)
            mix_p = _fox_flash(qp, kbp, vbp, cp, c_row, ogp, batch=bp, seq=sp, blk=256, pairs=pairs)

            def pad_t(a):
                a = jnp.transpose(a.reshape(bs, ts, -1), (0, 2, 1))
                return jnp.pad(a, ((0, 0), (0, 0), (0, PAGE - ts)))

            knt = pad_t(ks).reshape(bs, FOX_HEADS, HEAD_D, PAGE)
            vnt = pad_t(vs).reshape(bs, FOX_HEADS, HEAD_D, PAGE)
            lfnt = pad_t(lfs)
            kt_pool = jnp.transpose(cache_fox_k[i], (0, 2, 3, 1))
            vt_pool = jnp.transpose(cache_fox_v[i], (0, 2, 3, 1))
            lf_pool = jnp.transpose(cache_fox_logf[i], (0, 2, 1))
            mix_s = _fox_decode(page_table, qs, knt, vnt, lfnt, ogs, kt_pool, vt_pool, lf_pool,
                                pages_per_step=8)
            outs.setdefault("fkp", []).append(kp.reshape(bp, sp, FOX_HEADS, HEAD_D))
            outs.setdefault("fvp", []).append(vp.reshape(bp, sp, FOX_HEADS, HEAD_D))
            outs.setdefault("flp", []).append(lfp.reshape(bp, sp, FOX_HEADS))
            outs.setdefault("fks", []).append(ks.reshape(bs, ts, FOX_HEADS, HEAD_D))
            outs.setdefault("fvs", []).append(vs.reshape(bs, ts, FOX_HEADS, HEAD_D))
            outs.setdefault("fls", []).append(lfs.reshape(bs, ts, FOX_HEADS))
        else:
            w = w_in_ret[i].astype(BF16)
            go = g_o_ret[i][None, :]
            cos_p, sin_p = _rotary_tables(jnp.arange(sp))
            cos_s, sin_s = _rotary_tables(past + jnp.arange(ts))
            ret = functools.partial(_ret_inproj, g=g_l, w=w, bd=bd, gmq=gmq)
            qp, kp, vp, gp, qmp = ret(xp, cos=cos_p, sin=sin_p, tile=tile_p,
                                      tiles_per_seq=sp // tile_p, act_dtype=BF16)
            qs, ks, vs, gs, qms = ret(xs, cos=jnp.tile(cos_s, (bs, 1)), sin=jnp.tile(sin_s, (bs, 1)),
                                      tile=tile_s, tiles_per_seq=1, act_dtype=F32)
            mix_p, sp_state = _retention(
                qp, kp, vp, gp, jnp.zeros((bp, RET_HEADS, RET_D, RET_D), F32),
                _retention_tables(RET_D), go, batch=bp, seq=sp, out_dtype=BF16)
            pad_rows = lambda a: jnp.pad(a.reshape(bs, ts, MIX_W),
                                         ((0, 0), (0, RET_D - ts), (0, 0))).reshape(bs * RET_D, MIX_W)
            mix_s_pad, ss_state = _retention(
                pad_rows(qs), pad_rows(ks), pad_rows(vs), pad_rows(gs), state_ret[i],
                _retention_tables(ts), go, batch=bs, seq=RET_D, out_dtype=F32)
            mix_s = mix_s_pad.reshape(bs, RET_D, MIX_W)[:, :ts].reshape(ns, MIX_W)
            outs.setdefault("rsp", []).append(sp_state)
            outs.setdefault("rss", []).append(ss_state)

        memo_p = _mem_attend(qmp, mk_all[l], mv_all[l], batch=bp, seq=sp, tile=tile_p)
        memo_s = _mem_attend_t(qms, jnp.transpose(cache_mem_k[l], (0, 2, 3, 1)),
                               jnp.transpose(cache_mem_v[l], (0, 2, 3, 1)), batch=bs, tokens=ts)

        wo = w_out[l].astype(BF16)
        wr = jnp.concatenate([w_rg[l], w_re[l], jnp.zeros((D_MODEL, LANES - N_GROUPS - N_EXPERTS), F32)], axis=1)
        wr_hi = wr.astype(BF16)
        wr_lo = (wr - wr_hi.astype(F32)).astype(BF16)
        br = jnp.concatenate([b_rg[l], b_re[l], jnp.zeros((LANES - N_GROUPS - N_EXPERTS,), F32)])[None, :]
        route = functools.partial(_outproj_router, wa=wo[:MIX_W], wb=wo[MIX_W:], g=g_ffn[l][None, :],
                                  wr_hi=wr_hi, wr_lo=wr_lo, br=br)
        xp, hp, rip, rwp, cntp = route(xp, mix_p, memo_p, tile=tile_p)
        xs, hs, ris, rws, cnts = route(xs, mix_s, memo_s, tile=tile_s)
        xp = _moe(xp, hp, rip, rwp, cntp, w1[l], w3[l], w2[l], expert_tile=256, token_tile=tile_p)
        xs = _moe(xs, hs, ris, rws, cnts, w1[l], w3[l], w2[l], expert_tile=128, token_tile=tile_s)

    mem_shape = (depth, bp, N_MEM, MEM_W // HEAD_D, HEAD_D)
    return (xp.reshape(bp, sp, D_MODEL), xs.reshape(bs, ts, D_MODEL),
            jnp.stack(outs["fkp"]), jnp.stack(outs["fvp"]), jnp.stack(outs["flp"]),
            jnp.stack(outs["fks"]), jnp.stack(outs["fvs"]), jnp.stack(outs["fls"]),
            jnp.stack(outs["rsp"]), jnp.stack(outs["rss"]),
            mk_all.reshape(mem_shape), mv_all.reshape(mem_shape))
```

```python
import functools

import jax
import jax.numpy as jnp
from jax import lax
from jax.experimental import pallas as pl
from jax.experimental.pallas import tpu as pltpu

F32 = jnp.float32
BF16 = jnp.bfloat16
I32 = jnp.int32

D_MODEL = 1024
MIX_W = 768
MEM_W = 256
HEAD_D = 64
FOX_HEADS = 12
RET_HEADS = 6
RET_D = 128
N_MEM = 256
N_EXPERTS = 32
N_GROUPS = 4
EPG = 8
D_EXPERT = 256
PAGE = 128
EPS = 1e-6
NEG = -1e30
LANES = 128
VMEM_LIMIT = 56 * 1024 * 1024
ROUTE_LANE0 = N_GROUPS
LOG2E = 1.4426950408889634
SUBLANES = 8


def _cparams(sem, vmem=VMEM_LIMIT):
    return pltpu.CompilerParams(dimension_semantics=sem, vmem_limit_bytes=vmem)


def _dot(a, b):
    return jnp.dot(a, b, preferred_element_type=F32)


def _dot_nt(a, b):
    return lax.dot_general(a, b, (((1,), (1,)), ((), ())), preferred_element_type=F32)


def _split3(x):
    a = x.astype(BF16)
    r = x - a.astype(F32)
    b = r.astype(BF16)
    c = (r - b.astype(F32)).astype(BF16)
    return a, b, c


def _dot_exact_rhs01(m01, x):
    a, b, c = _split3(x)
    return _dot(m01, a) + _dot(m01, b) + _dot(m01, c)


def _rms_rows(x, gain):
    return x * lax.rsqrt(jnp.mean(x * x, axis=-1, keepdims=True) + EPS) * gain


def _head_norm(y, bd, gain):
    ss = _dot((y * y).astype(BF16), bd)
    return y * lax.rsqrt(ss * (1.0 / HEAD_D) + EPS) * gain


def _sigmoid(x):
    return 1.0 / (1.0 + jnp.exp(-x))


def _fox_inproj_body(x_ref, g_ref, w_ref, bd_ref, gq_ref, gk_ref, gmq_ref, bf_ref, tri_ref,
                     q_ref, k_ref, kb_ref, v_ref, vb_ref, og_ref, qm_ref, lf_ref, c_ref,
                     carry_ref, *, tiles_per_seq, q_scale):
    t = x_ref.shape[0]
    h = _rms_rows(x_ref[...], g_ref[...]).astype(BF16)
    bd = bd_ref[...]
    for j in range(3):
        sl = slice(j * 256, (j + 1) * 256)
        y = _dot(h, w_ref[:, sl])
        q_ref[:, sl] = (_head_norm(y, bd, gq_ref[...]) * q_scale).astype(q_ref.dtype)
    for j in range(3):
        sl = slice(j * 256, (j + 1) * 256)
        y = _dot(h, w_ref[:, MIX_W + j * 256:MIX_W + (j + 1) * 256])
        kn = _head_norm(y, bd, gk_ref[...])
        k_ref[:, sl] = kn
        kb_ref[:, sl] = kn.astype(BF16)
    for j in range(3):
        sl = slice(j * 256, (j + 1) * 256)
        y = _dot(h, w_ref[:, 2 * MIX_W + j * 256:2 * MIX_W + (j + 1) * 256])
        v_ref[:, sl] = y
        vb_ref[:, sl] = y.astype(BF16)
    for j in range(3):
        sl = slice(j * 256, (j + 1) * 256)
        og_ref[:, sl] = _dot(h, w_ref[:, 3 * MIX_W + j * 256:3 * MIX_W + (j + 1) * 256])
    y = _dot(h, w_ref[:, 4 * MIX_W:4 * MIX_W + MEM_W])
    qm_ref[...] = (_head_norm(y, bd, gmq_ref[...]) * 0.125).astype(qm_ref.dtype)
    f = _dot(h, w_ref[:, 4 * MIX_W + MEM_W:]) + bf_ref[...]
    lf = jnp.minimum(f, 0.0) - jnp.log1p(jnp.exp(-jnp.abs(f)))
    lane = lax.broadcasted_iota(I32, (t, LANES), 1)
    lf = jnp.where(lane < FOX_HEADS, lf, 0.0)
    lf_ref[...] = lf[:, :FOX_HEADS]

    @pl.when(pl.program_id(0) % tiles_per_seq == 0)
    def _():
        carry_ref[...] = jnp.zeros_like(carry_ref)

    c = _dot_exact_rhs01(tri_ref[...], lf) + carry_ref[...]
    c_ref[...] = c[:, :FOX_HEADS]
    carry_ref[...] = c[t - 1:t, :]


def _fox_inproj(x, g, w, bd, gq, gk, gmq, bf, *, tile, tiles_per_seq, act_dtype, q_scale):
    n = x.shape[0]
    tri = jnp.tril(jnp.ones((tile, tile), BF16))
    full = lambda shape: pl.BlockSpec(shape, lambda i: (0,) * len(shape))
    row = lambda width: pl.BlockSpec((tile, width), lambda i: (i, 0))
    outs = [
        jax.ShapeDtypeStruct((n, MIX_W), act_dtype),
        jax.ShapeDtypeStruct((n, MIX_W), F32),
        jax.ShapeDtypeStruct((n, MIX_W), BF16),
        jax.ShapeDtypeStruct((n, MIX_W), F32),
        jax.ShapeDtypeStruct((n, MIX_W), BF16),
        jax.ShapeDtypeStruct((n, MIX_W), F32),
        jax.ShapeDtypeStruct((n, MEM_W), act_dtype),
        jax.ShapeDtypeStruct((n, FOX_HEADS), F32),
        jax.ShapeDtypeStruct((n, FOX_HEADS), F32),
    ]
    return pl.pallas_call(
        functools.partial(_fox_inproj_body, tiles_per_seq=tiles_per_seq, q_scale=q_scale),
        out_shape=outs,
        grid=(n // tile,),
        in_specs=[row(D_MODEL), full((1, D_MODEL)), full(w.shape), full((256, 256)),
                  full((1, 256)), full((1, 256)), full((1, 256)), full((1, LANES)),
                  full((tile, tile))],
        out_specs=[row(MIX_W), row(MIX_W), row(MIX_W), row(MIX_W), row(MIX_W), row(MIX_W),
                   row(MEM_W), row(FOX_HEADS), row(FOX_HEADS)],
        scratch_shapes=[pltpu.VMEM((1, LANES), F32)],
        compiler_params=_cparams(("arbitrary",)),
        name="fox_inproj",
    )(x, g, w, bd, gq, gk, gmq, bf, tri)


def _ret_inproj_body(x_ref, g_ref, w_ref, bd_ref, gmq_ref, cos_ref, sin_ref,
                     q_ref, k_ref, v_ref, gate_ref, qm_ref):
    h = _rms_rows(x_ref[...], g_ref[...]).astype(BF16)
    cos = cos_ref[...]
    sin = sin_ref[...]
    rot = lambda y: y * cos + pltpu.roll(y, RET_D // 2, 1) * sin
    for j in range(3):
        y = _dot(h, w_ref[:, j * 256:(j + 1) * 256])
        for half in range(2):
            sl = slice(j * 256 + half * RET_D, j * 256 + (half + 1) * RET_D)
            q_ref[:, sl] = rot(y[:, half * RET_D:(half + 1) * RET_D])
    for j in range(3):
        y = _dot(h, w_ref[:, MIX_W + j * 256:MIX_W + (j + 1) * 256])
        for half in range(2):
            sl = slice(j * 256 + half * RET_D, j * 256 + (half + 1) * RET_D)
            k_ref[:, sl] = rot(y[:, half * RET_D:(half + 1) * RET_D]) * (RET_D ** -0.5)
    for j in range(3):
        sl = slice(j * 256, (j + 1) * 256)
        v_ref[:, sl] = _dot(h, w_ref[:, 2 * MIX_W + j * 256:2 * MIX_W + (j + 1) * 256])
    for j in range(3):
        sl = slice(j * 256, (j + 1) * 256)
        gate_ref[:, sl] = _dot(h, w_ref[:, 3 * MIX_W + j * 256:3 * MIX_W + (j + 1) * 256])
    y = _dot(h, w_ref[:, 4 * MIX_W:])
    qm_ref[...] = (_head_norm(y, bd_ref[...], gmq_ref[...]) * 0.125).astype(qm_ref.dtype)


def _ret_inproj(x, g, w, bd, gmq, cos, sin, *, tile, tiles_per_seq, act_dtype):
    n = x.shape[0]
    full = lambda shape: pl.BlockSpec(shape, lambda i: (0,) * len(shape))
    row = lambda width: pl.BlockSpec((tile, width), lambda i: (i, 0))
    tab = pl.BlockSpec((tile, RET_D), lambda i: (i % tiles_per_seq, 0))
    outs = [jax.ShapeDtypeStruct((n, MIX_W), F32)] * 4 + [jax.ShapeDtypeStruct((n, MEM_W), act_dtype)]
    return pl.pallas_call(
        _ret_inproj_body,
        out_shape=outs,
        grid=(n // tile,),
        in_specs=[row(D_MODEL), full((1, D_MODEL)), full(w.shape), full((256, 256)),
                  full((1, 256)), tab, tab],
        out_specs=[row(MIX_W)] * 4 + [row(MEM_W)],
        compiler_params=_cparams(("arbitrary",)),
        name="ret_inproj",
    )(x, g, w, bd, gmq, cos, sin)


def _fox_flash_body(q_ref, k_ref, v_ref, ccol_ref, crow_ref, og_ref, o_ref, *, blk, pairs):
    g = pl.program_id(1)
    qi = pl.program_id(2)
    heads = 2 * pairs
    lane = lax.broadcasted_iota(I32, (blk, LANES), 1)
    lo = lane < HEAD_D
    cc = ccol_ref[...] * LOG2E
    lane12 = lax.broadcasted_iota(I32, cc.shape, 1)
    qs, ccol = [], []
    for h in range(heads):
        q = q_ref[:, (h // 2) * LANES:(h // 2 + 1) * LANES]
        keep = lo if h % 2 == 0 else jnp.logical_not(lo)
        qs.append(jnp.where(keep, q, jnp.zeros_like(q)))
        ccol.append(jnp.sum(jnp.where(lane12 == g * heads + h, cc, 0.0), axis=-1, keepdims=True))
    rows = lax.broadcasted_iota(I32, (blk, blk), 0)
    cols = lax.broadcasted_iota(I32, (blk, blk), 1)
    causal = rows >= cols

    def step(start, carry, diag):
        ms, ls, accs = carry
        new_m, new_l, new_acc = [], [], []
        for p in range(pairs):
            sl = slice(p * LANES, (p + 1) * LANES)
            k = k_ref[pl.ds(start, blk), sl]
            v = v_ref[pl.ds(start, blk), sl]
            alphas, pvs = [], []
            for j in range(2):
                h = 2 * p + j
                s = _dot_nt(qs[h], k) + ccol[h] - crow_ref[h:h + 1, pl.ds(start, blk)] * LOG2E
                if diag:
                    s = jnp.where(causal, s, NEG)
                m_new = jnp.maximum(ms[h], jnp.max(s, axis=-1, keepdims=True))
                alpha = jnp.exp2(ms[h] - m_new)
                e = jnp.exp2(s - m_new)
                new_l.append(alpha * ls[h] + jnp.sum(e, axis=-1, keepdims=True))
                new_m.append(m_new)
                alphas.append(alpha)
                pvs.append(_dot(e.astype(BF16), v))
            new_acc.append(accs[p] * jnp.where(lo, alphas[0], alphas[1]) + jnp.where(lo, pvs[0], pvs[1]))
        return tuple(new_m), tuple(new_l), tuple(new_acc)

    init = ((jnp.full((blk, 1), NEG, F32),) * heads, (jnp.zeros((blk, 1), F32),) * heads,
            (jnp.zeros((blk, LANES), F32),) * pairs)
    carry = lax.fori_loop(
        0, qi, lambda i, c: step(pl.multiple_of(i * blk, blk), c, False), init)
    _, ls, accs = step(pl.multiple_of(qi * blk, blk), carry, True)
    for p in range(pairs):
        sl = slice(p * LANES, (p + 1) * LANES)
        o = accs[p] / jnp.where(lo, ls[2 * p], ls[2 * p + 1])
        o_ref[:, sl] = (o * _sigmoid(og_ref[:, sl])).astype(o_ref.dtype)


def _fox_flash(q, kb, vb, c, c_row, og, *, batch, seq, blk, pairs):
    n = q.shape[0]
    nq = seq // blk
    w = pairs * LANES
    qblk = pl.BlockSpec((blk, w), lambda b, g, i: (b * nq + i, g))
    kvblk = pl.BlockSpec((seq, w), lambda b, g, i: (b, g))
    return pl.pallas_call(
        functools.partial(_fox_flash_body, blk=blk, pairs=pairs),
        out_shape=jax.ShapeDtypeStruct((n, MIX_W), BF16),
        grid=(batch, FOX_HEADS // (2 * pairs), nq),
        in_specs=[
            qblk, kvblk, kvblk,
            pl.BlockSpec((blk, FOX_HEADS), lambda b, g, i: (b * nq + i, 0)),
            pl.BlockSpec((None, None, 2 * pairs, seq), lambda b, g, i: (b, g, 0, 0)),
            qblk,
        ],
        out_specs=qblk,
        compiler_params=_cparams(("arbitrary",) * 3),
        name="fox_flash",
    )(q, kb, vb, c, c_row, og)


def _fox_decode_body(pt_ref, q_ref, kn_ref, vn_ref, lfn_ref, og_ref, kt_hbm, vt_hbm, lf_hbm, tri_ref,
                     o_ref, kbuf, vbuf, lfbuf, sems, qbd_ref, m_ref, l_ref, acc_ref, carry_ref, *,
                     pages_per_step):
    pp = pages_per_step
    j = pl.program_id(1)
    steps = pl.num_programs(1)
    step = pl.program_id(0) * steps + j
    slot = step % 2

    def page_copies(s, into):
        copies = []
        for i in range(pp):
            page = pt_ref[s * pp + i]
            copies += [pltpu.make_async_copy(kt_hbm.at[page], kbuf.at[into, i], sems.at[into]),
                       pltpu.make_async_copy(vt_hbm.at[page], vbuf.at[into, i], sems.at[into]),
                       pltpu.make_async_copy(lf_hbm.at[page], lfbuf.at[into, i], sems.at[into])]
        return copies

    @pl.when(step == 0)
    def _():
        for cp in page_copies(step, slot):
            cp.start()

    @pl.when(step + 1 < pl.num_programs(0) * steps)
    def _():
        for cp in page_copies(step + 1, 1 - slot):
            cp.start()

    for cp in page_copies(step, slot):
        cp.wait()

    nrow = FOX_HEADS * 8
    rowh = lax.broadcasted_iota(I32, (nrow, MIX_W), 0) // 8
    colh = lax.broadcasted_iota(I32, (nrow, MIX_W), 1) // HEAD_D
    diag = rowh == colh

    @pl.when(j == 0)
    def _():
        q = q_ref[...]
        qbd_ref[...] = jnp.where(diag, jnp.concatenate([q] * FOX_HEADS, axis=0), 0.0)
        m_ref[...] = jnp.full_like(m_ref, NEG)
        l_ref[...] = jnp.zeros_like(l_ref)
        acc_ref[...] = jnp.zeros_like(acc_ref)
        carry_ref[...] = jnp.zeros_like(carry_ref)

    qbd = qbd_ref[...]
    tri = tri_ref[...]

    def scores(kt, lf, carry):
        c = _dot_exact_rhs01_left(lf, tri) + carry
        bias = jnp.concatenate(
            [jnp.broadcast_to(c[h:h + 1, :], (8, PAGE)) for h in range(FOX_HEADS)], axis=0)
        return _dot(qbd, kt) - bias, c[:, PAGE - 1:PAGE]

    def update(ss, vts):
        m_old = m_ref[...]
        m_new = m_old
        for s in ss:
            m_new = jnp.maximum(m_new, jnp.max(s, axis=-1, keepdims=True))
        alpha = jnp.exp(m_old - m_new)
        l_new = alpha * l_ref[...]
        pv = None
        for s, vt in zip(ss, vts):
            p = jnp.exp(s - m_new)
            l_new = l_new + jnp.sum(p, axis=-1, keepdims=True)
            d = _dot_nt(p, vt)
            pv = d if pv is None else pv + d
        m_ref[...] = m_new
        l_ref[...] = l_new
        acc_ref[...] = alpha * acc_ref[...] + pv

    carry = carry_ref[...]
    ss, vts = [], []
    for i in range(pp):
        s, carry = scores(kbuf[slot, i].reshape(MIX_W, PAGE), lfbuf[slot, i], carry)
        ss.append(s)
        vts.append(vbuf[slot, i].reshape(MIX_W, PAGE))
    carry_ref[...] = carry
    update(ss, vts)

    @pl.when(j == pl.num_programs(1) - 1)
    def _():
        s, _ = scores(kn_ref[...].reshape(MIX_W, PAGE), lfn_ref[...], carry_ref[...])
        qpos = lax.broadcasted_iota(I32, (nrow, PAGE), 0) % 8
        kpos = lax.broadcasted_iota(I32, (nrow, PAGE), 1)
        s = jnp.where(qpos >= kpos, s, NEG)
        update([s], [vn_ref[...].reshape(MIX_W, PAGE)])
        o = jnp.where(diag, acc_ref[...] / l_ref[...], 0.0)
        out = o[0:8, :]
        for h in range(1, FOX_HEADS):
            out = out + o[h * 8:(h + 1) * 8, :]
        o_ref[...] = (out * _sigmoid(og_ref[...])).astype(o_ref.dtype)


def _dot_exact_rhs01_left(x, m01):
    a, b, c = _split3(x)
    return _dot(a, m01) + _dot(b, m01) + _dot(c, m01)


def _fox_decode(page_table, q, knt, vnt, lfnt, og, kt_pool, vt_pool, lf_pool, *, pages_per_step):
    nb, npages = page_table.shape
    pp = pages_per_step
    steps = npages // pp
    tri = jnp.triu(jnp.ones((PAGE, PAGE), BF16))
    pt = page_table.reshape(-1)

    per_b = lambda shape: pl.BlockSpec((None,) + shape, lambda b, j, pt_ref: (b,) + (0,) * len(shape))
    rows8 = lambda width: pl.BlockSpec((8, width), lambda b, j, pt_ref: (b, 0))
    pool = pl.BlockSpec(memory_space=pl.ANY)
    in_specs = [rows8(MIX_W), per_b((FOX_HEADS, HEAD_D, PAGE)), per_b((FOX_HEADS, HEAD_D, PAGE)),
                per_b((FOX_HEADS, PAGE)), rows8(MIX_W), pool, pool, pool,
                pl.BlockSpec((PAGE, PAGE), lambda b, j, pt_ref: (0, 0))]
    nrow = FOX_HEADS * 8
    return pl.pallas_call(
        functools.partial(_fox_decode_body, pages_per_step=pp),
        out_shape=jax.ShapeDtypeStruct((nb * 8, MIX_W), F32),
        grid_spec=pltpu.PrefetchScalarGridSpec(
            num_scalar_prefetch=1,
            grid=(nb, steps),
            in_specs=in_specs,
            out_specs=pl.BlockSpec((8, MIX_W), lambda b, j, pt_ref: (b, 0)),
            scratch_shapes=[pltpu.VMEM((2, pp, FOX_HEADS, HEAD_D, PAGE), F32),
                            pltpu.VMEM((2, pp, FOX_HEADS, HEAD_D, PAGE), F32),
                            pltpu.VMEM((2, pp, FOX_HEADS, PAGE), F32),
                            pltpu.SemaphoreType.DMA((2,)),
                            pltpu.VMEM((nrow, MIX_W), F32), pltpu.VMEM((nrow, 1), F32),
                            pltpu.VMEM((nrow, 1), F32), pltpu.VMEM((nrow, MIX_W), F32),
                            pltpu.VMEM((FOX_HEADS, 1), F32)]),
        compiler_params=_cparams(("arbitrary", "arbitrary")),
        name="fox_decode",
    )(pt, q, knt, vnt, lfnt, og, kt_pool, vt_pool, lf_pool, tri)


def _retention_body(q_ref, k_ref, v_ref, gate_ref, s0_ref, dmat_ref, qdec_ref, kdec_ref, gc_ref,
                    go_ref, mix_ref, state_ref, *, n_chunks, heads, unroll):
    go = go_ref[...]

    def chunk(n, states):
        rows = pl.ds(pl.multiple_of(n * RET_D, RET_D), RET_D)
        new_states = []
        for h in range(heads):
            cols = slice(h * RET_D, (h + 1) * RET_D)
            q = q_ref[rows, cols]
            k = k_ref[rows, cols]
            v = v_ref[rows, cols].astype(BF16)
            inner = _dot_nt(q.astype(BF16), k.astype(BF16)) * dmat_ref[h]
            o = (_dot(inner.astype(BF16), v)
                 + _dot((q * qdec_ref[h]).astype(BF16), states[h].astype(BF16)))
            new_states.append(gc_ref[h] * states[h] + _dot((k * kdec_ref[h]).T.astype(BF16), v))
            g = gate_ref[rows, cols]
            mix_ref[rows, cols] = (_rms_rows(o, go) * (g * _sigmoid(g))).astype(mix_ref.dtype)
        return tuple(new_states)

    states = lax.fori_loop(0, n_chunks, chunk, tuple(s0_ref[h] for h in range(heads)), unroll=unroll)
    for h in range(heads):
        state_ref[h] = states[h]


def _retention(q, k, v, gate, s0, tabs, go, *, batch, seq, out_dtype, heads):
    n = q.shape[0]
    n_chunks = seq // RET_D
    dmat, qdec, kdec, gc = tabs
    seq_blk = pl.BlockSpec((seq, heads * RET_D), lambda b, g: (b, g))
    tab = pl.BlockSpec((heads, RET_D, RET_D), lambda b, g: (g, 0, 0))
    st = pl.BlockSpec((None, heads, RET_D, RET_D), lambda b, g: (b, g, 0, 0))
    return pl.pallas_call(
        functools.partial(_retention_body, n_chunks=n_chunks, heads=heads,
                          unroll=2 if n_chunks % 2 == 0 else 1),
        out_shape=[jax.ShapeDtypeStruct((n, MIX_W), out_dtype),
                   jax.ShapeDtypeStruct((batch, RET_HEADS, RET_D, RET_D), F32)],
        grid=(batch, RET_HEADS // heads),
        in_specs=[seq_blk, seq_blk, seq_blk, seq_blk, st, tab, tab, tab, tab,
                  pl.BlockSpec((1, RET_D), lambda b, g: (0, 0))],
        out_specs=[seq_blk, st],
        compiler_params=_cparams(("arbitrary", "arbitrary")),
        name="retention",
    )(q, k, v, gate, s0, dmat, qdec, kdec, gc, go)


def _retention_tables(chunk):
    log_g = jnp.log1p(-jnp.exp2(-5.0 - jnp.arange(RET_HEADS, dtype=F32)))
    n = jnp.arange(RET_D, dtype=F32)
    diff = n[:, None] - n[None, :]
    valid = (n[:, None] < chunk) & (n[None, :] < chunk)
    dmat = jnp.where((diff >= 0) & valid,
                     jnp.exp(jnp.maximum(diff, 0.0)[None] * log_g[:, None, None]), 0.0)
    qdec = jnp.exp((n + 1.0)[None, :] * log_g[:, None])
    kdec = jnp.where(n[None, :] < chunk, jnp.exp((chunk - 1.0 - n)[None, :] * log_g[:, None]), 0.0)
    gc = jnp.exp(chunk * log_g)
    bc = lambda a: jnp.broadcast_to(a[:, :, None], (RET_HEADS, RET_D, RET_D))
    return (dmat, bc(qdec), bc(kdec),
            jnp.broadcast_to(gc[:, None, None], (RET_HEADS, RET_D, RET_D)))


def _mem_kv_body(mem_ref, g_ref, w_ref, bd_ref, gk_ref, k_ref, v_ref):
    h = _rms_rows(mem_ref[...], g_ref[...]).astype(BF16)
    k_ref[...] = _head_norm(_dot(h, w_ref[:, :MEM_W]), bd_ref[...], gk_ref[...])
    v_ref[...] = _dot(h, w_ref[:, MEM_W:])


def _mem_kv(mem, g, w, bd, gk, *, tile):
    depth = w.shape[0]
    n = mem.shape[0]
    per_layer = lambda shape: pl.BlockSpec((None,) + shape, lambda l, i: (l,) + (0,) * len(shape))
    out = pl.BlockSpec((None, tile, MEM_W), lambda l, i: (l, i, 0))
    return pl.pallas_call(
        _mem_kv_body,
        out_shape=[jax.ShapeDtypeStruct((depth, n, MEM_W), F32)] * 2,
        grid=(depth, n // tile),
        in_specs=[pl.BlockSpec((tile, D_MODEL), lambda l, i: (i, 0)),
                  per_layer((1, D_MODEL)), per_layer((D_MODEL, 2 * MEM_W)),
                  pl.BlockSpec((256, 256), lambda l, i: (0, 0)), per_layer((1, MEM_W))],
        out_specs=[out, out],
        compiler_params=_cparams(("arbitrary", "arbitrary")),
        name="mem_kv",
    )(mem, g, w, bd, gk)


def _mem_attend_body(q_ref, k_ref, v_ref, o_ref):
    t = q_ref.shape[0]
    lane = lax.broadcasted_iota(I32, (t, LANES), 1)
    lo = lane < HEAD_D
    for slab in range(MEM_W // LANES):
        sl = slice(slab * LANES, (slab + 1) * LANES)
        q = q_ref[:, sl]
        k = k_ref[:, sl].astype(BF16)
        v = v_ref[:, sl].astype(BF16)
        zq = jnp.zeros_like(q)
        outs = []
        for qh in (jnp.where(lo, q, zq), jnp.where(lo, zq, q)):
            s = _dot_nt(qh, k)
            p = jnp.exp(s - jnp.max(s, axis=-1, keepdims=True))
            p = p / jnp.sum(p, axis=-1, keepdims=True)
            outs.append(_dot(p.astype(BF16), v))
        o_ref[:, sl] = jnp.where(lo, outs[0], outs[1]).astype(o_ref.dtype)


def _mem_attend(qm, mk, mv, *, batch, seq, tile):
    n = qm.shape[0]
    nt = seq // tile
    kv = pl.BlockSpec((N_MEM, MEM_W), lambda b, i: (b, 0))
    return pl.pallas_call(
        _mem_attend_body,
        out_shape=jax.ShapeDtypeStruct((n, MEM_W), BF16),
        grid=(batch, nt),
        in_specs=[pl.BlockSpec((tile, MEM_W), lambda b, i: (b * nt + i, 0)), kv, kv],
        out_specs=pl.BlockSpec((tile, MEM_W), lambda b, i: (b * nt + i, 0)),
        compiler_params=_cparams(("arbitrary", "arbitrary")),
        name="mem_attend",
    )(qm, mk, mv)


def _mem_attend_t_body(q_ref, kt_ref, vt_ref, o_ref):
    t = q_ref.shape[0]
    lane = lax.broadcasted_iota(I32, (t, LANES), 1)
    lo = lane < HEAD_D
    for slab in range(MEM_W // LANES):
        sl = slice(slab * LANES, (slab + 1) * LANES)
        q = q_ref[:, sl]
        kt = kt_ref[2 * slab:2 * slab + 2].reshape(LANES, N_MEM)
        vt = vt_ref[2 * slab:2 * slab + 2].reshape(LANES, N_MEM)
        zq = jnp.zeros_like(q)
        outs = []
        for qh in (jnp.where(lo, q, zq), jnp.where(lo, zq, q)):
            s = _dot(qh, kt)
            p = jnp.exp(s - jnp.max(s, axis=-1, keepdims=True))
            p = p / jnp.sum(p, axis=-1, keepdims=True)
            outs.append(_dot_nt(p, vt))
        o_ref[:, sl] = jnp.where(lo, outs[0], outs[1])


def _mem_attend_t(qm, mkt, mvt, *, batch, tokens):
    kv = pl.BlockSpec((None, MEM_W // HEAD_D, HEAD_D, N_MEM), lambda b: (b, 0, 0, 0))
    return pl.pallas_call(
        _mem_attend_t_body,
        out_shape=jax.ShapeDtypeStruct((batch * tokens, MEM_W), F32),
        grid=(batch,),
        in_specs=[pl.BlockSpec((tokens, MEM_W), lambda b: (b, 0)), kv, kv],
        out_specs=pl.BlockSpec((tokens, MEM_W), lambda b: (b, 0)),
        compiler_params=_cparams(("arbitrary",)),
        name="mem_attend_t",
    )(qm, mkt, mvt)


def _outproj_router_body(x_ref, mix_ref, memo_ref, wa_ref, wb_ref, g_ref, wr_ref, br_ref, tri_ref,
                         xo_ref, h_ref, ri_ref, rw_ref, cnt_ref, carry_ref):
    sub = tri_ref.shape[0]

    @pl.when(pl.program_id(0) == 0)
    def _():
        carry_ref[...] = jnp.zeros_like(carry_ref)

    carry = carry_ref[...]
    for s0 in range(0, x_ref.shape[0], sub):
        rows = slice(s0, s0 + sub)
        x = (x_ref[rows, :] + _dot(mix_ref[rows, :].astype(BF16), wa_ref[...])
             + _dot(memo_ref[rows, :].astype(BF16), wb_ref[...]))
        xo_ref[rows, :] = x
        h = _rms_rows(x, g_ref[...])
        _to_token_major(h_ref.at[pl.ds(s0 * TOK_ROWS, sub * TOK_ROWS), :], h)
        carry = _route(h, wr_ref, br_ref[...], tri_ref[...], carry, ri_ref.at[rows, :], rw_ref.at[rows, :])
    carry_ref[...] = carry
    cnt_ref[...] = carry


def _route(h, wr_ref, br, tri, carry, ri_ref, rw_ref):
    t = h.shape[0]
    h_hi = h.astype(BF16)
    h_lo = (h - h_hi.astype(F32)).astype(BF16)
    d = _dot(h_hi, wr_ref[...])
    logits = d[:, :LANES] + d[:, LANES:] + _dot(h_lo, wr_ref[:, :LANES]) + br
    lane = lax.broadcasted_iota(I32, (t, LANES), 1)
    big = jnp.int32(1 << 20)
    rmax = lambda a: jnp.max(a, axis=-1, keepdims=True)
    rmin = lambda a: jnp.min(a, axis=-1, keepdims=True)
    rsum = lambda a: jnp.sum(a, axis=-1, keepdims=True)
    gmask = lane < N_GROUPS
    lg = jnp.where(gmask, logits, -jnp.inf)
    gmax = rmax(lg)
    gi = rmin(jnp.where(lg == gmax, lane, big))
    p_top = 1.0 / rsum(jnp.where(gmask, jnp.exp(logits - gmax), 0.0))
    first = ROUTE_LANE0 + EPG * gi
    emask = (lane >= first) & (lane < first + EPG)
    le = jnp.where(emask, logits, -jnp.inf)
    v1 = rmax(le)
    i1 = rmin(jnp.where(le == v1, lane, big))
    le2 = jnp.where(lane == i1, -jnp.inf, le)
    v2 = rmax(le2)
    i2 = rmin(jnp.where(le2 == v2, lane, big))
    e2 = jnp.exp(v2 - v1)
    w1 = p_top / (1.0 + e2)
    w2 = p_top * e2 / (1.0 + e2)

    onehot = ((lane == i1) | (lane == i2)).astype(F32)
    before = _dot(tri, onehot.astype(BF16)) + carry
    r1 = rsum(jnp.where(lane == i1, before, 0.0)).astype(I32)
    r2 = rsum(jnp.where(lane == i2, before, 0.0)).astype(I32)
    lane4 = lax.broadcasted_iota(I32, (t, 4), 1)
    ri_ref[...] = jnp.where(lane4 == 0, i1 - ROUTE_LANE0,
                            jnp.where(lane4 == 1, i2 - ROUTE_LANE0, jnp.where(lane4 == 2, r1, r2)))
    lane2 = lax.broadcasted_iota(I32, (t, 2), 1)
    rw_ref[...] = jnp.where(lane2 == 0, w1, w2)
    return carry + jnp.sum(onehot, axis=0, keepdims=True)


def _outproj_router(x, mix, memo, wa, wb, g, wr, br, *, tile):
    n = x.shape[0]
    sub = tile
    tri = jnp.tril(jnp.ones((sub, sub), BF16), -1)
    full = lambda shape: pl.BlockSpec(shape, lambda i: (0,) * len(shape))
    row = lambda width: pl.BlockSpec((tile, width), lambda i: (i, 0))
    return pl.pallas_call(
        _outproj_router_body,
        out_shape=[jax.ShapeDtypeStruct((n, D_MODEL), F32),
                   jax.ShapeDtypeStruct((n * TOK_ROWS, LANES), F32),
                   jax.ShapeDtypeStruct((n, 4), I32), jax.ShapeDtypeStruct((n, 2), F32),
                   jax.ShapeDtypeStruct((1, LANES), F32)],
        grid=(n // tile,),
        in_specs=[row(D_MODEL), row(MIX_W), row(MEM_W), full((MIX_W, D_MODEL)), full((MEM_W, D_MODEL)),
                  full((1, D_MODEL)), full((D_MODEL, 2 * LANES)), full((1, LANES)), full((sub, sub))],
        out_specs=[row(D_MODEL), pl.BlockSpec((tile * TOK_ROWS, LANES), lambda i: (i, 0)),
                   row(4), row(2), full((1, LANES))],
        scratch_shapes=[pltpu.VMEM((1, LANES), F32)],
        compiler_params=_cparams(("arbitrary",)),
        name="outproj_router",
    )(x, mix, memo, wa, wb, g, wr, br, tri)


TOK_ROWS = D_MODEL // LANES
DMA_UNROLL = 8


def _to_token_major(ref, x):
    t = x.shape[0]
    for s in range(TOK_ROWS):
        ref[pl.ds(s, t, stride=TOK_ROWS), :] = x[:, s * LANES:(s + 1) * LANES]


def _from_token_major(ref, t):
    return [ref[pl.ds(s, t, stride=TOK_ROWS), :] for s in range(TOK_ROWS)]


def _token_tile(ref, row):
    return ref.at[pl.ds(pl.multiple_of(row * TOK_ROWS, TOK_ROWS), TOK_ROWS), :]


def _moe_dispatch_body(eid_ref, rank_ref, off_ref, zt_ref, h_ref, xs_hbm, zero_ref, sem, *,
                       tile, expert_tile, n_expert_tiles):
    base = pl.program_id(0) * tile

    @pl.when(pl.program_id(0) == 0)
    def _():
        zero_ref[...] = jnp.zeros_like(zero_ref)
        rows = expert_tile * TOK_ROWS

        def fill(wait):
            def body(i, c):
                @pl.when(zt_ref[i] == 1)
                def _():
                    dst = xs_hbm.at[pl.ds(pl.multiple_of(i * rows, rows), rows), :]
                    cp = pltpu.make_async_copy(zero_ref, dst, sem)
                    cp.wait() if wait else cp.start()
                return c
            lax.fori_loop(0, n_expert_tiles, body, 0)

        fill(False)
        fill(True)

    def issue(t, c):
        for slot in range(2):
            a = 2 * (base + t) + slot
            dst = off_ref[eid_ref[a]] + rank_ref[a]
            pltpu.make_async_copy(_token_tile(h_ref, t), _token_tile(xs_hbm, dst), sem).start()
        return c

    lax.fori_loop(0, tile, issue, 0, unroll=DMA_UNROLL)

    def drain(t, c):
        pltpu.make_async_copy(_token_tile(h_ref, 0), _token_tile(xs_hbm, 0), sem).wait()
        return c

    lax.fori_loop(0, 2 * tile, drain, 0, unroll=DMA_UNROLL)


def _moe_dispatch(eid, rank, off, zero_tile, h_tm, *, tile, expert_tile):
    n = h_tm.shape[0] // TOK_ROWS
    n_expert_tiles = zero_tile.shape[0]
    return pl.pallas_call(
        functools.partial(_moe_dispatch_body, tile=tile, expert_tile=expert_tile,
                          n_expert_tiles=n_expert_tiles),
        out_shape=jax.ShapeDtypeStruct((n_expert_tiles * expert_tile * TOK_ROWS, LANES), F32),
        grid_spec=pltpu.PrefetchScalarGridSpec(
            num_scalar_prefetch=4, grid=(n // tile,),
            in_specs=[pl.BlockSpec((tile * TOK_ROWS, LANES), lambda i, *_: (i, 0))],
            out_specs=pl.BlockSpec(memory_space=pl.ANY),
            scratch_shapes=[pltpu.VMEM((expert_tile * TOK_ROWS, LANES), F32),
                            pltpu.SemaphoreType.DMA(())]),
        compiler_params=_cparams(("arbitrary",)),
        name="moe_dispatch",
    )(eid, rank, off, zero_tile, h_tm)


def _moe_experts_body(te_ref, nu_ref, x_ref, w1_ref, w3_ref, w2_ref, y_ref, *, tile):
    del te_ref

    @pl.when(pl.program_id(0) < nu_ref[0])
    def _():
        x = jnp.concatenate(_from_token_major(x_ref, tile), axis=1).astype(BF16)
        a = _dot(x, w1_ref[...].astype(BF16))
        b = _dot(x, w3_ref[...].astype(BF16))
        hid = a * _sigmoid(a) * b
        _to_token_major(y_ref, _dot(hid.astype(BF16), w2_ref[...].astype(BF16)))

    @pl.when(pl.program_id(0) >= nu_ref[0])
    def _():
        y_ref[...] = jnp.zeros_like(y_ref)


def _moe_experts(tile_expert, n_used, xs, w1, w3, w2, *, tile, layer):
    rows = xs.shape[0] // TOK_ROWS
    wspec = lambda shape: pl.BlockSpec((None, None) + shape, lambda i, te, nu: (layer, te[i], 0, 0))
    blk = pl.BlockSpec((tile * TOK_ROWS, LANES), lambda i, te, nu: (i, 0))
    return pl.pallas_call(
        functools.partial(_moe_experts_body, tile=tile),
        out_shape=jax.ShapeDtypeStruct(xs.shape, F32),
        grid_spec=pltpu.PrefetchScalarGridSpec(
            num_scalar_prefetch=2, grid=(rows // tile,),
            in_specs=[blk, wspec((D_MODEL, D_EXPERT)), wspec((D_MODEL, D_EXPERT)),
                      wspec((D_EXPERT, D_MODEL))],
            out_specs=blk),
        compiler_params=_cparams(("arbitrary",)),
        name="moe_experts",
    )(tile_expert, n_used, xs, w1, w3, w2)


def _moe_combine_body(eid_ref, rank_ref, off_ref, x_ref, rw_ref, ys_hbm, o_ref, buf_ref, sem, *, tile):
    base = pl.program_id(0) * tile

    def issue(t, c):
        for slot in range(2):
            a = 2 * (base + t) + slot
            src = off_ref[eid_ref[a]] + rank_ref[a]
            pltpu.make_async_copy(_token_tile(ys_hbm, src), _token_tile(buf_ref.at[slot], t), sem).start()
        return c

    lax.fori_loop(0, tile, issue, 0, unroll=DMA_UNROLL)

    def drain(t, c):
        pltpu.make_async_copy(_token_tile(ys_hbm, 0), _token_tile(buf_ref.at[0], 0), sem).wait()
        return c

    lax.fori_loop(0, 2 * tile, drain, 0, unroll=DMA_UNROLL)
    rw = rw_ref[...]
    y0 = _from_token_major(buf_ref.at[0], tile)
    y1 = _from_token_major(buf_ref.at[1], tile)
    for s in range(TOK_ROWS):
        sl = slice(s * LANES, (s + 1) * LANES)
        o_ref[:, sl] = x_ref[:, sl] + rw[:, 0:1] * y0[s] + rw[:, 1:2] * y1[s]


def _moe_combine(eid, rank, off, x, rw, ys, *, tile):
    n = x.shape[0]
    row = lambda width: pl.BlockSpec((tile, width), lambda i, *_: (i, 0))
    return pl.pallas_call(
        functools.partial(_moe_combine_body, tile=tile),
        out_shape=jax.ShapeDtypeStruct((n, D_MODEL), F32),
        grid_spec=pltpu.PrefetchScalarGridSpec(
            num_scalar_prefetch=3, grid=(n // tile,),
            in_specs=[row(D_MODEL), row(2), pl.BlockSpec(memory_space=pl.ANY)],
            out_specs=row(D_MODEL),
            scratch_shapes=[pltpu.VMEM((2, tile * TOK_ROWS, LANES), F32), pltpu.SemaphoreType.DMA(())]),
        compiler_params=_cparams(("arbitrary",)),
        name="moe_combine",
    )(eid, rank, off, x, rw, ys)


def _moe(x, h, ri, rw, cnt, w1, w3, w2, *, layer, expert_tile, token_tile):
    n = x.shape[0]
    n_tiles = (2 * n + N_EXPERTS * (expert_tile - 1)) // expert_tile + 1
    counts = cnt[0, ROUTE_LANE0:ROUTE_LANE0 + N_EXPERTS].astype(I32)
    padded = (counts + expert_tile - 1) // expert_tile * expert_tile
    ends = jnp.cumsum(padded)
    off = ends - padded
    n_used = (ends[-1] // expert_tile).reshape(1)
    tile_start = jnp.arange(n_tiles, dtype=I32) * expert_tile
    tile_expert = jnp.minimum(
        jnp.sum((tile_start[:, None] >= ends[None, :]).astype(I32), axis=1), N_EXPERTS - 1)
    last_of_expert = jnp.any((tile_start[:, None] + expert_tile == ends[None, :])
                             & (padded[None, :] > counts[None, :]), axis=1)
    zero_tile = (last_of_expert | (tile_start >= ends[-1])).astype(I32)
    eid = ri[:, 0:2].reshape(-1)
    rank = ri[:, 2:4].reshape(-1)
    xs = _moe_dispatch(eid, rank, off, zero_tile, h, tile=token_tile, expert_tile=expert_tile)
    ys = _moe_experts(tile_expert, n_used, xs, w1, w3, w2, tile=expert_tile, layer=layer)
    return _moe_combine(eid, rank, off, x, rw, ys, tile=token_tile)


def _block_diag_ones():
    r = jnp.arange(256) // HEAD_D
    return (r[:, None] == r[None, :]).astype(BF16)


def _tile_gain(g, reps):
    return jnp.tile(g.astype(F32), reps)[None, :]


def _rotary_tables(pos):
    half = RET_D // 2
    inv = 10000.0 ** (-jnp.arange(half, dtype=F32) / half)
    ang = pos.astype(F32)[:, None] * inv[None, :]
    cos, sin = jnp.cos(ang), jnp.sin(ang)
    return jnp.concatenate([cos, cos], axis=-1), jnp.concatenate([-sin, sin], axis=-1)


def kernel(x_prompt, x_sample, mem_prompt, cache_fox_k, cache_fox_v, cache_fox_logf, page_table,
           state_ret, cache_mem_k, cache_mem_v, w_in_fox, b_f_fox, g_q_fox, g_k_fox, w_in_ret, g_o_ret,
           g_attn, g_mem, w_mem_kv, g_mq, g_mk, w_out, g_ffn, w_rg, b_rg, w_re, b_re, w1, w3, w2):
    bp, sp, _ = x_prompt.shape
    bs, ts, _ = x_sample.shape
    depth = w_out.shape[0]
    n_pages = page_table.shape[1]
    past = n_pages * PAGE
    np_, ns = bp * sp, bs * ts
    tile_p, tile_s = 512, ns
    bd = _block_diag_ones()

    xp = x_prompt.reshape(np_, D_MODEL)
    xs = x_sample.reshape(ns, D_MODEL)
    mem = mem_prompt.reshape(bp * N_MEM, D_MODEL)
    mk_all, mv_all = _mem_kv(mem, g_mem[:, None, :], w_mem_kv.astype(BF16), bd,
                             jnp.tile(g_mk, (1, MEM_W // HEAD_D))[:, None, :], tile=512)

    outs = {}
    for l in range(depth):
        i = l // 2
        g_l = g_attn[l][None, :]
        gmq = _tile_gain(g_mq[l], 256 // HEAD_D)
        if l % 2 == 0:
            w = w_in_fox[i]
            w_pack = jnp.concatenate(
                [w[:, :4 * MIX_W], w[:, 4 * MIX_W + FOX_HEADS:], w[:, 4 * MIX_W:4 * MIX_W + FOX_HEADS],
                 jnp.zeros((D_MODEL, LANES - FOX_HEADS), F32)], axis=1).astype(BF16)
            bf = jnp.concatenate([b_f_fox[i], jnp.zeros((LANES - FOX_HEADS,), F32)])[None, :]
            gq = _tile_gain(g_q_fox[i], 256 // HEAD_D)
            gk = _tile_gain(g_k_fox[i], 256 // HEAD_D)
            fox = functools.partial(_fox_inproj, g=g_l, w=w_pack, bd=bd, gq=gq, gk=gk, gmq=gmq, bf=bf)
            qp, kp, kbp, vp, vbp, ogp, qmp, lfp, cp = fox(
                xp, tile=tile_p, tiles_per_seq=sp // tile_p, act_dtype=BF16,
                q_scale=HEAD_D ** -0.5 * LOG2E)
            qs, ks, _, vs, _, ogs, qms, lfs, _ = fox(xs, tile=tile_s, tiles_per_seq=1, act_dtype=F32,
                                                     q_scale=HEAD_D ** -0.5)
            pairs = FOX_HEADS // 2
            c_row = jnp.transpose(cp.reshape(bp, sp, FOX_HEADS), (0, 2, 1)).reshape(
                bp, FOX_HEADS // (2 * pairs), 2 * pairs, sp)
            mix_p = _fox_flash(qp, kbp, vbp, cp, c_row, ogp, batch=bp, seq=sp, blk=256, pairs=pairs)

            def pad_t(a):
                a = jnp.transpose(a.reshape(bs, ts, -1), (0, 2, 1))
                return jnp.pad(a, ((0, 0), (0, 0), (0, PAGE - ts)))

            knt = pad_t(ks).reshape(bs, FOX_HEADS, HEAD_D, PAGE)
            vnt = pad_t(vs).reshape(bs, FOX_HEADS, HEAD_D, PAGE)
            lfnt = pad_t(lfs)
            kt_pool = jnp.transpose(cache_fox_k[i], (0, 2, 3, 1))
            vt_pool = jnp.transpose(cache_fox_v[i], (0, 2, 3, 1))
            lf_pool = jnp.transpose(cache_fox_logf[i], (0, 2, 1))
            mix_s = _fox_decode(page_table, qs, knt, vnt, lfnt, ogs, kt_pool, vt_pool, lf_pool,
                                pages_per_step=8)
            outs.setdefault("fkp", []).append(kp.reshape(bp, sp, FOX_HEADS, HEAD_D))
            outs.setdefault("fvp", []).append(vp.reshape(bp, sp, FOX_HEADS, HEAD_D))
            outs.setdefault("flp", []).append(lfp.reshape(bp, sp, FOX_HEADS))
            outs.setdefault("fks", []).append(ks.reshape(bs, ts, FOX_HEADS, HEAD_D))
            outs.setdefault("fvs", []).append(vs.reshape(bs, ts, FOX_HEADS, HEAD_D))
            outs.setdefault("fls", []).append(lfs.reshape(bs, ts, FOX_HEADS))
        else:
            w = w_in_ret[i].astype(BF16)
            go = g_o_ret[i][None, :]
            cos_p, sin_p = _rotary_tables(jnp.arange(sp))
            cos_s, sin_s = _rotary_tables(past + jnp.arange(ts))
            ret = functools.partial(_ret_inproj, g=g_l, w=w, bd=bd, gmq=gmq)
            qp, kp, vp, gp, qmp = ret(xp, cos=cos_p, sin=sin_p, tile=tile_p,
                                      tiles_per_seq=sp // tile_p, act_dtype=BF16)
            qs, ks, vs, gs, qms = ret(xs, cos=jnp.tile(cos_s, (bs, 1)), sin=jnp.tile(sin_s, (bs, 1)),
                                      tile=tile_s, tiles_per_seq=1, act_dtype=F32)
            mix_p, sp_state = _retention(
                qp, kp, vp, gp, jnp.zeros((bp, RET_HEADS, RET_D, RET_D), F32),
                _retention_tables(RET_D), go, batch=bp, seq=sp, out_dtype=BF16, heads=3)
            pad_rows = lambda a: jnp.pad(a.reshape(bs, ts, MIX_W),
                                         ((0, 0), (0, RET_D - ts), (0, 0))).reshape(bs * RET_D, MIX_W)
            mix_s_pad, ss_state = _retention(
                pad_rows(qs), pad_rows(ks), pad_rows(vs), pad_rows(gs), state_ret[i],
                _retention_tables(ts), go, batch=bs, seq=RET_D, out_dtype=F32, heads=RET_HEADS)
            mix_s = mix_s_pad.reshape(bs, RET_D, MIX_W)[:, :ts].reshape(ns, MIX_W)
            outs.setdefault("rsp", []).append(sp_state)
            outs.setdefault("rss", []).append(ss_state)

        memo_p = _mem_attend(qmp, mk_all[l], mv_all[l], batch=bp, seq=sp, tile=tile_p)
        memo_s = _mem_attend_t(qms, jnp.transpose(cache_mem_k[l], (0, 2, 3, 1)),
                               jnp.transpose(cache_mem_v[l], (0, 2, 3, 1)), batch=bs, tokens=ts)

        wo = w_out[l].astype(BF16)
        wr = jnp.concatenate([w_rg[l], w_re[l], jnp.zeros((D_MODEL, LANES - N_GROUPS - N_EXPERTS), F32)], axis=1)
        wr_hi = wr.astype(BF16)
        wr_lo = (wr - wr_hi.astype(F32)).astype(BF16)
        br = jnp.concatenate([b_rg[l], b_re[l], jnp.zeros((LANES - N_GROUPS - N_EXPERTS,), F32)])[None, :]
        route = functools.partial(_outproj_router, wa=wo[:MIX_W], wb=wo[MIX_W:], g=g_ffn[l][None, :],
                                  wr=jnp.concatenate([wr_hi, wr_lo], axis=1), br=br)
        xp, hp, rip, rwp, cntp = route(xp, mix_p, memo_p, tile=tile_p)
        xs, hs, ris, rws, cnts = route(xs, mix_s, memo_s, tile=tile_s)
        xp = _moe(xp, hp, rip, rwp, cntp, w1, w3, w2, layer=l, expert_tile=256, token_tile=tile_p)
        xs = _moe(xs, hs, ris, rws, cnts, w1, w3, w2, layer=l, expert_tile=128, token_tile=tile_s)

    mem_shape = (depth, bp, N_MEM, MEM_W // HEAD_D, HEAD_D)
    return (xp.reshape(bp, sp, D_MODEL), xs.reshape(bs, ts, D_MODEL),
            jnp.stack(outs["fkp"]), jnp.stack(outs["fvp"]), jnp.stack(outs["flp"]),
            jnp.stack(outs["fks"]), jnp.stack(outs["fvs"]), jnp.stack(outs["fls"]),
            jnp.stack(outs["rsp"]), jnp.stack(outs["rss"]),
            mk_all.reshape(mem_shape), mv_all.reshape(mem_shape))
```

```python
import functools

import jax
import jax.numpy as jnp
from jax import lax
from jax.experimental import pallas as pl
from jax.experimental.pallas import tpu as pltpu

F32 = jnp.float32
BF16 = jnp.bfloat16
I32 = jnp.int32

D_MODEL = 1024
MIX_W = 768
MEM_W = 256
HEAD_D = 64
FOX_HEADS = 12
RET_HEADS = 6
RET_D = 128
N_MEM = 256
N_EXPERTS = 32
N_GROUPS = 4
EPG = 8
D_EXPERT = 256
PAGE = 128
EPS = 1e-6
NEG = -1e30
LANES = 128
VMEM_LIMIT = 56 * 1024 * 1024
ROUTE_LANE0 = N_GROUPS
LOG2E = 1.4426950408889634
SUBLANES = 8


def _cparams(sem, vmem=VMEM_LIMIT):
    return pltpu.CompilerParams(dimension_semantics=sem, vmem_limit_bytes=vmem)


def _dot(a, b):
    return jnp.dot(a, b, preferred_element_type=F32)


def _dot_nt(a, b):
    return lax.dot_general(a, b, (((1,), (1,)), ((), ())), preferred_element_type=F32)


def _split3(x):
    a = x.astype(BF16)
    r = x - a.astype(F32)
    b = r.astype(BF16)
    c = (r - b.astype(F32)).astype(BF16)
    return a, b, c


def _dot_exact_rhs01(m01, x):
    a, b, c = _split3(x)
    return _dot(m01, a) + _dot(m01, b) + _dot(m01, c)


def _rms_rows(x, gain):
    return x * lax.rsqrt(jnp.mean(x * x, axis=-1, keepdims=True) + EPS) * gain


def _head_norm(y, bd, gain):
    ss = _dot((y * y).astype(BF16), bd)
    return y * lax.rsqrt(ss * (1.0 / HEAD_D) + EPS) * gain


def _sigmoid(x):
    return 1.0 / (1.0 + jnp.exp(-x))


def _fox_inproj_body(x_ref, g_ref, w_ref, bd_ref, gq_ref, gk_ref, gmq_ref, bf_ref, tri_ref,
                     sel_ref, ones_ref,
                     q_ref, k_ref, kb_ref, v_ref, vb_ref, og_ref, qm_ref, lf_ref, qx_ref, kx_ref,
                     carry_ref, *, tiles_per_seq, q_scale, bias_scale):
    t = x_ref.shape[0]
    h = _rms_rows(x_ref[...], g_ref[...]).astype(BF16)
    bd = bd_ref[...]
    for j in range(3):
        sl = slice(j * 256, (j + 1) * 256)
        y = _dot(h, w_ref[:, sl])
        q_ref[:, sl] = (_head_norm(y, bd, gq_ref[...]) * q_scale).astype(q_ref.dtype)
    for j in range(3):
        sl = slice(j * 256, (j + 1) * 256)
        y = _dot(h, w_ref[:, MIX_W + j * 256:MIX_W + (j + 1) * 256])
        kn = _head_norm(y, bd, gk_ref[...])
        k_ref[:, sl] = kn
        kb_ref[:, sl] = kn.astype(BF16)
    for j in range(3):
        sl = slice(j * 256, (j + 1) * 256)
        y = _dot(h, w_ref[:, 2 * MIX_W + j * 256:2 * MIX_W + (j + 1) * 256])
        v_ref[:, sl] = y
        vb_ref[:, sl] = y.astype(BF16)
    for j in range(3):
        sl = slice(j * 256, (j + 1) * 256)
        og_ref[:, sl] = _dot(h, w_ref[:, 3 * MIX_W + j * 256:3 * MIX_W + (j + 1) * 256])
    y = _dot(h, w_ref[:, 4 * MIX_W:4 * MIX_W + MEM_W])
    qm_ref[...] = (_head_norm(y, bd, gmq_ref[...]) * 0.125).astype(qm_ref.dtype)
    f = _dot(h, w_ref[:, 4 * MIX_W + MEM_W:]) + bf_ref[...]
    lf = jnp.minimum(f, 0.0) - jnp.log1p(jnp.exp(-jnp.abs(f)))
    lane = lax.broadcasted_iota(I32, (t, LANES), 1)
    lf = jnp.where(lane < FOX_HEADS, lf, 0.0)
    lf_ref[...] = lf[:, :FOX_HEADS]

    @pl.when(pl.program_id(0) % tiles_per_seq == 0)
    def _():
        carry_ref[...] = jnp.zeros_like(carry_ref)

    c = _dot_exact_rhs01(tri_ref[...], lf) + carry_ref[...]
    carry_ref[...] = c[t - 1:t, :]
    pieces = jnp.concatenate(_split3(c * bias_scale), axis=1)
    x = _dot(pieces, sel_ref[...]) + ones_ref[...]
    qx_ref[...] = x[:, :MIX_W].astype(BF16)
    kx_ref[...] = x[:, MIX_W:].astype(BF16)


def _bias_columns():
    piece = jnp.arange(3 * LANES) // LANES
    head = jnp.arange(3 * LANES) % LANES
    lane = jnp.arange(2 * MIX_W)
    half = lane // MIX_W
    in_pair = lane % LANES
    lane_head = 2 * ((lane % MIX_W) // LANES) + in_pair // 6
    slot = in_pair % 6
    live = in_pair < 12
    takes_c = live & jnp.where(half == 0, slot < 3, slot >= 3)
    sel = ((head[:, None] == lane_head[None, :]) & takes_c[None, :]
           & (piece[:, None] == (slot % 3)[None, :]))
    sel = jnp.where(sel, jnp.where(half == 0, 1.0, -1.0)[None, :], 0.0).astype(BF16)
    ones = (live & jnp.logical_not(takes_c)).astype(F32)[None, :]
    return sel, ones


def _fox_inproj(x, g, w, bd, gq, gk, gmq, bf, *, tile, tiles_per_seq, act_dtype, q_scale, bias_scale):
    n = x.shape[0]
    tri = jnp.tril(jnp.ones((tile, tile), BF16))
    sel, ones = _bias_columns()
    full = lambda shape: pl.BlockSpec(shape, lambda i: (0,) * len(shape))
    row = lambda width: pl.BlockSpec((tile, width), lambda i: (i, 0))
    outs = [
        jax.ShapeDtypeStruct((n, MIX_W), act_dtype),
        jax.ShapeDtypeStruct((n, MIX_W), F32),
        jax.ShapeDtypeStruct((n, MIX_W), BF16),
        jax.ShapeDtypeStruct((n, MIX_W), F32),
        jax.ShapeDtypeStruct((n, MIX_W), BF16),
        jax.ShapeDtypeStruct((n, MIX_W), F32),
        jax.ShapeDtypeStruct((n, MEM_W), act_dtype),
        jax.ShapeDtypeStruct((n, FOX_HEADS), F32),
        jax.ShapeDtypeStruct((n, MIX_W), BF16),
        jax.ShapeDtypeStruct((n, MIX_W), BF16),
    ]
    return pl.pallas_call(
        functools.partial(_fox_inproj_body, tiles_per_seq=tiles_per_seq, q_scale=q_scale,
                          bias_scale=bias_scale),
        out_shape=outs,
        grid=(n // tile,),
        in_specs=[row(D_MODEL), full((1, D_MODEL)), full(w.shape), full((256, 256)),
                  full((1, 256)), full((1, 256)), full((1, 256)), full((1, LANES)),
                  full((tile, tile)), full(sel.shape), full(ones.shape)],
        out_specs=[row(MIX_W), row(MIX_W), row(MIX_W), row(MIX_W), row(MIX_W), row(MIX_W),
                   row(MEM_W), row(FOX_HEADS), row(MIX_W), row(MIX_W)],
        scratch_shapes=[pltpu.VMEM((1, LANES), F32)],
        compiler_params=_cparams(("arbitrary",)),
        name="fox_inproj",
    )(x, g, w, bd, gq, gk, gmq, bf, tri, sel, ones)


def _ret_inproj_body(x_ref, g_ref, w_ref, bd_ref, gmq_ref, cos_ref, sin_ref,
                     q_ref, k_ref, v_ref, gate_ref, qm_ref):
    h = _rms_rows(x_ref[...], g_ref[...]).astype(BF16)
    cos = cos_ref[...]
    sin = sin_ref[...]
    rot = lambda y: y * cos + pltpu.roll(y, RET_D // 2, 1) * sin
    for j in range(3):
        y = _dot(h, w_ref[:, j * 256:(j + 1) * 256])
        for half in range(2):
            sl = slice(j * 256 + half * RET_D, j * 256 + (half + 1) * RET_D)
            q_ref[:, sl] = rot(y[:, half * RET_D:(half + 1) * RET_D])
    for j in range(3):
        y = _dot(h, w_ref[:, MIX_W + j * 256:MIX_W + (j + 1) * 256])
        for half in range(2):
            sl = slice(j * 256 + half * RET_D, j * 256 + (half + 1) * RET_D)
            k_ref[:, sl] = rot(y[:, half * RET_D:(half + 1) * RET_D]) * (RET_D ** -0.5)
    for j in range(3):
        sl = slice(j * 256, (j + 1) * 256)
        v_ref[:, sl] = _dot(h, w_ref[:, 2 * MIX_W + j * 256:2 * MIX_W + (j + 1) * 256])
    for j in range(3):
        sl = slice(j * 256, (j + 1) * 256)
        gate_ref[:, sl] = _dot(h, w_ref[:, 3 * MIX_W + j * 256:3 * MIX_W + (j + 1) * 256])
    y = _dot(h, w_ref[:, 4 * MIX_W:])
    qm_ref[...] = (_head_norm(y, bd_ref[...], gmq_ref[...]) * 0.125).astype(qm_ref.dtype)


def _ret_inproj(x, g, w, bd, gmq, cos, sin, *, tile, tiles_per_seq, act_dtype):
    n = x.shape[0]
    full = lambda shape: pl.BlockSpec(shape, lambda i: (0,) * len(shape))
    row = lambda width: pl.BlockSpec((tile, width), lambda i: (i, 0))
    tab = pl.BlockSpec((tile, RET_D), lambda i: (i % tiles_per_seq, 0))
    outs = [jax.ShapeDtypeStruct((n, MIX_W), F32)] * 4 + [jax.ShapeDtypeStruct((n, MEM_W), act_dtype)]
    return pl.pallas_call(
        _ret_inproj_body,
        out_shape=outs,
        grid=(n // tile,),
        in_specs=[row(D_MODEL), full((1, D_MODEL)), full(w.shape), full((256, 256)),
                  full((1, 256)), tab, tab],
        out_specs=[row(MIX_W)] * 4 + [row(MEM_W)],
        compiler_params=_cparams(("arbitrary",)),
        name="ret_inproj",
    )(x, g, w, bd, gmq, cos, sin)


def _fox_flash_body(q_ref, qx_ref, k_ref, kx_ref, v_ref, og_ref, o_ref, *, blk, pairs):
    qi = pl.program_id(2)
    heads = 2 * pairs
    lane = lax.broadcasted_iota(I32, (blk, LANES), 1)
    lo = lane < HEAD_D
    qs = []
    for h in range(heads):
        sl = slice((h // 2) * LANES, (h // 2 + 1) * LANES)
        q, qx = q_ref[:, sl], qx_ref[:, sl]
        keep = lo if h % 2 == 0 else jnp.logical_not(lo)
        keep_x = (lane >= 6 * (h % 2)) & (lane < 6 * (h % 2) + 6)
        qs.append(jnp.concatenate([jnp.where(keep, q, jnp.zeros_like(q)),
                                   jnp.where(keep_x, qx, jnp.zeros_like(qx))], axis=1))
    rows = lax.broadcasted_iota(I32, (blk, blk), 0)
    cols = lax.broadcasted_iota(I32, (blk, blk), 1)
    causal = rows >= cols

    def step(start, carry, diag):
        ms, ls, accs = carry
        new_m, new_l, new_acc = [], [], []
        for p in range(pairs):
            sl = slice(p * LANES, (p + 1) * LANES)
            k = jnp.concatenate([k_ref[pl.ds(start, blk), sl], kx_ref[pl.ds(start, blk), sl]], axis=1)
            v = v_ref[pl.ds(start, blk), sl]
            alphas, pvs = [], []
            for j in range(2):
                h = 2 * p + j
                s = _dot_nt(qs[h], k)
                if diag:
                    s = jnp.where(causal, s, NEG)
                m_new = jnp.maximum(ms[h], jnp.max(s, axis=-1, keepdims=True))
                alpha = jnp.exp2(ms[h] - m_new)
                e = jnp.exp2(s - m_new)
                new_l.append(alpha * ls[h] + jnp.sum(e, axis=-1, keepdims=True))
                new_m.append(m_new)
                alphas.append(alpha)
                pvs.append(_dot(e.astype(BF16), v))
            new_acc.append(accs[p] * jnp.where(lo, alphas[0], alphas[1]) + jnp.where(lo, pvs[0], pvs[1]))
        return tuple(new_m), tuple(new_l), tuple(new_acc)

    init = ((jnp.full((blk, 1), NEG, F32),) * heads, (jnp.zeros((blk, 1), F32),) * heads,
            (jnp.zeros((blk, LANES), F32),) * pairs)
    carry = lax.fori_loop(
        0, qi, lambda i, c: step(pl.multiple_of(i * blk, blk), c, False), init)
    _, ls, accs = step(pl.multiple_of(qi * blk, blk), carry, True)
    for p in range(pairs):
        sl = slice(p * LANES, (p + 1) * LANES)
        o = accs[p] / jnp.where(lo, ls[2 * p], ls[2 * p + 1])
        o_ref[:, sl] = (o * _sigmoid(og_ref[:, sl])).astype(o_ref.dtype)


def _fox_flash(q, qx, kb, kx, vb, og, *, batch, seq, blk, pairs):
    n = q.shape[0]
    nq = seq // blk
    w = pairs * LANES
    qblk = pl.BlockSpec((blk, w), lambda b, g, i: (b * nq + i, g))
    kvblk = pl.BlockSpec((seq, w), lambda b, g, i: (b, g))
    return pl.pallas_call(
        functools.partial(_fox_flash_body, blk=blk, pairs=pairs),
        out_shape=jax.ShapeDtypeStruct((n, MIX_W), BF16),
        grid=(batch, FOX_HEADS // (2 * pairs), nq),
        in_specs=[qblk, qblk, kvblk, kvblk, kvblk, qblk],
        out_specs=qblk,
        compiler_params=_cparams(("arbitrary",) * 3),
        name="fox_flash",
    )(q, qx, kb, kx, vb, og)


def _fox_decode_body(pt_ref, q_ref, kn_ref, vn_ref, lfn_ref, og_ref, kt_hbm, vt_hbm, lf_hbm, tri_ref,
                     o_ref, kbuf, vbuf, lfbuf, sems, qbd_ref, m_ref, l_ref, acc_ref, carry_ref, *,
                     pages_per_step):
    pp = pages_per_step
    j = pl.program_id(1)
    steps = pl.num_programs(1)
    step = pl.program_id(0) * steps + j
    slot = step % 2

    def page_copies(s, into):
        copies = []
        for i in range(pp):
            page = pt_ref[s * pp + i]
            copies += [pltpu.make_async_copy(kt_hbm.at[page], kbuf.at[into, i], sems.at[into]),
                       pltpu.make_async_copy(vt_hbm.at[page], vbuf.at[into, i], sems.at[into]),
                       pltpu.make_async_copy(lf_hbm.at[page], lfbuf.at[into, i], sems.at[into])]
        return copies

    def start_all(copies):
        for n, cp in enumerate(copies):
            cp.start(priority=1 if n % 3 == 1 else 0)

    @pl.when(step == 0)
    def _():
        start_all(page_copies(step, slot))

    @pl.when(step + 1 < pl.num_programs(0) * steps)
    def _():
        start_all(page_copies(step + 1, 1 - slot))

    for cp in page_copies(step, slot):
        cp.wait()

    nrow = FOX_HEADS * 8
    rowh = lax.broadcasted_iota(I32, (nrow, MIX_W), 0) // 8
    colh = lax.broadcasted_iota(I32, (nrow, MIX_W), 1) // HEAD_D
    diag = rowh == colh

    @pl.when(j == 0)
    def _():
        q = q_ref[...]
        qbd_ref[...] = jnp.where(diag, jnp.concatenate([q] * FOX_HEADS, axis=0), 0.0)
        m_ref[...] = jnp.full_like(m_ref, NEG)
        l_ref[...] = jnp.zeros_like(l_ref)
        acc_ref[...] = jnp.zeros_like(acc_ref)
        carry_ref[...] = jnp.zeros_like(carry_ref)

    qbd = qbd_ref[...]
    tri = tri_ref[...]

    def scores(kt, lf, carry):
        c = _dot_exact_rhs01_left(lf, tri) + carry
        bias = jnp.concatenate(
            [jnp.broadcast_to(c[h:h + 1, :], (8, PAGE)) for h in range(FOX_HEADS)], axis=0)
        return _dot(qbd, kt) - bias, c[:, PAGE - 1:PAGE]

    def update(ss, vts):
        m_old = m_ref[...]
        m_new = m_old
        for s in ss:
            m_new = jnp.maximum(m_new, jnp.max(s, axis=-1, keepdims=True))
        alpha = jnp.exp(m_old - m_new)
        l_new = alpha * l_ref[...]
        pv = None
        for s, vt in zip(ss, vts):
            p = jnp.exp(s - m_new)
            l_new = l_new + jnp.sum(p, axis=-1, keepdims=True)
            d = _dot_nt(p, vt)
            pv = d if pv is None else pv + d
        m_ref[...] = m_new
        l_ref[...] = l_new
        acc_ref[...] = alpha * acc_ref[...] + pv

    carry = carry_ref[...]
    ss, vts = [], []
    for i in range(pp):
        s, carry = scores(kbuf[slot, i].reshape(MIX_W, PAGE), lfbuf[slot, i], carry)
        ss.append(s)
        vts.append(vbuf[slot, i].reshape(MIX_W, PAGE))
    carry_ref[...] = carry
    update(ss, vts)

    @pl.when(j == pl.num_programs(1) - 1)
    def _():
        s, _ = scores(kn_ref[...].reshape(MIX_W, PAGE), lfn_ref[...], carry_ref[...])
        qpos = lax.broadcasted_iota(I32, (nrow, PAGE), 0) % 8
        kpos = lax.broadcasted_iota(I32, (nrow, PAGE), 1)
        s = jnp.where(qpos >= kpos, s, NEG)
        update([s], [vn_ref[...].reshape(MIX_W, PAGE)])
        o = jnp.where(diag, acc_ref[...] / l_ref[...], 0.0)
        out = o[0:8, :]
        for h in range(1, FOX_HEADS):
            out = out + o[h * 8:(h + 1) * 8, :]
        o_ref[...] = (out * _sigmoid(og_ref[...])).astype(o_ref.dtype)


def _dot_exact_rhs01_left(x, m01):
    a, b, c = _split3(x)
    return _dot(a, m01) + _dot(b, m01) + _dot(c, m01)


def _fox_decode(page_table, q, knt, vnt, lfnt, og, kt_pool, vt_pool, lf_pool, *, pages_per_step):
    nb, npages = page_table.shape
    pp = pages_per_step
    steps = npages // pp
    tri = jnp.triu(jnp.ones((PAGE, PAGE), BF16))
    pt = page_table.reshape(-1)

    per_b = lambda shape: pl.BlockSpec((None,) + shape, lambda b, j, pt_ref: (b,) + (0,) * len(shape))
    rows8 = lambda width: pl.BlockSpec((8, width), lambda b, j, pt_ref: (b, 0))
    pool = pl.BlockSpec(memory_space=pl.ANY)
    in_specs = [rows8(MIX_W), per_b((FOX_HEADS, HEAD_D, PAGE)), per_b((FOX_HEADS, HEAD_D, PAGE)),
                per_b((FOX_HEADS, PAGE)), rows8(MIX_W), pool, pool, pool,
                pl.BlockSpec((PAGE, PAGE), lambda b, j, pt_ref: (0, 0))]
    nrow = FOX_HEADS * 8
    return pl.pallas_call(
        functools.partial(_fox_decode_body, pages_per_step=pp),
        out_shape=jax.ShapeDtypeStruct((nb * 8, MIX_W), F32),
        grid_spec=pltpu.PrefetchScalarGridSpec(
            num_scalar_prefetch=1,
            grid=(nb, steps),
            in_specs=in_specs,
            out_specs=pl.BlockSpec((8, MIX_W), lambda b, j, pt_ref: (b, 0)),
            scratch_shapes=[pltpu.VMEM((2, pp, FOX_HEADS, HEAD_D, PAGE), F32),
                            pltpu.VMEM((2, pp, FOX_HEADS, HEAD_D, PAGE), F32),
                            pltpu.VMEM((2, pp, FOX_HEADS, PAGE), F32),
                            pltpu.SemaphoreType.DMA((2,)),
                            pltpu.VMEM((nrow, MIX_W), F32), pltpu.VMEM((nrow, 1), F32),
                            pltpu.VMEM((nrow, 1), F32), pltpu.VMEM((nrow, MIX_W), F32),
                            pltpu.VMEM((FOX_HEADS, 1), F32)]),
        compiler_params=_cparams(("arbitrary", "arbitrary")),
        name="fox_decode",
    )(pt, q, knt, vnt, lfnt, og, kt_pool, vt_pool, lf_pool, tri)


def _retention_body(q_ref, k_ref, v_ref, gate_ref, s0_ref, dmat_ref, qdec_ref, kdec_ref, gc_ref,
                    go_ref, mix_ref, state_ref, *, n_chunks, heads, unroll):
    go = go_ref[...]

    def chunk(n, states):
        rows = pl.ds(pl.multiple_of(n * RET_D, RET_D), RET_D)
        new_states = []
        for h in range(heads):
            cols = slice(h * RET_D, (h + 1) * RET_D)
            q = q_ref[rows, cols]
            k = k_ref[rows, cols]
            v = v_ref[rows, cols].astype(BF16)
            inner = _dot_nt(q.astype(BF16), k.astype(BF16)) * dmat_ref[h]
            o = (_dot(inner.astype(BF16), v)
                 + _dot((q * qdec_ref[h]).astype(BF16), states[h].astype(BF16)))
            new_states.append(gc_ref[h] * states[h] + _dot((k * kdec_ref[h]).T.astype(BF16), v))
            g = gate_ref[rows, cols]
            mix_ref[rows, cols] = (_rms_rows(o, go) * (g * _sigmoid(g))).astype(mix_ref.dtype)
        return tuple(new_states)

    states = lax.fori_loop(0, n_chunks, chunk, tuple(s0_ref[h] for h in range(heads)), unroll=unroll)
    for h in range(heads):
        state_ref[h] = states[h]


def _retention(q, k, v, gate, s0, tabs, go, *, batch, seq, out_dtype, heads):
    n = q.shape[0]
    n_chunks = seq // RET_D
    dmat, qdec, kdec, gc = tabs
    seq_blk = pl.BlockSpec((seq, heads * RET_D), lambda b, g: (b, g))
    tab = pl.BlockSpec((heads, RET_D, RET_D), lambda b, g: (g, 0, 0))
    st = pl.BlockSpec((None, heads, RET_D, RET_D), lambda b, g: (b, g, 0, 0))
    return pl.pallas_call(
        functools.partial(_retention_body, n_chunks=n_chunks, heads=heads,
                          unroll=2 if n_chunks % 2 == 0 else 1),
        out_shape=[jax.ShapeDtypeStruct((n, MIX_W), out_dtype),
                   jax.ShapeDtypeStruct((batch, RET_HEADS, RET_D, RET_D), F32)],
        grid=(batch, RET_HEADS // heads),
        in_specs=[seq_blk, seq_blk, seq_blk, seq_blk, st, tab, tab, tab, tab,
                  pl.BlockSpec((1, RET_D), lambda b, g: (0, 0))],
        out_specs=[seq_blk, st],
        compiler_params=_cparams(("arbitrary", "arbitrary")),
        name="retention",
    )(q, k, v, gate, s0, dmat, qdec, kdec, gc, go)


def _retention_tables(chunk):
    log_g = jnp.log1p(-jnp.exp2(-5.0 - jnp.arange(RET_HEADS, dtype=F32)))
    n = jnp.arange(RET_D, dtype=F32)
    diff = n[:, None] - n[None, :]
    valid = (n[:, None] < chunk) & (n[None, :] < chunk)
    dmat = jnp.where((diff >= 0) & valid,
                     jnp.exp(jnp.maximum(diff, 0.0)[None] * log_g[:, None, None]), 0.0)
    qdec = jnp.exp((n + 1.0)[None, :] * log_g[:, None])
    kdec = jnp.where(n[None, :] < chunk, jnp.exp((chunk - 1.0 - n)[None, :] * log_g[:, None]), 0.0)
    gc = jnp.exp(chunk * log_g)
    bc = lambda a: jnp.broadcast_to(a[:, :, None], (RET_HEADS, RET_D, RET_D))
    return (dmat, bc(qdec), bc(kdec),
            jnp.broadcast_to(gc[:, None, None], (RET_HEADS, RET_D, RET_D)))


def _mem_kv_body(mem_ref, g_ref, w_ref, bd_ref, gk_ref, k_ref, v_ref):
    h = _rms_rows(mem_ref[...], g_ref[...]).astype(BF16)
    k_ref[...] = _head_norm(_dot(h, w_ref[:, :MEM_W]), bd_ref[...], gk_ref[...])
    v_ref[...] = _dot(h, w_ref[:, MEM_W:])


def _mem_kv(mem, g, w, bd, gk, *, tile):
    depth = w.shape[0]
    n = mem.shape[0]
    per_layer = lambda shape: pl.BlockSpec((None,) + shape, lambda l, i: (l,) + (0,) * len(shape))
    out = pl.BlockSpec((None, tile, MEM_W), lambda l, i: (l, i, 0))
    return pl.pallas_call(
        _mem_kv_body,
        out_shape=[jax.ShapeDtypeStruct((depth, n, MEM_W), F32)] * 2,
        grid=(depth, n // tile),
        in_specs=[pl.BlockSpec((tile, D_MODEL), lambda l, i: (i, 0)),
                  per_layer((1, D_MODEL)), per_layer((D_MODEL, 2 * MEM_W)),
                  pl.BlockSpec((256, 256), lambda l, i: (0, 0)), per_layer((1, MEM_W))],
        out_specs=[out, out],
        compiler_params=_cparams(("arbitrary", "arbitrary")),
        name="mem_kv",
    )(mem, g, w, bd, gk)


def _mem_attend_body(q_ref, k_ref, v_ref, o_ref):
    t = q_ref.shape[0]
    lane = lax.broadcasted_iota(I32, (t, LANES), 1)
    lo = lane < HEAD_D
    for slab in range(MEM_W // LANES):
        sl = slice(slab * LANES, (slab + 1) * LANES)
        q = q_ref[:, sl]
        k = k_ref[:, sl].astype(BF16)
        v = v_ref[:, sl].astype(BF16)
        zq = jnp.zeros_like(q)
        outs = []
        for qh in (jnp.where(lo, q, zq), jnp.where(lo, zq, q)):
            s = _dot_nt(qh, k)
            p = jnp.exp(s - jnp.max(s, axis=-1, keepdims=True))
            p = p / jnp.sum(p, axis=-1, keepdims=True)
            outs.append(_dot(p.astype(BF16), v))
        o_ref[:, sl] = jnp.where(lo, outs[0], outs[1]).astype(o_ref.dtype)


def _mem_attend(qm, mk, mv, *, batch, seq, tile):
    n = qm.shape[0]
    nt = seq // tile
    kv = pl.BlockSpec((N_MEM, MEM_W), lambda b, i: (b, 0))
    return pl.pallas_call(
        _mem_attend_body,
        out_shape=jax.ShapeDtypeStruct((n, MEM_W), BF16),
        grid=(batch, nt),
        in_specs=[pl.BlockSpec((tile, MEM_W), lambda b, i: (b * nt + i, 0)), kv, kv],
        out_specs=pl.BlockSpec((tile, MEM_W), lambda b, i: (b * nt + i, 0)),
        compiler_params=_cparams(("arbitrary", "arbitrary")),
        name="mem_attend",
    )(qm, mk, mv)


def _mem_attend_t_body(q_ref, kt_ref, vt_ref, o_ref):
    t = q_ref.shape[0]
    lane = lax.broadcasted_iota(I32, (t, LANES), 1)
    lo = lane < HEAD_D
    for slab in range(MEM_W // LANES):
        sl = slice(slab * LANES, (slab + 1) * LANES)
        q = q_ref[:, sl]
        kt = kt_ref[2 * slab:2 * slab + 2].reshape(LANES, N_MEM)
        vt = vt_ref[2 * slab:2 * slab + 2].reshape(LANES, N_MEM)
        zq = jnp.zeros_like(q)
        outs = []
        for qh in (jnp.where(lo, q, zq), jnp.where(lo, zq, q)):
            s = _dot(qh, kt)
            p = jnp.exp(s - jnp.max(s, axis=-1, keepdims=True))
            p = p / jnp.sum(p, axis=-1, keepdims=True)
            outs.append(_dot_nt(p, vt))
        o_ref[:, sl] = jnp.where(lo, outs[0], outs[1])


def _mem_attend_t(qm, mkt, mvt, *, batch, tokens):
    kv = pl.BlockSpec((None, MEM_W // HEAD_D, HEAD_D, N_MEM), lambda b: (b, 0, 0, 0))
    return pl.pallas_call(
        _mem_attend_t_body,
        out_shape=jax.ShapeDtypeStruct((batch * tokens, MEM_W), F32),
        grid=(batch,),
        in_specs=[pl.BlockSpec((tokens, MEM_W), lambda b: (b, 0)), kv, kv],
        out_specs=pl.BlockSpec((tokens, MEM_W), lambda b: (b, 0)),
        compiler_params=_cparams(("arbitrary",)),
        name="mem_attend_t",
    )(qm, mkt, mvt)


def _outproj_router_body(x_ref, mix_ref, memo_ref, wa_ref, wb_ref, g_ref, wr_ref, br_ref, tri_ref,
                         xo_ref, h_ref, ri_ref, rw_ref, cnt_ref, carry_ref):
    sub = tri_ref.shape[0]

    @pl.when(pl.program_id(0) == 0)
    def _():
        carry_ref[...] = jnp.zeros_like(carry_ref)

    carry = carry_ref[...]
    for s0 in range(0, x_ref.shape[0], sub):
        rows = slice(s0, s0 + sub)
        x = (x_ref[rows, :] + _dot(mix_ref[rows, :].astype(BF16), wa_ref[...])
             + _dot(memo_ref[rows, :].astype(BF16), wb_ref[...]))
        xo_ref[rows, :] = x
        h = _rms_rows(x, g_ref[...])
        _to_token_major(h_ref.at[pl.ds(s0 * TOK_ROWS, sub * TOK_ROWS), :], h)
        carry = _route(h, wr_ref, br_ref[...], tri_ref[...], carry, ri_ref.at[rows, :], rw_ref.at[rows, :])
    carry_ref[...] = carry
    cnt_ref[...] = carry


def _route(h, wr_ref, br, tri, carry, ri_ref, rw_ref):
    t = h.shape[0]
    h_hi = h.astype(BF16)
    h_lo = (h - h_hi.astype(F32)).astype(BF16)
    d = _dot(h_hi, wr_ref[...])
    logits = d[:, :LANES] + d[:, LANES:] + _dot(h_lo, wr_ref[:, :LANES]) + br
    lane = lax.broadcasted_iota(I32, (t, LANES), 1)
    big = jnp.int32(1 << 20)
    rmax = lambda a: jnp.max(a, axis=-1, keepdims=True)
    rmin = lambda a: jnp.min(a, axis=-1, keepdims=True)
    rsum = lambda a: jnp.sum(a, axis=-1, keepdims=True)
    gmask = lane < N_GROUPS
    lg = jnp.where(gmask, logits, -jnp.inf)
    gmax = rmax(lg)
    gi = rmin(jnp.where(lg == gmax, lane, big))
    p_top = 1.0 / rsum(jnp.where(gmask, jnp.exp(logits - gmax), 0.0))
    first = ROUTE_LANE0 + EPG * gi
    emask = (lane >= first) & (lane < first + EPG)
    le = jnp.where(emask, logits, -jnp.inf)
    v1 = rmax(le)
    i1 = rmin(jnp.where(le == v1, lane, big))
    le2 = jnp.where(lane == i1, -jnp.inf, le)
    v2 = rmax(le2)
    i2 = rmin(jnp.where(le2 == v2, lane, big))
    e2 = jnp.exp(v2 - v1)
    w1 = p_top / (1.0 + e2)
    w2 = p_top * e2 / (1.0 + e2)

    onehot = ((lane == i1) | (lane == i2)).astype(F32)
    before = _dot(tri, onehot.astype(BF16)) + carry
    r1 = rsum(jnp.where(lane == i1, before, 0.0)).astype(I32)
    r2 = rsum(jnp.where(lane == i2, before, 0.0)).astype(I32)
    lane4 = lax.broadcasted_iota(I32, (t, 4), 1)
    ri_ref[...] = jnp.where(lane4 == 0, i1 - ROUTE_LANE0,
                            jnp.where(lane4 == 1, i2 - ROUTE_LANE0, jnp.where(lane4 == 2, r1, r2)))
    lane2 = lax.broadcasted_iota(I32, (t, 2), 1)
    rw_ref[...] = jnp.where(lane2 == 0, w1, w2)
    return carry + jnp.sum(onehot, axis=0, keepdims=True)


def _outproj_router(x, mix, memo, wa, wb, g, wr, br, *, tile):
    n = x.shape[0]
    sub = tile
    tri = jnp.tril(jnp.ones((sub, sub), BF16), -1)
    full = lambda shape: pl.BlockSpec(shape, lambda i: (0,) * len(shape))
    row = lambda width: pl.BlockSpec((tile, width), lambda i: (i, 0))
    return pl.pallas_call(
        _outproj_router_body,
        out_shape=[jax.ShapeDtypeStruct((n, D_MODEL), F32),
                   jax.ShapeDtypeStruct((n * TOK_ROWS, LANES), F32),
                   jax.ShapeDtypeStruct((n, 4), I32), jax.ShapeDtypeStruct((n, 2), F32),
                   jax.ShapeDtypeStruct((1, LANES), F32)],
        grid=(n // tile,),
        in_specs=[row(D_MODEL), row(MIX_W), row(MEM_W), full((MIX_W, D_MODEL)), full((MEM_W, D_MODEL)),
                  full((1, D_MODEL)), full((D_MODEL, 2 * LANES)), full((1, LANES)), full((sub, sub))],
        out_specs=[row(D_MODEL), pl.BlockSpec((tile * TOK_ROWS, LANES), lambda i: (i, 0)),
                   row(4), row(2), full((1, LANES))],
        scratch_shapes=[pltpu.VMEM((1, LANES), F32)],
        compiler_params=_cparams(("arbitrary",)),
        name="outproj_router",
    )(x, mix, memo, wa, wb, g, wr, br, tri)


TOK_ROWS = D_MODEL // LANES
DMA_UNROLL = 8


def _to_token_major(ref, x):
    t = x.shape[0]
    for s in range(TOK_ROWS):
        ref[pl.ds(s, t, stride=TOK_ROWS), :] = x[:, s * LANES:(s + 1) * LANES]


def _from_token_major(ref, t):
    return [ref[pl.ds(s, t, stride=TOK_ROWS), :] for s in range(TOK_ROWS)]


def _token_tile(ref, row):
    return ref.at[pl.ds(pl.multiple_of(row * TOK_ROWS, TOK_ROWS), TOK_ROWS), :]


def _moe_dispatch_body(eid_ref, rank_ref, off_ref, zt_ref, h_ref, xs_hbm, zero_ref, sem, *,
                       tile, expert_tile, n_expert_tiles):
    base = pl.program_id(0) * tile

    @pl.when(pl.program_id(0) == 0)
    def _():
        zero_ref[...] = jnp.zeros_like(zero_ref)
        rows = expert_tile * TOK_ROWS

        def fill(wait):
            def body(i, c):
                @pl.when(zt_ref[i] == 1)
                def _():
                    dst = xs_hbm.at[pl.ds(pl.multiple_of(i * rows, rows), rows), :]
                    cp = pltpu.make_async_copy(zero_ref, dst, sem)
                    cp.wait() if wait else cp.start()
                return c
            lax.fori_loop(0, n_expert_tiles, body, 0)

        fill(False)
        fill(True)

    def issue(t, c):
        for slot in range(2):
            a = 2 * (base + t) + slot
            dst = off_ref[eid_ref[a]] + rank_ref[a]
            pltpu.make_async_copy(_token_tile(h_ref, t), _token_tile(xs_hbm, dst), sem).start(priority=slot)
        return c

    lax.fori_loop(0, tile, issue, 0, unroll=DMA_UNROLL)

    def drain(t, c):
        pltpu.make_async_copy(_token_tile(h_ref, 0), _token_tile(xs_hbm, 0), sem).wait()
        return c

    lax.fori_loop(0, 2 * tile, drain, 0, unroll=DMA_UNROLL)


def _moe_dispatch(eid, rank, off, zero_tile, h_tm, *, tile, expert_tile):
    n = h_tm.shape[0] // TOK_ROWS
    n_expert_tiles = zero_tile.shape[0]
    return pl.pallas_call(
        functools.partial(_moe_dispatch_body, tile=tile, expert_tile=expert_tile,
                          n_expert_tiles=n_expert_tiles),
        out_shape=jax.ShapeDtypeStruct((n_expert_tiles * expert_tile * TOK_ROWS, LANES), F32),
        grid_spec=pltpu.PrefetchScalarGridSpec(
            num_scalar_prefetch=4, grid=(n // tile,),
            in_specs=[pl.BlockSpec((tile * TOK_ROWS, LANES), lambda i, *_: (i, 0))],
            out_specs=pl.BlockSpec(memory_space=pl.ANY),
            scratch_shapes=[pltpu.VMEM((expert_tile * TOK_ROWS, LANES), F32),
                            pltpu.SemaphoreType.DMA(())]),
        compiler_params=_cparams(("arbitrary",)),
        name="moe_dispatch",
    )(eid, rank, off, zero_tile, h_tm)


def _moe_experts_body(te_ref, nu_ref, x_ref, w1_ref, w3_ref, w2_ref, y_ref, *, tile):
    del te_ref

    @pl.when(pl.program_id(0) < nu_ref[0])
    def _():
        x = jnp.concatenate(_from_token_major(x_ref, tile), axis=1).astype(BF16)
        a = _dot(x, w1_ref[...].astype(BF16))
        b = _dot(x, w3_ref[...].astype(BF16))
        hid = a * _sigmoid(a) * b
        _to_token_major(y_ref, _dot(hid.astype(BF16), w2_ref[...].astype(BF16)))

    @pl.when(pl.program_id(0) >= nu_ref[0])
    def _():
        y_ref[...] = jnp.zeros_like(y_ref)


def _moe_experts(tile_expert, n_used, xs, w1, w3, w2, *, tile, layer):
    rows = xs.shape[0] // TOK_ROWS
    wspec = lambda shape: pl.BlockSpec((None, None) + shape, lambda i, te, nu: (layer, te[i], 0, 0))
    blk = pl.BlockSpec((tile * TOK_ROWS, LANES), lambda i, te, nu: (i, 0))
    return pl.pallas_call(
        functools.partial(_moe_experts_body, tile=tile),
        out_shape=jax.ShapeDtypeStruct(xs.shape, F32),
        grid_spec=pltpu.PrefetchScalarGridSpec(
            num_scalar_prefetch=2, grid=(rows // tile,),
            in_specs=[blk, wspec((D_MODEL, D_EXPERT)), wspec((D_MODEL, D_EXPERT)),
                      wspec((D_EXPERT, D_MODEL))],
            out_specs=blk),
        compiler_params=_cparams(("arbitrary",)),
        name="moe_experts",
    )(tile_expert, n_used, xs, w1, w3, w2)


def _moe_combine_body(eid_ref, rank_ref, off_ref, x_ref, rw_ref, ys_hbm, o_ref, buf_ref, sem, *, tile):
    base = pl.program_id(0) * tile

    def issue(t, c):
        for slot in range(2):
            a = 2 * (base + t) + slot
            src = off_ref[eid_ref[a]] + rank_ref[a]
            pltpu.make_async_copy(_token_tile(ys_hbm, src), _token_tile(buf_ref.at[slot], t), sem).start(priority=slot)
        return c

    lax.fori_loop(0, tile, issue, 0, unroll=DMA_UNROLL)

    def drain(t, c):
        pltpu.make_async_copy(_token_tile(ys_hbm, 0), _token_tile(buf_ref.at[0], 0), sem).wait()
        return c

    lax.fori_loop(0, 2 * tile, drain, 0, unroll=DMA_UNROLL)
    rw = rw_ref[...]
    y0 = _from_token_major(buf_ref.at[0], tile)
    y1 = _from_token_major(buf_ref.at[1], tile)
    for s in range(TOK_ROWS):
        sl = slice(s * LANES, (s + 1) * LANES)
        o_ref[:, sl] = x_ref[:, sl] + rw[:, 0:1] * y0[s] + rw[:, 1:2] * y1[s]


def _moe_combine(eid, rank, off, x, rw, ys, *, tile):
    n = x.shape[0]
    row = lambda width: pl.BlockSpec((tile, width), lambda i, *_: (i, 0))
    return pl.pallas_call(
        functools.partial(_moe_combine_body, tile=tile),
        out_shape=jax.ShapeDtypeStruct((n, D_MODEL), F32),
        grid_spec=pltpu.PrefetchScalarGridSpec(
            num_scalar_prefetch=3, grid=(n // tile,),
            in_specs=[row(D_MODEL), row(2), pl.BlockSpec(memory_space=pl.ANY)],
            out_specs=row(D_MODEL),
            scratch_shapes=[pltpu.VMEM((2, tile * TOK_ROWS, LANES), F32), pltpu.SemaphoreType.DMA(())]),
        compiler_params=_cparams(("arbitrary",)),
        name="moe_combine",
    )(eid, rank, off, x, rw, ys)


def _moe(x, h, ri, rw, cnt, w1, w3, w2, *, layer, expert_tile, token_tile):
    n = x.shape[0]
    n_tiles = (2 * n + N_EXPERTS * (expert_tile - 1)) // expert_tile + 1
    counts = cnt[0, ROUTE_LANE0:ROUTE_LANE0 + N_EXPERTS].astype(I32)
    padded = (counts + expert_tile - 1) // expert_tile * expert_tile
    ends = jnp.cumsum(padded)
    off = ends - padded
    n_used = (ends[-1] // expert_tile).reshape(1)
    tile_start = jnp.arange(n_tiles, dtype=I32) * expert_tile
    tile_expert = jnp.minimum(
        jnp.sum((tile_start[:, None] >= ends[None, :]).astype(I32), axis=1), N_EXPERTS - 1)
    last_of_expert = jnp.any((tile_start[:, None] + expert_tile == ends[None, :])
                             & (padded[None, :] > counts[None, :]), axis=1)
    zero_tile = (last_of_expert | (tile_start >= ends[-1])).astype(I32)
    eid = ri[:, 0:2].reshape(-1)
    rank = ri[:, 2:4].reshape(-1)
    xs = _moe_dispatch(eid, rank, off, zero_tile, h, tile=token_tile, expert_tile=expert_tile)
    ys = _moe_experts(tile_expert, n_used, xs, w1, w3, w2, tile=expert_tile, layer=layer)
    return _moe_combine(eid, rank, off, x, rw, ys, tile=token_tile)


def _block_diag_ones():
    r = jnp.arange(256) // HEAD_D
    return (r[:, None] == r[None, :]).astype(BF16)


def _tile_gain(g, reps):
    return jnp.tile(g.astype(F32), reps)[None, :]


def _rotary_tables(pos):
    half = RET_D // 2
    inv = 10000.0 ** (-jnp.arange(half, dtype=F32) / half)
    ang = pos.astype(F32)[:, None] * inv[None, :]
    cos, sin = jnp.cos(ang), jnp.sin(ang)
    return jnp.concatenate([cos, cos], axis=-1), jnp.concatenate([-sin, sin], axis=-1)


def kernel(x_prompt, x_sample, mem_prompt, cache_fox_k, cache_fox_v, cache_fox_logf, page_table,
           state_ret, cache_mem_k, cache_mem_v, w_in_fox, b_f_fox, g_q_fox, g_k_fox, w_in_ret, g_o_ret,
           g_attn, g_mem, w_mem_kv, g_mq, g_mk, w_out, g_ffn, w_rg, b_rg, w_re, b_re, w1, w3, w2):
    bp, sp, _ = x_prompt.shape
    bs, ts, _ = x_sample.shape
    depth = w_out.shape[0]
    n_pages = page_table.shape[1]
    past = n_pages * PAGE
    np_, ns = bp * sp, bs * ts
    tile_p, tile_s = 512, ns
    bd = _block_diag_ones()

    xp = x_prompt.reshape(np_, D_MODEL)
    xs = x_sample.reshape(ns, D_MODEL)
    mem = mem_prompt.reshape(bp * N_MEM, D_MODEL)
    mk_all, mv_all = _mem_kv(mem, g_mem[:, None, :], w_mem_kv.astype(BF16), bd,
                             jnp.tile(g_mk, (1, MEM_W // HEAD_D))[:, None, :], tile=512)

    outs = {}
    for l in range(depth):
        i = l // 2
        g_l = g_attn[l][None, :]
        gmq = _tile_gain(g_mq[l], 256 // HEAD_D)
        if l % 2 == 0:
            w = w_in_fox[i]
            w_pack = jnp.concatenate(
                [w[:, :4 * MIX_W], w[:, 4 * MIX_W + FOX_HEADS:], w[:, 4 * MIX_W:4 * MIX_W + FOX_HEADS],
                 jnp.zeros((D_MODEL, LANES - FOX_HEADS), F32)], axis=1).astype(BF16)
            bf = jnp.concatenate([b_f_fox[i], jnp.zeros((LANES - FOX_HEADS,), F32)])[None, :]
            gq = _tile_gain(g_q_fox[i], 256 // HEAD_D)
            gk = _tile_gain(g_k_fox[i], 256 // HEAD_D)
            fox = functools.partial(_fox_inproj, g=g_l, w=w_pack, bd=bd, gq=gq, gk=gk, gmq=gmq, bf=bf)
            qp, kp, kbp, vp, vbp, ogp, qmp, lfp, qxp, kxp = fox(
                xp, tile=tile_p, tiles_per_seq=sp // tile_p, act_dtype=BF16,
                q_scale=HEAD_D ** -0.5 * LOG2E, bias_scale=LOG2E)
            qs, ks, _, vs, _, ogs, qms, lfs, _, _ = fox(
                xs, tile=tile_s, tiles_per_seq=1, act_dtype=F32, q_scale=HEAD_D ** -0.5, bias_scale=1.0)
            mix_p = _fox_flash(qp, qxp, kbp, kxp, vbp, ogp, batch=bp, seq=sp, blk=256,
                               pairs=FOX_HEADS // 2)

            def pad_t(a):
                a = jnp.transpose(a.reshape(bs, ts, -1), (0, 2, 1))
                return jnp.pad(a, ((0, 0), (0, 0), (0, PAGE - ts)))

            knt = pad_t(ks).reshape(bs, FOX_HEADS, HEAD_D, PAGE)
            vnt = pad_t(vs).reshape(bs, FOX_HEADS, HEAD_D, PAGE)
            lfnt = pad_t(lfs)
            kt_pool = jnp.transpose(cache_fox_k[i], (0, 2, 3, 1))
            vt_pool = jnp.transpose(cache_fox_v[i], (0, 2, 3, 1))
            lf_pool = jnp.transpose(cache_fox_logf[i], (0, 2, 1))
            mix_s = _fox_decode(page_table, qs, knt, vnt, lfnt, ogs, kt_pool, vt_pool, lf_pool,
                                pages_per_step=16)
            outs.setdefault("fkp", []).append(kp.reshape(bp, sp, FOX_HEADS, HEAD_D))
            outs.setdefault("fvp", []).append(vp.reshape(bp, sp, FOX_HEADS, HEAD_D))
            outs.setdefault("flp", []).append(lfp.reshape(bp, sp, FOX_HEADS))
            outs.setdefault("fks", []).append(ks.reshape(bs, ts, FOX_HEADS, HEAD_D))
            outs.setdefault("fvs", []).append(vs.reshape(bs, ts, FOX_HEADS, HEAD_D))
            outs.setdefault("fls", []).append(lfs.reshape(bs, ts, FOX_HEADS))
        else:
            w = w_in_ret[i].astype(BF16)
            go = g_o_ret[i][None, :]
            cos_p, sin_p = _rotary_tables(jnp.arange(sp))
            cos_s, sin_s = _rotary_tables(past + jnp.arange(ts))
            ret = functools.partial(_ret_inproj, g=g_l, w=w, bd=bd, gmq=gmq)
            qp, kp, vp, gp, qmp = ret(xp, cos=cos_p, sin=sin_p, tile=tile_p,
                                      tiles_per_seq=sp // tile_p, act_dtype=BF16)
            qs, ks, vs, gs, qms = ret(xs, cos=jnp.tile(cos_s, (bs, 1)), sin=jnp.tile(sin_s, (bs, 1)),
                                      tile=tile_s, tiles_per_seq=1, act_dtype=F32)
            mix_p, sp_state = _retention(
                qp, kp, vp, gp, jnp.zeros((bp, RET_HEADS, RET_D, RET_D), F32),
                _retention_tables(RET_D), go, batch=bp, seq=sp, out_dtype=BF16, heads=3)
            pad_rows = lambda a: jnp.pad(a.reshape(bs, ts, MIX_W),
                                         ((0, 0), (0, RET_D - ts), (0, 0))).reshape(bs * RET_D, MIX_W)
            mix_s_pad, ss_state = _retention(
                pad_rows(qs), pad_rows(ks), pad_rows(vs), pad_rows(gs), state_ret[i],
                _retention_tables(ts), go, batch=bs, seq=RET_D, out_dtype=F32, heads=RET_HEADS)
            mix_s = mix_s_pad.reshape(bs, RET_D, MIX_W)[:, :ts].reshape(ns, MIX_W)
            outs.setdefault("rsp", []).append(sp_state)
            outs.setdefault("rss", []).append(ss_state)

        memo_p = _mem_attend(qmp, mk_all[l], mv_all[l], batch=bp, seq=sp, tile=tile_p)
        memo_s = _mem_attend_t(qms, jnp.transpose(cache_mem_k[l], (0, 2, 3, 1)),
                               jnp.transpose(cache_mem_v[l], (0, 2, 3, 1)), batch=bs, tokens=ts)

        wo = w_out[l].astype(BF16)
        wr = jnp.concatenate([w_rg[l], w_re[l], jnp.zeros((D_MODEL, LANES - N_GROUPS - N_EXPERTS), F32)], axis=1)
        wr_hi = wr.astype(BF16)
        wr_lo = (wr - wr_hi.astype(F32)).astype(BF16)
        br = jnp.concatenate([b_rg[l], b_re[l], jnp.zeros((LANES - N_GROUPS - N_EXPERTS,), F32)])[None, :]
        route = functools.partial(_outproj_router, wa=wo[:MIX_W], wb=wo[MIX_W:], g=g_ffn[l][None, :],
                                  wr=jnp.concatenate([wr_hi, wr_lo], axis=1), br=br)
        xp, hp, rip, rwp, cntp = route(xp, mix_p, memo_p, tile=tile_p)
        xs, hs, ris, rws, cnts = route(xs, mix_s, memo_s, tile=tile_s)
        xp = _moe(xp, hp, rip, rwp, cntp, w1, w3, w2, layer=l, expert_tile=256, token_tile=tile_p)
        xs = _moe(xs, hs, ris, rws, cnts, w1, w3, w2, layer=l, expert_tile=128, token_tile=tile_s)

    mem_shape = (depth, bp, N_MEM, MEM_W // HEAD_D, HEAD_D)
    return (xp.reshape(bp, sp, D_MODEL), xs.reshape(bs, ts, D_MODEL),
            jnp.stack(outs["fkp"]), jnp.stack(outs["fvp"]), jnp.stack(outs["flp"]),
            jnp.stack(outs["fks"]), jnp.stack(outs["fvs"]), jnp.stack(outs["fls"]),
            jnp.stack(outs["rsp"]), jnp.stack(outs["rss"]),
            mk_all.reshape(mem_shape), mv_all.reshape(mem_shape))
```

```python
import functools

import jax
import jax.numpy as jnp
from jax import lax
from jax.experimental import pallas as pl
from jax.experimental.pallas import tpu as pltpu

F32 = jnp.float32
BF16 = jnp.bfloat16
I32 = jnp.int32

D_MODEL = 1024
MIX_W = 768
MEM_W = 256
HEAD_D = 64
FOX_HEADS = 12
RET_HEADS = 6
RET_D = 128
N_MEM = 256
N_EXPERTS = 32
N_GROUPS = 4
EPG = 8
D_EXPERT = 256
PAGE = 128
EPS = 1e-6
NEG = -1e30
LANES = 128
VMEM_LIMIT = 56 * 1024 * 1024
ROUTE_LANE0 = N_GROUPS
LOG2E = 1.4426950408889634
SUBLANES = 8


def _cparams(sem, vmem=VMEM_LIMIT):
    return pltpu.CompilerParams(dimension_semantics=sem, vmem_limit_bytes=vmem)


def _dot(a, b):
    return jnp.dot(a, b, preferred_element_type=F32)


def _dot_nt(a, b):
    return lax.dot_general(a, b, (((1,), (1,)), ((), ())), preferred_element_type=F32)


def _split3(x):
    a = x.astype(BF16)
    r = x - a.astype(F32)
    b = r.astype(BF16)
    c = (r - b.astype(F32)).astype(BF16)
    return a, b, c


def _dot_exact_rhs01(m01, x):
    a, b, c = _split3(x)
    return _dot(m01, a) + _dot(m01, b) + _dot(m01, c)


def _rms_rows(x, gain):
    return x * lax.rsqrt(jnp.mean(x * x, axis=-1, keepdims=True) + EPS) * gain


def _head_norm(y, bd, gain):
    ss = _dot((y * y).astype(BF16), bd)
    return y * lax.rsqrt(ss * (1.0 / HEAD_D) + EPS) * gain


def _sigmoid(x):
    return 1.0 / (1.0 + jnp.exp(-x))


def _fox_inproj_body(x_ref, g_ref, w_ref, bd_ref, gq_ref, gk_ref, gmq_ref, bf_ref, tri_ref,
                     sel_ref, ones_ref,
                     q_ref, k_ref, kb_ref, v_ref, vb_ref, og_ref, qm_ref, lf_ref, qx_ref, kx_ref,
                     carry_ref, *, tiles_per_seq, q_scale, bias_scale):
    t = x_ref.shape[0]
    h = _rms_rows(x_ref[...], g_ref[...]).astype(BF16)
    bd = bd_ref[...]
    for j in range(3):
        sl = slice(j * 256, (j + 1) * 256)
        y = _dot(h, w_ref[:, sl])
        q_ref[:, sl] = (_head_norm(y, bd, gq_ref[...]) * q_scale).astype(q_ref.dtype)
    for j in range(3):
        sl = slice(j * 256, (j + 1) * 256)
        y = _dot(h, w_ref[:, MIX_W + j * 256:MIX_W + (j + 1) * 256])
        kn = _head_norm(y, bd, gk_ref[...])
        k_ref[:, sl] = kn
        kb_ref[:, sl] = kn.astype(BF16)
    for j in range(3):
        sl = slice(j * 256, (j + 1) * 256)
        y = _dot(h, w_ref[:, 2 * MIX_W + j * 256:2 * MIX_W + (j + 1) * 256])
        v_ref[:, sl] = y
        vb_ref[:, sl] = y.astype(BF16)
    for j in range(3):
        sl = slice(j * 256, (j + 1) * 256)
        og_ref[:, sl] = _dot(h, w_ref[:, 3 * MIX_W + j * 256:3 * MIX_W + (j + 1) * 256])
    y = _dot(h, w_ref[:, 4 * MIX_W:4 * MIX_W + MEM_W])
    qm_ref[...] = (_head_norm(y, bd, gmq_ref[...]) * 0.125).astype(qm_ref.dtype)
    f = _dot(h, w_ref[:, 4 * MIX_W + MEM_W:]) + bf_ref[...]
    lf = jnp.minimum(f, 0.0) - jnp.log1p(jnp.exp(-jnp.abs(f)))
    lane = lax.broadcasted_iota(I32, (t, LANES), 1)
    lf = jnp.where(lane < FOX_HEADS, lf, 0.0)
    lf_ref[...] = lf[:, :FOX_HEADS]

    @pl.when(pl.program_id(0) % tiles_per_seq == 0)
    def _():
        carry_ref[...] = jnp.zeros_like(carry_ref)

    c = _dot_exact_rhs01(tri_ref[...], lf) + carry_ref[...]
    carry_ref[...] = c[t - 1:t, :]
    pieces = jnp.concatenate(_split3(c * bias_scale), axis=1)
    x = _dot(pieces, sel_ref[...]) + ones_ref[...]
    qx_ref[...] = x[:, :MIX_W].astype(BF16)
    kx_ref[...] = x[:, MIX_W:].astype(BF16)


def _bias_columns():
    piece = jnp.arange(3 * LANES) // LANES
    head = jnp.arange(3 * LANES) % LANES
    lane = jnp.arange(2 * MIX_W)
    half = lane // MIX_W
    in_pair = lane % LANES
    lane_head = 2 * ((lane % MIX_W) // LANES) + in_pair // 6
    slot = in_pair % 6
    live = in_pair < 12
    takes_c = live & jnp.where(half == 0, slot < 3, slot >= 3)
    sel = ((head[:, None] == lane_head[None, :]) & takes_c[None, :]
           & (piece[:, None] == (slot % 3)[None, :]))
    sel = jnp.where(sel, jnp.where(half == 0, 1.0, -1.0)[None, :], 0.0).astype(BF16)
    ones = (live & jnp.logical_not(takes_c)).astype(F32)[None, :]
    return sel, ones


def _fox_inproj(x, g, w, bd, gq, gk, gmq, bf, *, tile, tiles_per_seq, act_dtype, q_scale, bias_scale):
    n = x.shape[0]
    tri = jnp.tril(jnp.ones((tile, tile), BF16))
    sel, ones = _bias_columns()
    full = lambda shape: pl.BlockSpec(shape, lambda i: (0,) * len(shape))
    row = lambda width: pl.BlockSpec((tile, width), lambda i: (i, 0))
    outs = [
        jax.ShapeDtypeStruct((n, MIX_W), act_dtype),
        jax.ShapeDtypeStruct((n, MIX_W), F32),
        jax.ShapeDtypeStruct((n, MIX_W), BF16),
        jax.ShapeDtypeStruct((n, MIX_W), F32),
        jax.ShapeDtypeStruct((n, MIX_W), BF16),
        jax.ShapeDtypeStruct((n, MIX_W), F32),
        jax.ShapeDtypeStruct((n, MEM_W), act_dtype),
        jax.ShapeDtypeStruct((n, FOX_HEADS), F32),
        jax.ShapeDtypeStruct((n, MIX_W), BF16),
        jax.ShapeDtypeStruct((n, MIX_W), BF16),
    ]
    return pl.pallas_call(
        functools.partial(_fox_inproj_body, tiles_per_seq=tiles_per_seq, q_scale=q_scale,
                          bias_scale=bias_scale),
        out_shape=outs,
        grid=(n // tile,),
        in_specs=[row(D_MODEL), full((1, D_MODEL)), full(w.shape), full((256, 256)),
                  full((1, 256)), full((1, 256)), full((1, 256)), full((1, LANES)),
                  full((tile, tile)), full(sel.shape), full(ones.shape)],
        out_specs=[row(MIX_W), row(MIX_W), row(MIX_W), row(MIX_W), row(MIX_W), row(MIX_W),
                   row(MEM_W), row(FOX_HEADS), row(MIX_W), row(MIX_W)],
        scratch_shapes=[pltpu.VMEM((1, LANES), F32)],
        compiler_params=_cparams(("arbitrary",)),
        name="fox_inproj",
    )(x, g, w, bd, gq, gk, gmq, bf, tri, sel, ones)


def _ret_inproj_body(x_ref, g_ref, w_ref, bd_ref, gmq_ref, cos_ref, sin_ref,
                     q_ref, k_ref, v_ref, gate_ref, qm_ref):
    h = _rms_rows(x_ref[...], g_ref[...]).astype(BF16)
    cos = cos_ref[...]
    sin = sin_ref[...]
    rot = lambda y: y * cos + pltpu.roll(y, RET_D // 2, 1) * sin
    for j in range(3):
        y = _dot(h, w_ref[:, j * 256:(j + 1) * 256])
        for half in range(2):
            sl = slice(j * 256 + half * RET_D, j * 256 + (half + 1) * RET_D)
            q_ref[:, sl] = rot(y[:, half * RET_D:(half + 1) * RET_D])
    for j in range(3):
        y = _dot(h, w_ref[:, MIX_W + j * 256:MIX_W + (j + 1) * 256])
        for half in range(2):
            sl = slice(j * 256 + half * RET_D, j * 256 + (half + 1) * RET_D)
            k_ref[:, sl] = rot(y[:, half * RET_D:(half + 1) * RET_D]) * (RET_D ** -0.5)
    for j in range(3):
        sl = slice(j * 256, (j + 1) * 256)
        v_ref[:, sl] = _dot(h, w_ref[:, 2 * MIX_W + j * 256:2 * MIX_W + (j + 1) * 256])
    for j in range(3):
        sl = slice(j * 256, (j + 1) * 256)
        gate_ref[:, sl] = _dot(h, w_ref[:, 3 * MIX_W + j * 256:3 * MIX_W + (j + 1) * 256])
    y = _dot(h, w_ref[:, 4 * MIX_W:])
    qm_ref[...] = (_head_norm(y, bd_ref[...], gmq_ref[...]) * 0.125).astype(qm_ref.dtype)


def _ret_inproj(x, g, w, bd, gmq, cos, sin, *, tile, tiles_per_seq, act_dtype):
    n = x.shape[0]
    full = lambda shape: pl.BlockSpec(shape, lambda i: (0,) * len(shape))
    row = lambda width: pl.BlockSpec((tile, width), lambda i: (i, 0))
    tab = pl.BlockSpec((tile, RET_D), lambda i: (i % tiles_per_seq, 0))
    outs = [jax.ShapeDtypeStruct((n, MIX_W), F32)] * 4 + [jax.ShapeDtypeStruct((n, MEM_W), act_dtype)]
    return pl.pallas_call(
        _ret_inproj_body,
        out_shape=outs,
        grid=(n // tile,),
        in_specs=[row(D_MODEL), full((1, D_MODEL)), full(w.shape), full((256, 256)),
                  full((1, 256)), tab, tab],
        out_specs=[row(MIX_W)] * 4 + [row(MEM_W)],
        compiler_params=_cparams(("arbitrary",)),
        name="ret_inproj",
    )(x, g, w, bd, gmq, cos, sin)


def _fox_flash_body(q_ref, qx_ref, k_ref, kx_ref, v_ref, og_ref, o_ref, *, blk, pairs):
    qi = pl.program_id(2)
    heads = 2 * pairs
    lane = lax.broadcasted_iota(I32, (blk, LANES), 1)
    lo = lane < HEAD_D
    qs = []
    for h in range(heads):
        sl = slice((h // 2) * LANES, (h // 2 + 1) * LANES)
        q, qx = q_ref[:, sl], qx_ref[:, sl]
        keep = lo if h % 2 == 0 else jnp.logical_not(lo)
        keep_x = (lane >= 6 * (h % 2)) & (lane < 6 * (h % 2) + 6)
        qs.append(jnp.concatenate([jnp.where(keep, q, jnp.zeros_like(q)),
                                   jnp.where(keep_x, qx, jnp.zeros_like(qx))], axis=1))
    rows = lax.broadcasted_iota(I32, (blk, blk), 0)
    cols = lax.broadcasted_iota(I32, (blk, blk), 1)
    causal = rows >= cols

    def step(start, carry, diag):
        ms, ls, accs = carry
        new_m, new_l, new_acc = [], [], []
        for p in range(pairs):
            sl = slice(p * LANES, (p + 1) * LANES)
            k = jnp.concatenate([k_ref[pl.ds(start, blk), sl], kx_ref[pl.ds(start, blk), sl]], axis=1)
            v = v_ref[pl.ds(start, blk), sl]
            alphas, pvs = [], []
            for j in range(2):
                h = 2 * p + j
                s = _dot_nt(qs[h], k)
                if diag:
                    s = jnp.where(causal, s, NEG)
                m_new = jnp.maximum(ms[h], jnp.max(s, axis=-1, keepdims=True))
                alpha = jnp.exp2(ms[h] - m_new)
                e = jnp.exp2(s - m_new)
                new_l.append(alpha * ls[h] + jnp.sum(e, axis=-1, keepdims=True))
                new_m.append(m_new)
                alphas.append(alpha)
                pvs.append(_dot(e.astype(BF16), v))
            new_acc.append(accs[p] * jnp.where(lo, alphas[0], alphas[1]) + jnp.where(lo, pvs[0], pvs[1]))
        return tuple(new_m), tuple(new_l), tuple(new_acc)

    init = ((jnp.full((blk, 1), NEG, F32),) * heads, (jnp.zeros((blk, 1), F32),) * heads,
            (jnp.zeros((blk, LANES), F32),) * pairs)
    carry = lax.fori_loop(
        0, qi, lambda i, c: step(pl.multiple_of(i * blk, blk), c, False), init)
    _, ls, accs = step(pl.multiple_of(qi * blk, blk), carry, True)
    for p in range(pairs):
        sl = slice(p * LANES, (p + 1) * LANES)
        o = accs[p] / jnp.where(lo, ls[2 * p], ls[2 * p + 1])
        o_ref[:, sl] = (o * _sigmoid(og_ref[:, sl])).astype(o_ref.dtype)


def _fox_flash(q, qx, kb, kx, vb, og, *, batch, seq, blk, pairs):
    n = q.shape[0]
    nq = seq // blk
    w = pairs * LANES
    qblk = pl.BlockSpec((blk, w), lambda b, g, i: (b * nq + i, g))
    kvblk = pl.BlockSpec((seq, w), lambda b, g, i: (b, g))
    return pl.pallas_call(
        functools.partial(_fox_flash_body, blk=blk, pairs=pairs),
        out_shape=jax.ShapeDtypeStruct((n, MIX_W), BF16),
        grid=(batch, FOX_HEADS // (2 * pairs), nq),
        in_specs=[qblk, qblk, kvblk, kvblk, kvblk, qblk],
        out_specs=qblk,
        compiler_params=_cparams(("arbitrary",) * 3),
        name="fox_flash",
    )(q, qx, kb, kx, vb, og)


def _fox_decode_body(pt_ref, q_ref, kn_ref, vn_ref, lfn_ref, og_ref, kt_hbm, vt_hbm, lf_hbm, tri_ref,
                     o_ref, kbuf, vbuf, lfbuf, sems, qbd_ref, m_ref, l_ref, acc_ref, carry_ref, *,
                     pages_per_step):
    pp = pages_per_step
    j = pl.program_id(1)
    steps = pl.num_programs(1)
    step = pl.program_id(0) * steps + j
    slot = step % 2

    def page_copies(s, into):
        copies = []
        for i in range(pp):
            page = pt_ref[s * pp + i]
            copies += [pltpu.make_async_copy(kt_hbm.at[page], kbuf.at[into, i], sems.at[into]),
                       pltpu.make_async_copy(vt_hbm.at[page], vbuf.at[into, i], sems.at[into]),
                       pltpu.make_async_copy(lf_hbm.at[page], lfbuf.at[into, i], sems.at[into])]
        return copies

    def start_all(copies):
        for n, cp in enumerate(copies):
            cp.start(priority=1 if n % 3 == 1 else 0)

    @pl.when(step == 0)
    def _():
        start_all(page_copies(step, slot))

    @pl.when(step + 1 < pl.num_programs(0) * steps)
    def _():
        start_all(page_copies(step + 1, 1 - slot))

    for cp in page_copies(step, slot):
        cp.wait()

    nrow = FOX_HEADS * 8
    rowh = lax.broadcasted_iota(I32, (nrow, MIX_W), 0) // 8
    colh = lax.broadcasted_iota(I32, (nrow, MIX_W), 1) // HEAD_D
    diag = rowh == colh

    @pl.when(j == 0)
    def _():
        q = q_ref[...]
        qbd_ref[...] = jnp.where(diag, jnp.concatenate([q] * FOX_HEADS, axis=0), 0.0)
        m_ref[...] = jnp.full_like(m_ref, NEG)
        l_ref[...] = jnp.zeros_like(l_ref)
        acc_ref[...] = jnp.zeros_like(acc_ref)
        carry_ref[...] = jnp.zeros_like(carry_ref)

    qbd = qbd_ref[...]
    tri = tri_ref[...]

    def scores(kt, lf, carry):
        c = _dot_exact_rhs01_left(lf, tri) + carry
        bias = jnp.concatenate(
            [jnp.broadcast_to(c[h:h + 1, :], (8, PAGE)) for h in range(FOX_HEADS)], axis=0)
        return _dot(qbd, kt) - bias, c[:, PAGE - 1:PAGE]

    def update(ss, vts):
        m_old = m_ref[...]
        m_new = m_old
        for s in ss:
            m_new = jnp.maximum(m_new, jnp.max(s, axis=-1, keepdims=True))
        alpha = jnp.exp(m_old - m_new)
        l_new = alpha * l_ref[...]
        pv = None
        for s, vt in zip(ss, vts):
            p = jnp.exp(s - m_new)
            l_new = l_new + jnp.sum(p, axis=-1, keepdims=True)
            d = _dot_nt(p, vt)
            pv = d if pv is None else pv + d
        m_ref[...] = m_new
        l_ref[...] = l_new
        acc_ref[...] = alpha * acc_ref[...] + pv

    carry = carry_ref[...]
    ss, vts = [], []
    for i in range(pp):
        s, carry = scores(kbuf[slot, i].reshape(MIX_W, PAGE), lfbuf[slot, i], carry)
        ss.append(s)
        vts.append(vbuf[slot, i].reshape(MIX_W, PAGE))
    carry_ref[...] = carry
    update(ss, vts)

    @pl.when(j == pl.num_programs(1) - 1)
    def _():
        s, _ = scores(kn_ref[...].reshape(MIX_W, PAGE), lfn_ref[...], carry_ref[...])
        qpos = lax.broadcasted_iota(I32, (nrow, PAGE), 0) % 8
        kpos = lax.broadcasted_iota(I32, (nrow, PAGE), 1)
        s = jnp.where(qpos >= kpos, s, NEG)
        update([s], [vn_ref[...].reshape(MIX_W, PAGE)])
        o = jnp.where(diag, acc_ref[...] / l_ref[...], 0.0)
        out = o[0:8, :]
        for h in range(1, FOX_HEADS):
            out = out + o[h * 8:(h + 1) * 8, :]
        o_ref[...] = (out * _sigmoid(og_ref[...])).astype(o_ref.dtype)


def _dot_exact_rhs01_left(x, m01):
    a, b, c = _split3(x)
    return _dot(a, m01) + _dot(b, m01) + _dot(c, m01)


def _fox_decode(page_table, q, knt, vnt, lfnt, og, kt_pool, vt_pool, lf_pool, *, pages_per_step):
    nb, npages = page_table.shape
    pp = pages_per_step
    steps = npages // pp
    tri = jnp.triu(jnp.ones((PAGE, PAGE), BF16))
    pt = page_table.reshape(-1)

    per_b = lambda shape: pl.BlockSpec((None,) + shape, lambda b, j, pt_ref: (b,) + (0,) * len(shape))
    rows8 = lambda width: pl.BlockSpec((8, width), lambda b, j, pt_ref: (b, 0))
    pool = pl.BlockSpec(memory_space=pl.ANY)
    in_specs = [rows8(MIX_W), per_b((FOX_HEADS, HEAD_D, PAGE)), per_b((FOX_HEADS, HEAD_D, PAGE)),
                per_b((FOX_HEADS, PAGE)), rows8(MIX_W), pool, pool, pool,
                pl.BlockSpec((PAGE, PAGE), lambda b, j, pt_ref: (0, 0))]
    nrow = FOX_HEADS * 8
    return pl.pallas_call(
        functools.partial(_fox_decode_body, pages_per_step=pp),
        out_shape=jax.ShapeDtypeStruct((nb * 8, MIX_W), F32),
        grid_spec=pltpu.PrefetchScalarGridSpec(
            num_scalar_prefetch=1,
            grid=(nb, steps),
            in_specs=in_specs,
            out_specs=pl.BlockSpec((8, MIX_W), lambda b, j, pt_ref: (b, 0)),
            scratch_shapes=[pltpu.VMEM((2, pp, FOX_HEADS, HEAD_D, PAGE), F32),
                            pltpu.VMEM((2, pp, FOX_HEADS, HEAD_D, PAGE), F32),
                            pltpu.VMEM((2, pp, FOX_HEADS, PAGE), F32),
                            pltpu.SemaphoreType.DMA((2,)),
                            pltpu.VMEM((nrow, MIX_W), F32), pltpu.VMEM((nrow, 1), F32),
                            pltpu.VMEM((nrow, 1), F32), pltpu.VMEM((nrow, MIX_W), F32),
                            pltpu.VMEM((FOX_HEADS, 1), F32)]),
        compiler_params=_cparams(("arbitrary", "arbitrary")),
        name="fox_decode",
    )(pt, q, knt, vnt, lfnt, og, kt_pool, vt_pool, lf_pool, tri)


def _retention_body(q_ref, k_ref, v_ref, gate_ref, s0_ref, dmat_ref, qdec_ref, kdec_ref, gc_ref,
                    go_ref, mix_ref, state_ref, *, n_chunks, heads, unroll):
    go = go_ref[...]

    def chunk(n, states):
        rows = pl.ds(pl.multiple_of(n * RET_D, RET_D), RET_D)
        new_states = []
        for h in range(heads):
            cols = slice(h * RET_D, (h + 1) * RET_D)
            q = q_ref[rows, cols]
            k = k_ref[rows, cols]
            v = v_ref[rows, cols].astype(BF16)
            inner = _dot_nt(q.astype(BF16), k.astype(BF16)) * dmat_ref[h]
            o = (_dot(inner.astype(BF16), v)
                 + _dot((q * qdec_ref[h]).astype(BF16), states[h].astype(BF16)))
            new_states.append(gc_ref[h] * states[h] + _dot((k * kdec_ref[h]).T.astype(BF16), v))
            g = gate_ref[rows, cols]
            mix_ref[rows, cols] = (_rms_rows(o, go) * (g * _sigmoid(g))).astype(mix_ref.dtype)
        return tuple(new_states)

    states = lax.fori_loop(0, n_chunks, chunk, tuple(s0_ref[h] for h in range(heads)), unroll=unroll)
    for h in range(heads):
        state_ref[h] = states[h]


def _retention(q, k, v, gate, s0, tabs, go, *, batch, seq, out_dtype, heads):
    n = q.shape[0]
    n_chunks = seq // RET_D
    dmat, qdec, kdec, gc = tabs
    seq_blk = pl.BlockSpec((seq, heads * RET_D), lambda b, g: (b, g))
    tab = pl.BlockSpec((heads, RET_D, RET_D), lambda b, g: (g, 0, 0))
    st = pl.BlockSpec((None, heads, RET_D, RET_D), lambda b, g: (b, g, 0, 0))
    return pl.pallas_call(
        functools.partial(_retention_body, n_chunks=n_chunks, heads=heads,
                          unroll=2 if n_chunks % 2 == 0 else 1),
        out_shape=[jax.ShapeDtypeStruct((n, MIX_W), out_dtype),
                   jax.ShapeDtypeStruct((batch, RET_HEADS, RET_D, RET_D), F32)],
        grid=(batch, RET_HEADS // heads),
        in_specs=[seq_blk, seq_blk, seq_blk, seq_blk, st, tab, tab, tab, tab,
                  pl.BlockSpec((1, RET_D), lambda b, g: (0, 0))],
        out_specs=[seq_blk, st],
        compiler_params=_cparams(("arbitrary", "arbitrary")),
        name="retention",
    )(q, k, v, gate, s0, dmat, qdec, kdec, gc, go)


def _retention_tables(chunk):
    log_g = jnp.log1p(-jnp.exp2(-5.0 - jnp.arange(RET_HEADS, dtype=F32)))
    n = jnp.arange(RET_D, dtype=F32)
    diff = n[:, None] - n[None, :]
    valid = (n[:, None] < chunk) & (n[None, :] < chunk)
    dmat = jnp.where((diff >= 0) & valid,
                     jnp.exp(jnp.maximum(diff, 0.0)[None] * log_g[:, None, None]), 0.0)
    qdec = jnp.exp((n + 1.0)[None, :] * log_g[:, None])
    kdec = jnp.where(n[None, :] < chunk, jnp.exp((chunk - 1.0 - n)[None, :] * log_g[:, None]), 0.0)
    gc = jnp.exp(chunk * log_g)
    bc = lambda a: jnp.broadcast_to(a[:, :, None], (RET_HEADS, RET_D, RET_D))
    return (dmat, bc(qdec), bc(kdec),
            jnp.broadcast_to(gc[:, None, None], (RET_HEADS, RET_D, RET_D)))


def _mem_kv_body(mem_ref, g_ref, w_ref, bd_ref, gk_ref, k_ref, v_ref):
    h = _rms_rows(mem_ref[...], g_ref[...]).astype(BF16)
    k_ref[...] = _head_norm(_dot(h, w_ref[:, :MEM_W]), bd_ref[...], gk_ref[...])
    v_ref[...] = _dot(h, w_ref[:, MEM_W:])


def _mem_kv(mem, g, w, bd, gk, *, tile):
    depth = w.shape[0]
    n = mem.shape[0]
    per_layer = lambda shape: pl.BlockSpec((None,) + shape, lambda l, i: (l,) + (0,) * len(shape))
    out = pl.BlockSpec((None, tile, MEM_W), lambda l, i: (l, i, 0))
    return pl.pallas_call(
        _mem_kv_body,
        out_shape=[jax.ShapeDtypeStruct((depth, n, MEM_W), F32)] * 2,
        grid=(depth, n // tile),
        in_specs=[pl.BlockSpec((tile, D_MODEL), lambda l, i: (i, 0)),
                  per_layer((1, D_MODEL)), per_layer((D_MODEL, 2 * MEM_W)),
                  pl.BlockSpec((256, 256), lambda l, i: (0, 0)), per_layer((1, MEM_W))],
        out_specs=[out, out],
        compiler_params=_cparams(("arbitrary", "arbitrary")),
        name="mem_kv",
    )(mem, g, w, bd, gk)


def _mem_attend_body(q_ref, k_ref, v_ref, o_ref):
    t = q_ref.shape[0]
    lane = lax.broadcasted_iota(I32, (t, LANES), 1)
    lo = lane < HEAD_D
    for slab in range(MEM_W // LANES):
        sl = slice(slab * LANES, (slab + 1) * LANES)
        q = q_ref[:, sl]
        k = k_ref[:, sl].astype(BF16)
        v = v_ref[:, sl].astype(BF16)
        zq = jnp.zeros_like(q)
        outs = []
        for qh in (jnp.where(lo, q, zq), jnp.where(lo, zq, q)):
            s = _dot_nt(qh, k)
            p = jnp.exp(s - jnp.max(s, axis=-1, keepdims=True))
            p = p / jnp.sum(p, axis=-1, keepdims=True)
            outs.append(_dot(p.astype(BF16), v))
        o_ref[:, sl] = jnp.where(lo, outs[0], outs[1]).astype(o_ref.dtype)


def _mem_attend(qm, mk, mv, *, layer, batch, seq, tile):
    n = qm.shape[0]
    nt = seq // tile
    kv = pl.BlockSpec((None, N_MEM, MEM_W), lambda b, i: (layer, b, 0))
    return pl.pallas_call(
        _mem_attend_body,
        out_shape=jax.ShapeDtypeStruct((n, MEM_W), BF16),
        grid=(batch, nt),
        in_specs=[pl.BlockSpec((tile, MEM_W), lambda b, i: (b * nt + i, 0)), kv, kv],
        out_specs=pl.BlockSpec((tile, MEM_W), lambda b, i: (b * nt + i, 0)),
        compiler_params=_cparams(("arbitrary", "arbitrary")),
        name="mem_attend",
    )(qm, mk, mv)


def _mem_attend_t_body(q_ref, kt_ref, vt_ref, o_ref):
    t = q_ref.shape[0]
    lane = lax.broadcasted_iota(I32, (t, LANES), 1)
    lo = lane < HEAD_D
    for slab in range(MEM_W // LANES):
        sl = slice(slab * LANES, (slab + 1) * LANES)
        q = q_ref[:, sl]
        kt = kt_ref[2 * slab:2 * slab + 2].reshape(LANES, N_MEM)
        vt = vt_ref[2 * slab:2 * slab + 2].reshape(LANES, N_MEM)
        zq = jnp.zeros_like(q)
        outs = []
        for qh in (jnp.where(lo, q, zq), jnp.where(lo, zq, q)):
            s = _dot(qh, kt)
            p = jnp.exp(s - jnp.max(s, axis=-1, keepdims=True))
            p = p / jnp.sum(p, axis=-1, keepdims=True)
            outs.append(_dot_nt(p, vt))
        o_ref[:, sl] = jnp.where(lo, outs[0], outs[1])


def _mem_attend_t(qm, mkt, mvt, *, layer, batch, tokens):
    kv = pl.BlockSpec((None, None, MEM_W // HEAD_D, HEAD_D, N_MEM), lambda b: (layer, b, 0, 0, 0))
    return pl.pallas_call(
        _mem_attend_t_body,
        out_shape=jax.ShapeDtypeStruct((batch * tokens, MEM_W), F32),
        grid=(batch,),
        in_specs=[pl.BlockSpec((tokens, MEM_W), lambda b: (b, 0)), kv, kv],
        out_specs=pl.BlockSpec((tokens, MEM_W), lambda b: (b, 0)),
        compiler_params=_cparams(("arbitrary",)),
        name="mem_attend_t",
    )(qm, mkt, mvt)


def _outproj_router_body(x_ref, mix_ref, memo_ref, wa_ref, wb_ref, g_ref, wr_ref, br_ref, tri_ref,
                         xo_ref, h_ref, ri_ref, rw_ref, cnt_ref, carry_ref):
    sub = tri_ref.shape[0]

    @pl.when(pl.program_id(0) == 0)
    def _():
        carry_ref[...] = jnp.zeros_like(carry_ref)

    carry = carry_ref[...]
    for s0 in range(0, x_ref.shape[0], sub):
        rows = slice(s0, s0 + sub)
        x = (x_ref[rows, :] + _dot(mix_ref[rows, :].astype(BF16), wa_ref[...])
             + _dot(memo_ref[rows, :].astype(BF16), wb_ref[...]))
        xo_ref[rows, :] = x
        h = _rms_rows(x, g_ref[...])
        _to_token_major(h_ref.at[pl.ds(s0 * TOK_ROWS, sub * TOK_ROWS), :], h)
        carry = _route(h, wr_ref, br_ref[...], tri_ref[...], carry, ri_ref.at[rows, :], rw_ref.at[rows, :])
    carry_ref[...] = carry
    cnt_ref[...] = carry


def _route(h, wr_ref, br, tri, carry, ri_ref, rw_ref):
    t = h.shape[0]
    h_hi = h.astype(BF16)
    h_lo = (h - h_hi.astype(F32)).astype(BF16)
    d = _dot(h_hi, wr_ref[...])
    logits = d[:, :LANES] + d[:, LANES:] + _dot(h_lo, wr_ref[:, :LANES]) + br
    lane = lax.broadcasted_iota(I32, (t, LANES), 1)
    big = jnp.int32(1 << 20)
    rmax = lambda a: jnp.max(a, axis=-1, keepdims=True)
    rmin = lambda a: jnp.min(a, axis=-1, keepdims=True)
    rsum = lambda a: jnp.sum(a, axis=-1, keepdims=True)
    gmask = lane < N_GROUPS
    lg = jnp.where(gmask, logits, -jnp.inf)
    gmax = rmax(lg)
    gi = rmin(jnp.where(lg == gmax, lane, big))
    p_top = 1.0 / rsum(jnp.where(gmask, jnp.exp(logits - gmax), 0.0))
    first = ROUTE_LANE0 + EPG * gi
    emask = (lane >= first) & (lane < first + EPG)
    le = jnp.where(emask, logits, -jnp.inf)
    v1 = rmax(le)
    i1 = rmin(jnp.where(le == v1, lane, big))
    le2 = jnp.where(lane == i1, -jnp.inf, le)
    v2 = rmax(le2)
    i2 = rmin(jnp.where(le2 == v2, lane, big))
    e2 = jnp.exp(v2 - v1)
    w1 = p_top / (1.0 + e2)
    w2 = p_top * e2 / (1.0 + e2)

    onehot = ((lane == i1) | (lane == i2)).astype(F32)
    before = _dot(tri, onehot.astype(BF16)) + carry
    r1 = rsum(jnp.where(lane == i1, before, 0.0)).astype(I32)
    r2 = rsum(jnp.where(lane == i2, before, 0.0)).astype(I32)
    lane4 = lax.broadcasted_iota(I32, (t, 4), 1)
    ri_ref[...] = jnp.where(lane4 == 0, i1 - ROUTE_LANE0,
                            jnp.where(lane4 == 1, i2 - ROUTE_LANE0, jnp.where(lane4 == 2, r1, r2)))
    lane2 = lax.broadcasted_iota(I32, (t, 2), 1)
    rw_ref[...] = jnp.where(lane2 == 0, w1, w2)
    return carry + jnp.sum(onehot, axis=0, keepdims=True)


def _outproj_router(x, mix, memo, wa, wb, g, wr, br, *, tile):
    n = x.shape[0]
    sub = tile
    tri = jnp.tril(jnp.ones((sub, sub), BF16), -1)
    full = lambda shape: pl.BlockSpec(shape, lambda i: (0,) * len(shape))
    row = lambda width: pl.BlockSpec((tile, width), lambda i: (i, 0))
    return pl.pallas_call(
        _outproj_router_body,
        out_shape=[jax.ShapeDtypeStruct((n, D_MODEL), F32),
                   jax.ShapeDtypeStruct((n * TOK_ROWS, LANES), F32),
                   jax.ShapeDtypeStruct((n, 4), I32), jax.ShapeDtypeStruct((n, 2), F32),
                   jax.ShapeDtypeStruct((1, LANES), F32)],
        grid=(n // tile,),
        in_specs=[row(D_MODEL), row(MIX_W), row(MEM_W), full((MIX_W, D_MODEL)), full((MEM_W, D_MODEL)),
                  full((1, D_MODEL)), full((D_MODEL, 2 * LANES)), full((1, LANES)), full((sub, sub))],
        out_specs=[row(D_MODEL), pl.BlockSpec((tile * TOK_ROWS, LANES), lambda i: (i, 0)),
                   row(4), row(2), full((1, LANES))],
        scratch_shapes=[pltpu.VMEM((1, LANES), F32)],
        compiler_params=_cparams(("arbitrary",)),
        name="outproj_router",
    )(x, mix, memo, wa, wb, g, wr, br, tri)


TOK_ROWS = D_MODEL // LANES
DMA_UNROLL = 8


def _to_token_major(ref, x):
    t = x.shape[0]
    for s in range(TOK_ROWS):
        ref[pl.ds(s, t, stride=TOK_ROWS), :] = x[:, s * LANES:(s + 1) * LANES]


def _from_token_major(ref, t):
    return [ref[pl.ds(s, t, stride=TOK_ROWS), :] for s in range(TOK_ROWS)]


def _token_tile(ref, row):
    return ref.at[pl.ds(pl.multiple_of(row * TOK_ROWS, TOK_ROWS), TOK_ROWS), :]


def _moe_dispatch_body(eid_ref, rank_ref, off_ref, zt_ref, h_ref, xs_hbm, zero_ref, sem, *,
                       tile, expert_tile, n_expert_tiles):
    base = pl.program_id(0) * tile

    @pl.when(pl.program_id(0) == 0)
    def _():
        zero_ref[...] = jnp.zeros_like(zero_ref)
        rows = expert_tile * TOK_ROWS

        def fill(wait):
            def body(i, c):
                @pl.when(zt_ref[i] == 1)
                def _():
                    dst = xs_hbm.at[pl.ds(pl.multiple_of(i * rows, rows), rows), :]
                    cp = pltpu.make_async_copy(zero_ref, dst, sem)
                    cp.wait() if wait else cp.start()
                return c
            lax.fori_loop(0, n_expert_tiles, body, 0)

        fill(False)
        fill(True)

    def issue(t, c):
        for slot in range(2):
            a = 2 * (base + t) + slot
            dst = off_ref[eid_ref[a]] + rank_ref[a]
            pltpu.make_async_copy(_token_tile(h_ref, t), _token_tile(xs_hbm, dst), sem).start(priority=slot)
        return c

    lax.fori_loop(0, tile, issue, 0, unroll=DMA_UNROLL)

    group = pl.ds(0, DMA_UNROLL * TOK_ROWS)
    drain_group = pltpu.make_async_copy(h_ref.at[group, :], xs_hbm.at[group, :], sem)

    def drain(t, c):
        drain_group.wait()
        return c

    lax.fori_loop(0, 2 * tile // DMA_UNROLL, drain, 0)


def _moe_dispatch(eid, rank, off, zero_tile, h_tm, *, tile, expert_tile):
    n = h_tm.shape[0] // TOK_ROWS
    n_expert_tiles = zero_tile.shape[0]
    return pl.pallas_call(
        functools.partial(_moe_dispatch_body, tile=tile, expert_tile=expert_tile,
                          n_expert_tiles=n_expert_tiles),
        out_shape=jax.ShapeDtypeStruct((n_expert_tiles * expert_tile * TOK_ROWS, LANES), F32),
        grid_spec=pltpu.PrefetchScalarGridSpec(
            num_scalar_prefetch=4, grid=(n // tile,),
            in_specs=[pl.BlockSpec((tile * TOK_ROWS, LANES), lambda i, *_: (i, 0))],
            out_specs=pl.BlockSpec(memory_space=pl.ANY),
            scratch_shapes=[pltpu.VMEM((expert_tile * TOK_ROWS, LANES), F32),
                            pltpu.SemaphoreType.DMA(())]),
        compiler_params=_cparams(("arbitrary",)),
        name="moe_dispatch",
    )(eid, rank, off, zero_tile, h_tm)


def _moe_experts_body(te_ref, nu_ref, x_ref, w1_ref, w3_ref, w2_ref, y_ref, *, tile):
    del te_ref

    @pl.when(pl.program_id(0) < nu_ref[0])
    def _():
        x = jnp.concatenate(_from_token_major(x_ref, tile), axis=1).astype(BF16)
        a = _dot(x, w1_ref[...])
        b = _dot(x, w3_ref[...])
        hid = a * _sigmoid(a) * b
        _to_token_major(y_ref, _dot(hid.astype(BF16), w2_ref[...]))

    @pl.when(pl.program_id(0) >= nu_ref[0])
    def _():
        y_ref[...] = jnp.zeros_like(y_ref)


def _moe_experts(tile_expert, n_used, xs, w1, w3, w2, *, tile, layer):
    rows = xs.shape[0] // TOK_ROWS
    wspec = lambda shape: pl.BlockSpec((None, None) + shape, lambda i, te, nu: (layer, te[i], 0, 0))
    blk = pl.BlockSpec((tile * TOK_ROWS, LANES), lambda i, te, nu: (i, 0))
    return pl.pallas_call(
        functools.partial(_moe_experts_body, tile=tile),
        out_shape=jax.ShapeDtypeStruct(xs.shape, F32),
        grid_spec=pltpu.PrefetchScalarGridSpec(
            num_scalar_prefetch=2, grid=(rows // tile,),
            in_specs=[blk, wspec((D_MODEL, D_EXPERT)), wspec((D_MODEL, D_EXPERT)),
                      wspec((D_EXPERT, D_MODEL))],
            out_specs=blk),
        compiler_params=_cparams(("arbitrary",)),
        name="moe_experts",
    )(tile_expert, n_used, xs, w1, w3, w2)


def _moe_combine_body(eid_ref, rank_ref, off_ref, x_ref, rw_ref, ys_hbm, o_ref, buf_ref, sems, *, tile):
    i = pl.program_id(0)
    cur = i % 2

    def gather(tile_idx, into):
        base = tile_idx * tile

        def issue(t, c):
            for slot in range(2):
                a = 2 * (base + t) + slot
                src = off_ref[eid_ref[a]] + rank_ref[a]
                pltpu.make_async_copy(_token_tile(ys_hbm, src), _token_tile(buf_ref.at[into, slot], t),
                                      sems.at[into]).start(priority=slot)
            return c

        lax.fori_loop(0, tile, issue, 0, unroll=DMA_UNROLL)

    @pl.when(i == 0)
    def _():
        gather(i, cur)

    @pl.when(i + 1 < pl.num_programs(0))
    def _():
        gather(i + 1, 1 - cur)

    group = pl.ds(0, DMA_UNROLL * TOK_ROWS)
    drain_group = pltpu.make_async_copy(ys_hbm.at[group, :], buf_ref.at[cur, 0, group, :], sems.at[cur])

    def drain(t, c):
        drain_group.wait()
        return c

    lax.fori_loop(0, 2 * tile // DMA_UNROLL, drain, 0)
    rw = rw_ref[...]
    y0 = _from_token_major(buf_ref.at[cur, 0], tile)
    y1 = _from_token_major(buf_ref.at[cur, 1], tile)
    for s in range(TOK_ROWS):
        sl = slice(s * LANES, (s + 1) * LANES)
        o_ref[:, sl] = x_ref[:, sl] + rw[:, 0:1] * y0[s] + rw[:, 1:2] * y1[s]


def _moe_combine(eid, rank, off, x, rw, ys, *, tile):
    n = x.shape[0]
    row = lambda width: pl.BlockSpec((tile, width), lambda i, *_: (i, 0))
    return pl.pallas_call(
        functools.partial(_moe_combine_body, tile=tile),
        out_shape=jax.ShapeDtypeStruct((n, D_MODEL), F32),
        grid_spec=pltpu.PrefetchScalarGridSpec(
            num_scalar_prefetch=3, grid=(n // tile,),
            in_specs=[row(D_MODEL), row(2), pl.BlockSpec(memory_space=pl.ANY)],
            out_specs=row(D_MODEL),
            scratch_shapes=[pltpu.VMEM((2, 2, tile * TOK_ROWS, LANES), F32),
                            pltpu.SemaphoreType.DMA((2,))]),
        compiler_params=_cparams(("arbitrary",)),
        name="moe_combine",
    )(eid, rank, off, x, rw, ys)


def _moe(x, h, ri, rw, cnt, w1, w3, w2, *, layer, expert_tile, token_tile):
    n = x.shape[0]
    n_tiles = (2 * n + N_EXPERTS * (expert_tile - 1)) // expert_tile + 1
    counts = cnt[0, ROUTE_LANE0:ROUTE_LANE0 + N_EXPERTS].astype(I32)
    padded = (counts + expert_tile - 1) // expert_tile * expert_tile
    ends = jnp.cumsum(padded)
    off = ends - padded
    n_used = (ends[-1] // expert_tile).reshape(1)
    tile_start = jnp.arange(n_tiles, dtype=I32) * expert_tile
    tile_expert = jnp.minimum(
        jnp.sum((tile_start[:, None] >= ends[None, :]).astype(I32), axis=1), N_EXPERTS - 1)
    last_of_expert = jnp.any((tile_start[:, None] + expert_tile == ends[None, :])
                             & (padded[None, :] > counts[None, :]), axis=1)
    zero_tile = (last_of_expert | (tile_start >= ends[-1])).astype(I32)
    eid = ri[:, 0:2].reshape(-1)
    rank = ri[:, 2:4].reshape(-1)
    xs = _moe_dispatch(eid, rank, off, zero_tile, h, tile=token_tile, expert_tile=expert_tile)
    ys = _moe_experts(tile_expert, n_used, xs, w1, w3, w2, tile=expert_tile, layer=layer)
    return _moe_combine(eid, rank, off, x, rw, ys, tile=token_tile)


def _block_diag_ones():
    r = jnp.arange(256) // HEAD_D
    return (r[:, None] == r[None, :]).astype(BF16)


def _tile_gain(g, reps):
    return jnp.tile(g.astype(F32), reps)[None, :]


def _rotary_tables(pos):
    half = RET_D // 2
    inv = 10000.0 ** (-jnp.arange(half, dtype=F32) / half)
    ang = pos.astype(F32)[:, None] * inv[None, :]
    cos, sin = jnp.cos(ang), jnp.sin(ang)
    return jnp.concatenate([cos, cos], axis=-1), jnp.concatenate([-sin, sin], axis=-1)


def kernel(x_prompt, x_sample, mem_prompt, cache_fox_k, cache_fox_v, cache_fox_logf, page_table,
           state_ret, cache_mem_k, cache_mem_v, w_in_fox, b_f_fox, g_q_fox, g_k_fox, w_in_ret, g_o_ret,
           g_attn, g_mem, w_mem_kv, g_mq, g_mk, w_out, g_ffn, w_rg, b_rg, w_re, b_re, w1, w3, w2):
    bp, sp, _ = x_prompt.shape
    bs, ts, _ = x_sample.shape
    depth = w_out.shape[0]
    n_pages = page_table.shape[1]
    past = n_pages * PAGE
    np_, ns = bp * sp, bs * ts
    tile_p, tile_s = 512, ns
    bd = _block_diag_ones()

    xp = x_prompt.reshape(np_, D_MODEL)
    xs = x_sample.reshape(ns, D_MODEL)
    mem = mem_prompt.reshape(bp * N_MEM, D_MODEL)
    mk_all, mv_all = _mem_kv(mem, g_mem[:, None, :], w_mem_kv.astype(BF16), bd,
                             jnp.tile(g_mk, (1, MEM_W // HEAD_D))[:, None, :], tile=512)

    w1b, w3b, w2b = w1.astype(BF16), w3.astype(BF16), w2.astype(BF16)
    mkt_cache = jnp.transpose(cache_mem_k, (0, 1, 3, 4, 2))
    mvt_cache = jnp.transpose(cache_mem_v, (0, 1, 3, 4, 2))

    outs = {}
    for l in range(depth):
        i = l // 2
        g_l = g_attn[l][None, :]
        gmq = _tile_gain(g_mq[l], 256 // HEAD_D)
        if l % 2 == 0:
            w = w_in_fox[i]
            w_pack = jnp.concatenate(
                [w[:, :4 * MIX_W], w[:, 4 * MIX_W + FOX_HEADS:], w[:, 4 * MIX_W:4 * MIX_W + FOX_HEADS],
                 jnp.zeros((D_MODEL, LANES - FOX_HEADS), F32)], axis=1).astype(BF16)
            bf = jnp.concatenate([b_f_fox[i], jnp.zeros((LANES - FOX_HEADS,), F32)])[None, :]
            gq = _tile_gain(g_q_fox[i], 256 // HEAD_D)
            gk = _tile_gain(g_k_fox[i], 256 // HEAD_D)
            fox = functools.partial(_fox_inproj, g=g_l, w=w_pack, bd=bd, gq=gq, gk=gk, gmq=gmq, bf=bf)
            qp, kp, kbp, vp, vbp, ogp, qmp, lfp, qxp, kxp = fox(
                xp, tile=tile_p, tiles_per_seq=sp // tile_p, act_dtype=BF16,
                q_scale=HEAD_D ** -0.5 * LOG2E, bias_scale=LOG2E)
            qs, ks, _, vs, _, ogs, qms, lfs, _, _ = fox(
                xs, tile=tile_s, tiles_per_seq=1, act_dtype=F32, q_scale=HEAD_D ** -0.5, bias_scale=1.0)
            mix_p = _fox_flash(qp, qxp, kbp, kxp, vbp, ogp, batch=bp, seq=sp, blk=256,
                               pairs=FOX_HEADS // 2)

            def pad_t(a):
                a = jnp.transpose(a.reshape(bs, ts, -1), (0, 2, 1))
                return jnp.pad(a, ((0, 0), (0, 0), (0, PAGE - ts)))

            knt = pad_t(ks).reshape(bs, FOX_HEADS, HEAD_D, PAGE)
            vnt = pad_t(vs).reshape(bs, FOX_HEADS, HEAD_D, PAGE)
            lfnt = pad_t(lfs)
            kt_pool = jnp.transpose(cache_fox_k[i], (0, 2, 3, 1))
            vt_pool = jnp.transpose(cache_fox_v[i], (0, 2, 3, 1))
            lf_pool = jnp.transpose(cache_fox_logf[i], (0, 2, 1))
            mix_s = _fox_decode(page_table, qs, knt, vnt, lfnt, ogs, kt_pool, vt_pool, lf_pool,
                                pages_per_step=16)
            outs.setdefault("fkp", []).append(kp.reshape(bp, sp, FOX_HEADS, HEAD_D))
            outs.setdefault("fvp", []).append(vp.reshape(bp, sp, FOX_HEADS, HEAD_D))
            outs.setdefault("flp", []).append(lfp.reshape(bp, sp, FOX_HEADS))
            outs.setdefault("fks", []).append(ks.reshape(bs, ts, FOX_HEADS, HEAD_D))
            outs.setdefault("fvs", []).append(vs.reshape(bs, ts, FOX_HEADS, HEAD_D))
            outs.setdefault("fls", []).append(lfs.reshape(bs, ts, FOX_HEADS))
        else:
            w = w_in_ret[i].astype(BF16)
            go = g_o_ret[i][None, :]
            cos_p, sin_p = _rotary_tables(jnp.arange(sp))
            cos_s, sin_s = _rotary_tables(past + jnp.arange(ts))
            ret = functools.partial(_ret_inproj, g=g_l, w=w, bd=bd, gmq=gmq)
            qp, kp, vp, gp, qmp = ret(xp, cos=cos_p, sin=sin_p, tile=tile_p,
                                      tiles_per_seq=sp // tile_p, act_dtype=BF16)
            qs, ks, vs, gs, qms = ret(xs, cos=jnp.tile(cos_s, (bs, 1)), sin=jnp.tile(sin_s, (bs, 1)),
                                      tile=tile_s, tiles_per_seq=1, act_dtype=F32)
            mix_p, sp_state = _retention(
                qp, kp, vp, gp, jnp.zeros((bp, RET_HEADS, RET_D, RET_D), F32),
                _retention_tables(RET_D), go, batch=bp, seq=sp, out_dtype=BF16, heads=3)
            pad_rows = lambda a: jnp.pad(a.reshape(bs, ts, MIX_W),
                                         ((0, 0), (0, RET_D - ts), (0, 0))).reshape(bs * RET_D, MIX_W)
            mix_s_pad, ss_state = _retention(
                pad_rows(qs), pad_rows(ks), pad_rows(vs), pad_rows(gs), state_ret[i],
                _retention_tables(ts), go, batch=bs, seq=RET_D, out_dtype=F32, heads=RET_HEADS)
            mix_s = mix_s_pad.reshape(bs, RET_D, MIX_W)[:, :ts].reshape(ns, MIX_W)
            outs.setdefault("rsp", []).append(sp_state)
            outs.setdefault("rss", []).append(ss_state)

        memo_p = _mem_attend(qmp, mk_all, mv_all, layer=l, batch=bp, seq=sp, tile=tile_p)
        memo_s = _mem_attend_t(qms, mkt_cache, mvt_cache, layer=l, batch=bs, tokens=ts)

        wo = w_out[l].astype(BF16)
        wr = jnp.concatenate([w_rg[l], w_re[l], jnp.zeros((D_MODEL, LANES - N_GROUPS - N_EXPERTS), F32)], axis=1)
        wr_hi = wr.astype(BF16)
        wr_lo = (wr - wr_hi.astype(F32)).astype(BF16)
        br = jnp.concatenate([b_rg[l], b_re[l], jnp.zeros((LANES - N_GROUPS - N_EXPERTS,), F32)])[None, :]
        route = functools.partial(_outproj_router, wa=wo[:MIX_W], wb=wo[MIX_W:], g=g_ffn[l][None, :],
                                  wr=jnp.concatenate([wr_hi, wr_lo], axis=1), br=br)
        xp, hp, rip, rwp, cntp = route(xp, mix_p, memo_p, tile=tile_p)
        xs, hs, ris, rws, cnts = route(xs, mix_s, memo_s, tile=tile_s)
        xp = _moe(xp, hp, rip, rwp, cntp, w1b, w3b, w2b, layer=l, expert_tile=256, token_tile=tile_p)
        xs = _moe(xs, hs, ris, rws, cnts, w1b, w3b, w2b, layer=l, expert_tile=128, token_tile=tile_s)

    mem_shape = (depth, bp, N_MEM, MEM_W // HEAD_D, HEAD_D)
    return (xp.reshape(bp, sp, D_MODEL), xs.reshape(bs, ts, D_MODEL),
            jnp.stack(outs["fkp"]), jnp.stack(outs["fvp"]), jnp.stack(outs["flp"]),
            jnp.stack(outs["fks"]), jnp.stack(outs["fvs"]), jnp.stack(outs["fls"]),
            jnp.stack(outs["rsp"]), jnp.stack(outs["rss"]),
            mk_all.reshape(mem_shape), mv_all.reshape(mem_shape))
```

```python
import functools

import jax
import jax.numpy as jnp
from jax import lax
from jax.experimental import pallas as pl
from jax.experimental.pallas import tpu as pltpu

F32 = jnp.float32
BF16 = jnp.bfloat16
I32 = jnp.int32

D_MODEL = 1024
MIX_W = 768
MEM_W = 256
HEAD_D = 64
FOX_HEADS = 12
RET_HEADS = 6
RET_D = 128
N_MEM = 256
N_EXPERTS = 32
N_GROUPS = 4
EPG = 8
D_EXPERT = 256
PAGE = 128
EPS = 1e-6
NEG = -1e30
LANES = 128
VMEM_LIMIT = 56 * 1024 * 1024
ROUTE_LANE0 = N_GROUPS
LOG2E = 1.4426950408889634
SUBLANES = 8


def _cparams(sem, vmem=VMEM_LIMIT):
    return pltpu.CompilerParams(dimension_semantics=sem, vmem_limit_bytes=vmem)


def _dot(a, b):
    return jnp.dot(a, b, preferred_element_type=F32)


def _dot_nt(a, b):
    return lax.dot_general(a, b, (((1,), (1,)), ((), ())), preferred_element_type=F32)


def _split3(x):
    a = x.astype(BF16)
    r = x - a.astype(F32)
    b = r.astype(BF16)
    c = (r - b.astype(F32)).astype(BF16)
    return a, b, c


def _dot_exact_rhs01(m01, x):
    a, b, c = _split3(x)
    return _dot(m01, a) + _dot(m01, b) + _dot(m01, c)


def _rms_rows(x, gain):
    return x * lax.rsqrt(jnp.mean(x * x, axis=-1, keepdims=True) + EPS) * gain


def _head_norm(y, bd, gain):
    ss = _dot((y * y).astype(BF16), bd)
    return y * lax.rsqrt(ss * (1.0 / HEAD_D) + EPS) * gain


def _sigmoid(x):
    return 1.0 / (1.0 + jnp.exp(-x))


def _fox_inproj_body(x_ref, g_ref, w_ref, bd_ref, gq_ref, gk_ref, gmq_ref, bf_ref, tri_ref,
                     sel_ref, ones_ref,
                     q_ref, k_ref, kb_ref, v_ref, vb_ref, og_ref, qm_ref, lf_ref, qx_ref, kx_ref,
                     carry_ref, *, tiles_per_seq, q_scale, bias_scale):
    t = x_ref.shape[0]
    h = _rms_rows(x_ref[...], g_ref[...]).astype(BF16)
    bd = bd_ref[...]
    for j in range(3):
        sl = slice(j * 256, (j + 1) * 256)
        y = _dot(h, w_ref[:, sl])
        q_ref[:, sl] = (_head_norm(y, bd, gq_ref[...]) * q_scale).astype(q_ref.dtype)
    for j in range(3):
        sl = slice(j * 256, (j + 1) * 256)
        y = _dot(h, w_ref[:, MIX_W + j * 256:MIX_W + (j + 1) * 256])
        kn = _head_norm(y, bd, gk_ref[...])
        k_ref[:, sl] = kn
        kb_ref[:, sl] = kn.astype(BF16)
    for j in range(3):
        sl = slice(j * 256, (j + 1) * 256)
        y = _dot(h, w_ref[:, 2 * MIX_W + j * 256:2 * MIX_W + (j + 1) * 256])
        v_ref[:, sl] = y
        vb_ref[:, sl] = y.astype(BF16)
    for j in range(3):
        sl = slice(j * 256, (j + 1) * 256)
        og_ref[:, sl] = _dot(h, w_ref[:, 3 * MIX_W + j * 256:3 * MIX_W + (j + 1) * 256])
    y = _dot(h, w_ref[:, 4 * MIX_W:4 * MIX_W + MEM_W])
    qm_ref[...] = (_head_norm(y, bd, gmq_ref[...]) * 0.125).astype(qm_ref.dtype)
    f = _dot(h, w_ref[:, 4 * MIX_W + MEM_W:]) + bf_ref[...]
    lf = jnp.minimum(f, 0.0) - jnp.log1p(jnp.exp(-jnp.abs(f)))
    lane = lax.broadcasted_iota(I32, (t, LANES), 1)
    lf = jnp.where(lane < FOX_HEADS, lf, 0.0)
    lf_ref[...] = lf[:, :FOX_HEADS]

    @pl.when(pl.program_id(0) % tiles_per_seq == 0)
    def _():
        carry_ref[...] = jnp.zeros_like(carry_ref)

    c = _dot_exact_rhs01(tri_ref[...], lf) + carry_ref[...]
    carry_ref[...] = c[t - 1:t, :]
    pieces = jnp.concatenate(_split3(c * bias_scale), axis=1)
    x = _dot(pieces, sel_ref[...]) + ones_ref[...]
    qx_ref[...] = x[:, :MIX_W].astype(BF16)
    kx_ref[...] = x[:, MIX_W:].astype(BF16)


def _bias_columns():
    piece = jnp.arange(3 * LANES) // LANES
    head = jnp.arange(3 * LANES) % LANES
    lane = jnp.arange(2 * MIX_W)
    half = lane // MIX_W
    in_pair = lane % LANES
    lane_head = 2 * ((lane % MIX_W) // LANES) + in_pair // 6
    slot = in_pair % 6
    live = in_pair < 12
    takes_c = live & jnp.where(half == 0, slot < 3, slot >= 3)
    sel = ((head[:, None] == lane_head[None, :]) & takes_c[None, :]
           & (piece[:, None] == (slot % 3)[None, :]))
    sel = jnp.where(sel, jnp.where(half == 0, 1.0, -1.0)[None, :], 0.0).astype(BF16)
    ones = (live & jnp.logical_not(takes_c)).astype(F32)[None, :]
    return sel, ones


def _fox_inproj(x, g, w, bd, gq, gk, gmq, bf, *, tile, tiles_per_seq, act_dtype, q_scale, bias_scale):
    n = x.shape[0]
    tri = jnp.tril(jnp.ones((tile, tile), BF16))
    sel, ones = _bias_columns()
    full = lambda shape: pl.BlockSpec(shape, lambda i: (0,) * len(shape))
    row = lambda width: pl.BlockSpec((tile, width), lambda i: (i, 0))
    outs = [
        jax.ShapeDtypeStruct((n, MIX_W), act_dtype),
        jax.ShapeDtypeStruct((n, MIX_W), F32),
        jax.ShapeDtypeStruct((n, MIX_W), BF16),
        jax.ShapeDtypeStruct((n, MIX_W), F32),
        jax.ShapeDtypeStruct((n, MIX_W), BF16),
        jax.ShapeDtypeStruct((n, MIX_W), F32),
        jax.ShapeDtypeStruct((n, MEM_W), act_dtype),
        jax.ShapeDtypeStruct((n, FOX_HEADS), F32),
        jax.ShapeDtypeStruct((n, MIX_W), BF16),
        jax.ShapeDtypeStruct((n, MIX_W), BF16),
    ]
    return pl.pallas_call(
        functools.partial(_fox_inproj_body, tiles_per_seq=tiles_per_seq, q_scale=q_scale,
                          bias_scale=bias_scale),
        out_shape=outs,
        grid=(n // tile,),
        in_specs=[row(D_MODEL), full((1, D_MODEL)), full(w.shape), full((256, 256)),
                  full((1, 256)), full((1, 256)), full((1, 256)), full((1, LANES)),
                  full((tile, tile)), full(sel.shape), full(ones.shape)],
        out_specs=[row(MIX_W), row(MIX_W), row(MIX_W), row(MIX_W), row(MIX_W), row(MIX_W),
                   row(MEM_W), row(FOX_HEADS), row(MIX_W), row(MIX_W)],
        scratch_shapes=[pltpu.VMEM((1, LANES), F32)],
        compiler_params=_cparams(("arbitrary",)),
        name="fox_inproj",
    )(x, g, w, bd, gq, gk, gmq, bf, tri, sel, ones)


def _ret_inproj_body(x_ref, g_ref, w_ref, bd_ref, gmq_ref, cos_ref, sin_ref,
                     q_ref, k_ref, v_ref, gate_ref, qm_ref):
    h = _rms_rows(x_ref[...], g_ref[...]).astype(BF16)
    cos = cos_ref[...]
    sin = sin_ref[...]
    rot = lambda y: y * cos + pltpu.roll(y, RET_D // 2, 1) * sin
    for j in range(3):
        y = _dot(h, w_ref[:, j * 256:(j + 1) * 256])
        for half in range(2):
            sl = slice(j * 256 + half * RET_D, j * 256 + (half + 1) * RET_D)
            q_ref[:, sl] = rot(y[:, half * RET_D:(half + 1) * RET_D])
    for j in range(3):
        y = _dot(h, w_ref[:, MIX_W + j * 256:MIX_W + (j + 1) * 256])
        for half in range(2):
            sl = slice(j * 256 + half * RET_D, j * 256 + (half + 1) * RET_D)
            k_ref[:, sl] = rot(y[:, half * RET_D:(half + 1) * RET_D]) * (RET_D ** -0.5)
    for j in range(3):
        sl = slice(j * 256, (j + 1) * 256)
        v_ref[:, sl] = _dot(h, w_ref[:, 2 * MIX_W + j * 256:2 * MIX_W + (j + 1) * 256])
    for j in range(3):
        sl = slice(j * 256, (j + 1) * 256)
        gate_ref[:, sl] = _dot(h, w_ref[:, 3 * MIX_W + j * 256:3 * MIX_W + (j + 1) * 256])
    y = _dot(h, w_ref[:, 4 * MIX_W:])
    qm_ref[...] = (_head_norm(y, bd_ref[...], gmq_ref[...]) * 0.125).astype(qm_ref.dtype)


def _ret_inproj(x, g, w, bd, gmq, cos, sin, *, tile, tiles_per_seq, act_dtype):
    n = x.shape[0]
    full = lambda shape: pl.BlockSpec(shape, lambda i: (0,) * len(shape))
    row = lambda width: pl.BlockSpec((tile, width), lambda i: (i, 0))
    tab = pl.BlockSpec((tile, RET_D), lambda i: (i % tiles_per_seq, 0))
    outs = [jax.ShapeDtypeStruct((n, MIX_W), F32)] * 4 + [jax.ShapeDtypeStruct((n, MEM_W), act_dtype)]
    return pl.pallas_call(
        _ret_inproj_body,
        out_shape=outs,
        grid=(n // tile,),
        in_specs=[row(D_MODEL), full((1, D_MODEL)), full(w.shape), full((256, 256)),
                  full((1, 256)), tab, tab],
        out_specs=[row(MIX_W)] * 4 + [row(MEM_W)],
        compiler_params=_cparams(("arbitrary",)),
        name="ret_inproj",
    )(x, g, w, bd, gmq, cos, sin)


def _fox_flash_body(q_ref, qx_ref, k_ref, kx_ref, v_ref, og_ref, o_ref, *, blk, pairs):
    qi = pl.program_id(2)
    heads = 2 * pairs
    lane = lax.broadcasted_iota(I32, (blk, LANES), 1)
    lo = lane < HEAD_D
    qs = []
    for h in range(heads):
        sl = slice((h // 2) * LANES, (h // 2 + 1) * LANES)
        q, qx = q_ref[:, sl], qx_ref[:, sl]
        keep = lo if h % 2 == 0 else jnp.logical_not(lo)
        keep_x = (lane >= 6 * (h % 2)) & (lane < 6 * (h % 2) + 6)
        qs.append(jnp.concatenate([jnp.where(keep, q, jnp.zeros_like(q)),
                                   jnp.where(keep_x, qx, jnp.zeros_like(qx))], axis=1))
    rows = lax.broadcasted_iota(I32, (blk, blk), 0)
    cols = lax.broadcasted_iota(I32, (blk, blk), 1)
    causal = rows >= cols

    def step(start, carry, diag):
        ms, ls, accs = carry
        new_m, new_l, new_acc = [], [], []
        for p in range(pairs):
            sl = slice(p * LANES, (p + 1) * LANES)
            k = jnp.concatenate([k_ref[pl.ds(start, blk), sl], kx_ref[pl.ds(start, blk), sl]], axis=1)
            v = v_ref[pl.ds(start, blk), sl]
            alphas, pvs = [], []
            for j in range(2):
                h = 2 * p + j
                s = _dot_nt(qs[h], k)
                if diag:
                    s = jnp.where(causal, s, NEG)
                m_new = jnp.maximum(ms[h], jnp.max(s, axis=-1, keepdims=True))
                alpha = jnp.exp2(ms[h] - m_new)
                e = jnp.exp2(s - m_new)
                new_l.append(alpha * ls[h] + jnp.sum(e, axis=-1, keepdims=True))
                new_m.append(m_new)
                alphas.append(alpha)
                pvs.append(_dot(e.astype(BF16), v))
            new_acc.append(accs[p] * jnp.where(lo, alphas[0], alphas[1]) + jnp.where(lo, pvs[0], pvs[1]))
        return tuple(new_m), tuple(new_l), tuple(new_acc)

    init = ((jnp.full((blk, 1), NEG, F32),) * heads, (jnp.zeros((blk, 1), F32),) * heads,
            (jnp.zeros((blk, LANES), F32),) * pairs)
    carry = lax.fori_loop(
        0, qi, lambda i, c: step(pl.multiple_of(i * blk, blk), c, False), init)
    _, ls, accs = step(pl.multiple_of(qi * blk, blk), carry, True)
    for p in range(pairs):
        sl = slice(p * LANES, (p + 1) * LANES)
        o = accs[p] / jnp.where(lo, ls[2 * p], ls[2 * p + 1])
        o_ref[:, sl] = (o * _sigmoid(og_ref[:, sl])).astype(o_ref.dtype)


def _fox_flash(q, qx, kb, kx, vb, og, *, batch, seq, blk, pairs):
    n = q.shape[0]
    nq = seq // blk
    w = pairs * LANES
    qblk = pl.BlockSpec((blk, w), lambda b, g, i: (b * nq + i, g))
    kvblk = pl.BlockSpec((seq, w), lambda b, g, i: (b, g))
    return pl.pallas_call(
        functools.partial(_fox_flash_body, blk=blk, pairs=pairs),
        out_shape=jax.ShapeDtypeStruct((n, MIX_W), BF16),
        grid=(batch, FOX_HEADS // (2 * pairs), nq),
        in_specs=[qblk, qblk, kvblk, kvblk, kvblk, qblk],
        out_specs=qblk,
        compiler_params=_cparams(("arbitrary",) * 3),
        name="fox_flash",
    )(q, qx, kb, kx, vb, og)


def _fox_decode_body(pt_ref, q_ref, kn_ref, vn_ref, lfn_ref, og_ref, kt_hbm, vt_hbm, lf_hbm, tri_ref,
                     o_ref, kbuf, vbuf, lfbuf, sems, qbd_ref, m_ref, l_ref, acc_ref, carry_ref, *,
                     pages_per_step):
    pp = pages_per_step
    j = pl.program_id(1)
    steps = pl.num_programs(1)
    step = pl.program_id(0) * steps + j
    slot = step % 2

    def page_copies(s, into):
        copies = []
        for i in range(pp):
            page = pt_ref[s * pp + i]
            copies += [pltpu.make_async_copy(kt_hbm.at[page], kbuf.at[into, i], sems.at[into]),
                       pltpu.make_async_copy(vt_hbm.at[page], vbuf.at[into, i], sems.at[into]),
                       pltpu.make_async_copy(lf_hbm.at[page], lfbuf.at[into, i], sems.at[into])]
        return copies

    def start_all(copies):
        for n, cp in enumerate(copies):
            cp.start(priority=1 if n % 3 == 1 else 0)

    @pl.when(step == 0)
    def _():
        start_all(page_copies(step, slot))

    @pl.when(step + 1 < pl.num_programs(0) * steps)
    def _():
        start_all(page_copies(step + 1, 1 - slot))

    for cp in page_copies(step, slot):
        cp.wait()

    nrow = FOX_HEADS * 8
    rowh = lax.broadcasted_iota(I32, (nrow, MIX_W), 0) // 8
    colh = lax.broadcasted_iota(I32, (nrow, MIX_W), 1) // HEAD_D
    diag = rowh == colh

    @pl.when(j == 0)
    def _():
        q = q_ref[...]
        qbd_ref[...] = jnp.where(diag, jnp.concatenate([q] * FOX_HEADS, axis=0), 0.0)
        m_ref[...] = jnp.full_like(m_ref, NEG)
        l_ref[...] = jnp.zeros_like(l_ref)
        acc_ref[...] = jnp.zeros_like(acc_ref)
        carry_ref[...] = jnp.zeros_like(carry_ref)

    qbd = qbd_ref[...]
    tri = tri_ref[...]

    def scores(kt, lf, carry):
        c = _dot_exact_rhs01_left(lf, tri) + carry
        bias = jnp.concatenate(
            [jnp.broadcast_to(c[h:h + 1, :], (8, PAGE)) for h in range(FOX_HEADS)], axis=0)
        return _dot(qbd, kt) - bias, c[:, PAGE - 1:PAGE]

    def update(ss, vts):
        m_old = m_ref[...]
        m_new = m_old
        for s in ss:
            m_new = jnp.maximum(m_new, jnp.max(s, axis=-1, keepdims=True))
        alpha = jnp.exp(m_old - m_new)
        l_new = alpha * l_ref[...]
        pv = None
        for s, vt in zip(ss, vts):
            p = jnp.exp(s - m_new)
            l_new = l_new + jnp.sum(p, axis=-1, keepdims=True)
            d = _dot_nt(p, vt)
            pv = d if pv is None else pv + d
        m_ref[...] = m_new
        l_ref[...] = l_new
        acc_ref[...] = alpha * acc_ref[...] + pv

    carry = carry_ref[...]
    ss, vts = [], []
    for i in range(pp):
        s, carry = scores(kbuf[slot, i].reshape(MIX_W, PAGE), lfbuf[slot, i], carry)
        ss.append(s)
        vts.append(vbuf[slot, i].reshape(MIX_W, PAGE))
    carry_ref[...] = carry
    update(ss, vts)

    @pl.when(j == pl.num_programs(1) - 1)
    def _():
        s, _ = scores(kn_ref[...].reshape(MIX_W, PAGE), lfn_ref[...], carry_ref[...])
        qpos = lax.broadcasted_iota(I32, (nrow, PAGE), 0) % 8
        kpos = lax.broadcasted_iota(I32, (nrow, PAGE), 1)
        s = jnp.where(qpos >= kpos, s, NEG)
        update([s], [vn_ref[...].reshape(MIX_W, PAGE)])
        o = jnp.where(diag, acc_ref[...] / l_ref[...], 0.0)
        out = o[0:8, :]
        for h in range(1, FOX_HEADS):
            out = out + o[h * 8:(h + 1) * 8, :]
        o_ref[...] = (out * _sigmoid(og_ref[...])).astype(o_ref.dtype)


def _dot_exact_rhs01_left(x, m01):
    a, b, c = _split3(x)
    return _dot(a, m01) + _dot(b, m01) + _dot(c, m01)


def _fox_decode(page_table, q, knt, vnt, lfnt, og, kt_pool, vt_pool, lf_pool, *, pages_per_step):
    nb, npages = page_table.shape
    pp = pages_per_step
    steps = npages // pp
    tri = jnp.triu(jnp.ones((PAGE, PAGE), BF16))
    pt = page_table.reshape(-1)

    per_b = lambda shape: pl.BlockSpec((None,) + shape, lambda b, j, pt_ref: (b,) + (0,) * len(shape))
    rows8 = lambda width: pl.BlockSpec((8, width), lambda b, j, pt_ref: (b, 0))
    pool = pl.BlockSpec(memory_space=pl.ANY)
    in_specs = [rows8(MIX_W), per_b((FOX_HEADS, HEAD_D, PAGE)), per_b((FOX_HEADS, HEAD_D, PAGE)),
                per_b((FOX_HEADS, PAGE)), rows8(MIX_W), pool, pool, pool,
                pl.BlockSpec((PAGE, PAGE), lambda b, j, pt_ref: (0, 0))]
    nrow = FOX_HEADS * 8
    return pl.pallas_call(
        functools.partial(_fox_decode_body, pages_per_step=pp),
        out_shape=jax.ShapeDtypeStruct((nb * 8, MIX_W), F32),
        grid_spec=pltpu.PrefetchScalarGridSpec(
            num_scalar_prefetch=1,
            grid=(nb, steps),
            in_specs=in_specs,
            out_specs=pl.BlockSpec((8, MIX_W), lambda b, j, pt_ref: (b, 0)),
            scratch_shapes=[pltpu.VMEM((2, pp, FOX_HEADS, HEAD_D, PAGE), F32),
                            pltpu.VMEM((2, pp, FOX_HEADS, HEAD_D, PAGE), F32),
                            pltpu.VMEM((2, pp, FOX_HEADS, PAGE), F32),
                            pltpu.SemaphoreType.DMA((2,)),
                            pltpu.VMEM((nrow, MIX_W), F32), pltpu.VMEM((nrow, 1), F32),
                            pltpu.VMEM((nrow, 1), F32), pltpu.VMEM((nrow, MIX_W), F32),
                            pltpu.VMEM((FOX_HEADS, 1), F32)]),
        compiler_params=_cparams(("arbitrary", "arbitrary")),
        name="fox_decode",
    )(pt, q, knt, vnt, lfnt, og, kt_pool, vt_pool, lf_pool, tri)


def _retention_body(q_ref, k_ref, v_ref, gate_ref, s0_ref, dmat_ref, qdec_ref, kdec_ref, gc_ref,
                    go_ref, mix_ref, state_ref, *, n_chunks, heads, unroll):
    go = go_ref[...]

    def chunk(n, states):
        rows = pl.ds(pl.multiple_of(n * RET_D, RET_D), RET_D)
        new_states = []
        for h in range(heads):
            cols = slice(h * RET_D, (h + 1) * RET_D)
            q = q_ref[rows, cols]
            k = k_ref[rows, cols]
            v = v_ref[rows, cols].astype(BF16)
            inner = _dot_nt(q.astype(BF16), k.astype(BF16)) * dmat_ref[h]
            o = (_dot(inner.astype(BF16), v)
                 + _dot((q * qdec_ref[h]).astype(BF16), states[h].astype(BF16)))
            new_states.append(gc_ref[h] * states[h] + _dot((k * kdec_ref[h]).T.astype(BF16), v))
            g = gate_ref[rows, cols]
            mix_ref[rows, cols] = (_rms_rows(o, go) * (g * _sigmoid(g))).astype(mix_ref.dtype)
        return tuple(new_states)

    states = lax.fori_loop(0, n_chunks, chunk, tuple(s0_ref[h] for h in range(heads)), unroll=unroll)
    for h in range(heads):
        state_ref[h] = states[h]


def _retention(q, k, v, gate, s0, tabs, go, *, batch, seq, out_dtype, heads):
    n = q.shape[0]
    n_chunks = seq // RET_D
    dmat, qdec, kdec, gc = tabs
    seq_blk = pl.BlockSpec((seq, heads * RET_D), lambda b, g: (b, g))
    tab = pl.BlockSpec((heads, RET_D, RET_D), lambda b, g: (g, 0, 0))
    st = pl.BlockSpec((None, heads, RET_D, RET_D), lambda b, g: (b, g, 0, 0))
    return pl.pallas_call(
        functools.partial(_retention_body, n_chunks=n_chunks, heads=heads,
                          unroll=2 if n_chunks % 2 == 0 else 1),
        out_shape=[jax.ShapeDtypeStruct((n, MIX_W), out_dtype),
                   jax.ShapeDtypeStruct((batch, RET_HEADS, RET_D, RET_D), F32)],
        grid=(batch, RET_HEADS // heads),
        in_specs=[seq_blk, seq_blk, seq_blk, seq_blk, st, tab, tab, tab, tab,
                  pl.BlockSpec((1, RET_D), lambda b, g: (0, 0))],
        out_specs=[seq_blk, st],
        compiler_params=_cparams(("arbitrary", "arbitrary")),
        name="retention",
    )(q, k, v, gate, s0, dmat, qdec, kdec, gc, go)


def _retention_tables(chunk):
    log_g = jnp.log1p(-jnp.exp2(-5.0 - jnp.arange(RET_HEADS, dtype=F32)))
    n = jnp.arange(RET_D, dtype=F32)
    diff = n[:, None] - n[None, :]
    valid = (n[:, None] < chunk) & (n[None, :] < chunk)
    dmat = jnp.where((diff >= 0) & valid,
                     jnp.exp(jnp.maximum(diff, 0.0)[None] * log_g[:, None, None]), 0.0)
    qdec = jnp.exp((n + 1.0)[None, :] * log_g[:, None])
    kdec = jnp.where(n[None, :] < chunk, jnp.exp((chunk - 1.0 - n)[None, :] * log_g[:, None]), 0.0)
    gc = jnp.exp(chunk * log_g)
    bc = lambda a: jnp.broadcast_to(a[:, :, None], (RET_HEADS, RET_D, RET_D))
    return (dmat, bc(qdec), bc(kdec),
            jnp.broadcast_to(gc[:, None, None], (RET_HEADS, RET_D, RET_D)))


def _mem_kv_body(mem_ref, g_ref, w_ref, bd_ref, gk_ref, k_ref, v_ref):
    h = _rms_rows(mem_ref[...], g_ref[...]).astype(BF16)
    k_ref[...] = _head_norm(_dot(h, w_ref[:, :MEM_W]), bd_ref[...], gk_ref[...])
    v_ref[...] = _dot(h, w_ref[:, MEM_W:])


def _mem_kv(mem, g, w, bd, gk, *, tile):
    depth = w.shape[0]
    n = mem.shape[0]
    per_layer = lambda shape: pl.BlockSpec((None,) + shape, lambda l, i: (l,) + (0,) * len(shape))
    out = pl.BlockSpec((None, tile, MEM_W), lambda l, i: (l, i, 0))
    return pl.pallas_call(
        _mem_kv_body,
        out_shape=[jax.ShapeDtypeStruct((depth, n, MEM_W), F32)] * 2,
        grid=(depth, n // tile),
        in_specs=[pl.BlockSpec((tile, D_MODEL), lambda l, i: (i, 0)),
                  per_layer((1, D_MODEL)), per_layer((D_MODEL, 2 * MEM_W)),
                  pl.BlockSpec((256, 256), lambda l, i: (0, 0)), per_layer((1, MEM_W))],
        out_specs=[out, out],
        compiler_params=_cparams(("arbitrary", "arbitrary")),
        name="mem_kv",
    )(mem, g, w, bd, gk)


def _mem_attend_body(q_ref, k_ref, v_ref, o_ref):
    t = q_ref.shape[0]
    lane = lax.broadcasted_iota(I32, (t, LANES), 1)
    lo = lane < HEAD_D
    for slab in range(MEM_W // LANES):
        sl = slice(slab * LANES, (slab + 1) * LANES)
        q = q_ref[:, sl]
        k = k_ref[:, sl].astype(BF16)
        v = v_ref[:, sl].astype(BF16)
        zq = jnp.zeros_like(q)
        outs = []
        for qh in (jnp.where(lo, q, zq), jnp.where(lo, zq, q)):
            s = _dot_nt(qh, k)
            p = jnp.exp(s - jnp.max(s, axis=-1, keepdims=True))
            p = p / jnp.sum(p, axis=-1, keepdims=True)
            outs.append(_dot(p.astype(BF16), v))
        o_ref[:, sl] = jnp.where(lo, outs[0], outs[1]).astype(o_ref.dtype)


def _mem_attend(qm, mk, mv, *, layer, batch, seq, tile):
    n = qm.shape[0]
    nt = seq // tile
    kv = pl.BlockSpec((None, N_MEM, MEM_W), lambda b, i: (layer, b, 0))
    return pl.pallas_call(
        _mem_attend_body,
        out_shape=jax.ShapeDtypeStruct((n, MEM_W), BF16),
        grid=(batch, nt),
        in_specs=[pl.BlockSpec((tile, MEM_W), lambda b, i: (b * nt + i, 0)), kv, kv],
        out_specs=pl.BlockSpec((tile, MEM_W), lambda b, i: (b * nt + i, 0)),
        compiler_params=_cparams(("arbitrary", "arbitrary")),
        name="mem_attend",
    )(qm, mk, mv)


def _mem_attend_t_body(q_ref, kt_ref, vt_ref, o_ref):
    t = q_ref.shape[0]
    lane = lax.broadcasted_iota(I32, (t, LANES), 1)
    lo = lane < HEAD_D
    for slab in range(MEM_W // LANES):
        sl = slice(slab * LANES, (slab + 1) * LANES)
        q = q_ref[:, sl]
        kt = kt_ref[2 * slab:2 * slab + 2].reshape(LANES, N_MEM)
        vt = vt_ref[2 * slab:2 * slab + 2].reshape(LANES, N_MEM)
        zq = jnp.zeros_like(q)
        outs = []
        for qh in (jnp.where(lo, q, zq), jnp.where(lo, zq, q)):
            s = _dot(qh, kt)
            p = jnp.exp(s - jnp.max(s, axis=-1, keepdims=True))
            p = p / jnp.sum(p, axis=-1, keepdims=True)
            outs.append(_dot_nt(p, vt))
        o_ref[:, sl] = jnp.where(lo, outs[0], outs[1])


def _mem_attend_t(qm, mkt, mvt, *, layer, batch, tokens):
    kv = pl.BlockSpec((None, None, MEM_W // HEAD_D, HEAD_D, N_MEM), lambda b: (layer, b, 0, 0, 0))
    return pl.pallas_call(
        _mem_attend_t_body,
        out_shape=jax.ShapeDtypeStruct((batch * tokens, MEM_W), F32),
        grid=(batch,),
        in_specs=[pl.BlockSpec((tokens, MEM_W), lambda b: (b, 0)), kv, kv],
        out_specs=pl.BlockSpec((tokens, MEM_W), lambda b: (b, 0)),
        compiler_params=_cparams(("arbitrary",)),
        name="mem_attend_t",
    )(qm, mkt, mvt)


def _outproj_router_body(x_ref, mix_ref, memo_ref, wa_ref, wb_ref, g_ref, wr_ref, br_ref, tri_ref,
                         xo_ref, h_ref, ri_ref, rw_ref, cnt_ref, carry_ref):
    sub = tri_ref.shape[0]

    @pl.when(pl.program_id(0) == 0)
    def _():
        carry_ref[...] = jnp.zeros_like(carry_ref)

    carry = carry_ref[...]
    for s0 in range(0, x_ref.shape[0], sub):
        rows = slice(s0, s0 + sub)
        x = (x_ref[rows, :] + _dot(mix_ref[rows, :].astype(BF16), wa_ref[...])
             + _dot(memo_ref[rows, :].astype(BF16), wb_ref[...]))
        xo_ref[rows, :] = x
        h = _rms_rows(x, g_ref[...])
        _to_token_major(h_ref.at[pl.ds(s0 * TOK_ROWS, sub * TOK_ROWS), :], h)
        carry = _route(h, wr_ref, br_ref[...], tri_ref[...], carry, ri_ref.at[rows, :], rw_ref.at[rows, :])
    carry_ref[...] = carry
    cnt_ref[...] = carry


def _route(h, wr_ref, br, tri, carry, ri_ref, rw_ref):
    t = h.shape[0]
    h_hi = h.astype(BF16)
    h_lo = (h - h_hi.astype(F32)).astype(BF16)
    d = _dot(h_hi, wr_ref[...])
    logits = d[:, :LANES] + d[:, LANES:] + _dot(h_lo, wr_ref[:, :LANES]) + br
    lane = lax.broadcasted_iota(I32, (t, LANES), 1)
    big = jnp.int32(1 << 20)
    rmax = lambda a: jnp.max(a, axis=-1, keepdims=True)
    rmin = lambda a: jnp.min(a, axis=-1, keepdims=True)
    rsum = lambda a: jnp.sum(a, axis=-1, keepdims=True)
    gmask = lane < N_GROUPS
    lg = jnp.where(gmask, logits, -jnp.inf)
    gmax = rmax(lg)
    gi = rmin(jnp.where(lg == gmax, lane, big))
    p_top = 1.0 / rsum(jnp.where(gmask, jnp.exp(logits - gmax), 0.0))
    first = ROUTE_LANE0 + EPG * gi
    emask = (lane >= first) & (lane < first + EPG)
    le = jnp.where(emask, logits, -jnp.inf)
    v1 = rmax(le)
    i1 = rmin(jnp.where(le == v1, lane, big))
    le2 = jnp.where(lane == i1, -jnp.inf, le)
    v2 = rmax(le2)
    i2 = rmin(jnp.where(le2 == v2, lane, big))
    e2 = jnp.exp(v2 - v1)
    w1 = p_top / (1.0 + e2)
    w2 = p_top * e2 / (1.0 + e2)

    onehot = ((lane == i1) | (lane == i2)).astype(F32)
    before = _dot(tri, onehot.astype(BF16)) + carry
    r1 = rsum(jnp.where(lane == i1, before, 0.0)).astype(I32)
    r2 = rsum(jnp.where(lane == i2, before, 0.0)).astype(I32)
    lane4 = lax.broadcasted_iota(I32, (t, 4), 1)
    ri_ref[...] = jnp.where(lane4 == 0, i1 - ROUTE_LANE0,
                            jnp.where(lane4 == 1, i2 - ROUTE_LANE0, jnp.where(lane4 == 2, r1, r2)))
    lane2 = lax.broadcasted_iota(I32, (t, 2), 1)
    rw_ref[...] = jnp.where(lane2 == 0, w1, w2)
    return carry + jnp.sum(onehot, axis=0, keepdims=True)


def _outproj_router(x, mix, memo, wa, wb, g, wr, br, *, tile):
    n = x.shape[0]
    sub = tile
    tri = jnp.tril(jnp.ones((sub, sub), BF16), -1)
    full = lambda shape: pl.BlockSpec(shape, lambda i: (0,) * len(shape))
    row = lambda width: pl.BlockSpec((tile, width), lambda i: (i, 0))
    return pl.pallas_call(
        _outproj_router_body,
        out_shape=[jax.ShapeDtypeStruct((n, D_MODEL), F32),
                   jax.ShapeDtypeStruct((n * TOK_ROWS, LANES), F32),
                   jax.ShapeDtypeStruct((n, 4), I32), jax.ShapeDtypeStruct((n, 2), F32),
                   jax.ShapeDtypeStruct((1, LANES), F32)],
        grid=(n // tile,),
        in_specs=[row(D_MODEL), row(MIX_W), row(MEM_W), full((MIX_W, D_MODEL)), full((MEM_W, D_MODEL)),
                  full((1, D_MODEL)), full((D_MODEL, 2 * LANES)), full((1, LANES)), full((sub, sub))],
        out_specs=[row(D_MODEL), pl.BlockSpec((tile * TOK_ROWS, LANES), lambda i: (i, 0)),
                   row(4), row(2), full((1, LANES))],
        scratch_shapes=[pltpu.VMEM((1, LANES), F32)],
        compiler_params=_cparams(("arbitrary",)),
        name="outproj_router",
    )(x, mix, memo, wa, wb, g, wr, br, tri)


TOK_ROWS = D_MODEL // LANES
DMA_UNROLL = 8


def _to_token_major(ref, x):
    t = x.shape[0]
    for s in range(TOK_ROWS):
        ref[pl.ds(s, t, stride=TOK_ROWS), :] = x[:, s * LANES:(s + 1) * LANES]


def _from_token_major(ref, t):
    return [ref[pl.ds(s, t, stride=TOK_ROWS), :] for s in range(TOK_ROWS)]


def _token_tile(ref, row):
    return ref.at[pl.ds(pl.multiple_of(row * TOK_ROWS, TOK_ROWS), TOK_ROWS), :]


def _moe_dispatch_body(eid_ref, rank_ref, off_ref, zt_ref, h_ref, xs_hbm, zero_ref, sem, *,
                       tile, expert_tile, n_expert_tiles):
    base = pl.program_id(0) * tile

    @pl.when(pl.program_id(0) == 0)
    def _():
        zero_ref[...] = jnp.zeros_like(zero_ref)
        rows = expert_tile * TOK_ROWS

        def fill(wait):
            def body(i, c):
                @pl.when(zt_ref[i] == 1)
                def _():
                    dst = xs_hbm.at[pl.ds(pl.multiple_of(i * rows, rows), rows), :]
                    cp = pltpu.make_async_copy(zero_ref, dst, sem)
                    cp.wait() if wait else cp.start()
                return c
            lax.fori_loop(0, n_expert_tiles, body, 0)

        fill(False)
        fill(True)

    def issue(t, c):
        for slot in range(2):
            a = 2 * (base + t) + slot
            dst = off_ref[eid_ref[a]] + rank_ref[a]
            pltpu.make_async_copy(_token_tile(h_ref, t), _token_tile(xs_hbm, dst), sem).start(priority=slot)
        return c

    lax.fori_loop(0, tile, issue, 0, unroll=DMA_UNROLL)

    group = pl.ds(0, DMA_UNROLL * TOK_ROWS)
    drain_group = pltpu.make_async_copy(h_ref.at[group, :], xs_hbm.at[group, :], sem)

    def drain(t, c):
        drain_group.wait()
        return c

    lax.fori_loop(0, 2 * tile // DMA_UNROLL, drain, 0)


def _moe_dispatch(eid, rank, off, zero_tile, h_tm, *, tile, expert_tile):
    n = h_tm.shape[0] // TOK_ROWS
    n_expert_tiles = zero_tile.shape[0]
    return pl.pallas_call(
        functools.partial(_moe_dispatch_body, tile=tile, expert_tile=expert_tile,
                          n_expert_tiles=n_expert_tiles),
        out_shape=jax.ShapeDtypeStruct((n_expert_tiles * expert_tile * TOK_ROWS, LANES), F32),
        grid_spec=pltpu.PrefetchScalarGridSpec(
            num_scalar_prefetch=4, grid=(n // tile,),
            in_specs=[pl.BlockSpec((tile * TOK_ROWS, LANES), lambda i, *_: (i, 0))],
            out_specs=pl.BlockSpec(memory_space=pl.ANY),
            scratch_shapes=[pltpu.VMEM((expert_tile * TOK_ROWS, LANES), F32),
                            pltpu.SemaphoreType.DMA(())]),
        compiler_params=_cparams(("arbitrary",)),
        name="moe_dispatch",
    )(eid, rank, off, zero_tile, h_tm)


def _moe_experts_body(te_ref, nu_ref, x_ref, w1_ref, w3_ref, w2_ref, y_ref, *, tile):
    del te_ref

    @pl.when(pl.program_id(0) < nu_ref[0])
    def _():
        x = jnp.concatenate(_from_token_major(x_ref, tile), axis=1).astype(BF16)
        a = _dot(x, w1_ref[...].astype(BF16))
        b = _dot(x, w3_ref[...].astype(BF16))
        hid = a * _sigmoid(a) * b
        _to_token_major(y_ref, _dot(hid.astype(BF16), w2_ref[...].astype(BF16)))

    @pl.when(pl.program_id(0) >= nu_ref[0])
    def _():
        y_ref[...] = jnp.zeros_like(y_ref)


def _moe_experts(tile_expert, n_used, xs, w1, w3, w2, *, tile, layer):
    rows = xs.shape[0] // TOK_ROWS
    wspec = lambda shape: pl.BlockSpec((None, None) + shape, lambda i, te, nu: (layer, te[i], 0, 0))
    blk = pl.BlockSpec((tile * TOK_ROWS, LANES), lambda i, te, nu: (i, 0))
    return pl.pallas_call(
        functools.partial(_moe_experts_body, tile=tile),
        out_shape=jax.ShapeDtypeStruct(xs.shape, F32),
        grid_spec=pltpu.PrefetchScalarGridSpec(
            num_scalar_prefetch=2, grid=(rows // tile,),
            in_specs=[blk, wspec((D_MODEL, D_EXPERT)), wspec((D_MODEL, D_EXPERT)),
                      wspec((D_EXPERT, D_MODEL))],
            out_specs=blk),
        compiler_params=_cparams(("arbitrary",)),
        name="moe_experts",
    )(tile_expert, n_used, xs, w1, w3, w2)


def _moe_combine_body(eid_ref, rank_ref, off_ref, x_ref, rw_ref, ys_hbm, o_ref, buf_ref, sem, *, tile):
    base = pl.program_id(0) * tile

    def issue(t, c):
        for slot in range(2):
            a = 2 * (base + t) + slot
            src = off_ref[eid_ref[a]] + rank_ref[a]
            pltpu.make_async_copy(_token_tile(ys_hbm, src), _token_tile(buf_ref.at[slot], t),
                                  sem).start(priority=slot)
        return c

    lax.fori_loop(0, tile, issue, 0, unroll=DMA_UNROLL)

    group = pl.ds(0, DMA_UNROLL * TOK_ROWS)
    drain_group = pltpu.make_async_copy(ys_hbm.at[group, :], buf_ref.at[0, group, :], sem)

    def drain(t, c):
        drain_group.wait()
        return c

    lax.fori_loop(0, 2 * tile // DMA_UNROLL, drain, 0)
    rw = rw_ref[...]
    y0 = _from_token_major(buf_ref.at[0], tile)
    y1 = _from_token_major(buf_ref.at[1], tile)
    for s in range(TOK_ROWS):
        sl = slice(s * LANES, (s + 1) * LANES)
        o_ref[:, sl] = x_ref[:, sl] + rw[:, 0:1] * y0[s] + rw[:, 1:2] * y1[s]


def _moe_combine(eid, rank, off, x, rw, ys, *, tile):
    n = x.shape[0]
    row = lambda width: pl.BlockSpec((tile, width), lambda i, *_: (i, 0))
    return pl.pallas_call(
        functools.partial(_moe_combine_body, tile=tile),
        out_shape=jax.ShapeDtypeStruct((n, D_MODEL), F32),
        grid_spec=pltpu.PrefetchScalarGridSpec(
            num_scalar_prefetch=3, grid=(n // tile,),
            in_specs=[row(D_MODEL), row(2), pl.BlockSpec(memory_space=pl.ANY)],
            out_specs=row(D_MODEL),
            scratch_shapes=[pltpu.VMEM((2, tile * TOK_ROWS, LANES), F32), pltpu.SemaphoreType.DMA(())]),
        compiler_params=_cparams(("arbitrary",)),
        name="moe_combine",
    )(eid, rank, off, x, rw, ys)


def _moe(x, h, ri, rw, cnt, w1, w3, w2, *, layer, expert_tile, token_tile):
    n = x.shape[0]
    n_tiles = (2 * n + N_EXPERTS * (expert_tile - 1)) // expert_tile + 1
    counts = cnt[0, ROUTE_LANE0:ROUTE_LANE0 + N_EXPERTS].astype(I32)
    padded = (counts + expert_tile - 1) // expert_tile * expert_tile
    ends = jnp.cumsum(padded)
    off = ends - padded
    n_used = (ends[-1] // expert_tile).reshape(1)
    tile_start = jnp.arange(n_tiles, dtype=I32) * expert_tile
    tile_expert = jnp.minimum(
        jnp.sum((tile_start[:, None] >= ends[None, :]).astype(I32), axis=1), N_EXPERTS - 1)
    last_of_expert = jnp.any((tile_start[:, None] + expert_tile == ends[None, :])
                             & (padded[None, :] > counts[None, :]), axis=1)
    zero_tile = (last_of_expert | (tile_start >= ends[-1])).astype(I32)
    eid = ri[:, 0:2].reshape(-1)
    rank = ri[:, 2:4].reshape(-1)
    xs = _moe_dispatch(eid, rank, off, zero_tile, h, tile=token_tile, expert_tile=expert_tile)
    ys = _moe_experts(tile_expert, n_used, xs, w1, w3, w2, tile=expert_tile, layer=layer)
    return _moe_combine(eid, rank, off, x, rw, ys, tile=token_tile)


def _block_diag_ones():
    r = jnp.arange(256) // HEAD_D
    return (r[:, None] == r[None, :]).astype(BF16)


def _tile_gain(g, reps):
    return jnp.tile(g.astype(F32), reps)[None, :]


def _rotary_tables(pos):
    half = RET_D // 2
    inv = 10000.0 ** (-jnp.arange(half, dtype=F32) / half)
    ang = pos.astype(F32)[:, None] * inv[None, :]
    cos, sin = jnp.cos(ang), jnp.sin(ang)
    return jnp.concatenate([cos, cos], axis=-1), jnp.concatenate([-sin, sin], axis=-1)


def kernel(x_prompt, x_sample, mem_prompt, cache_fox_k, cache_fox_v, cache_fox_logf, page_table,
           state_ret, cache_mem_k, cache_mem_v, w_in_fox, b_f_fox, g_q_fox, g_k_fox, w_in_ret, g_o_ret,
           g_attn, g_mem, w_mem_kv, g_mq, g_mk, w_out, g_ffn, w_rg, b_rg, w_re, b_re, w1, w3, w2):
    bp, sp, _ = x_prompt.shape
    bs, ts, _ = x_sample.shape
    depth = w_out.shape[0]
    n_pages = page_table.shape[1]
    past = n_pages * PAGE
    np_, ns = bp * sp, bs * ts
    tile_p, tile_s = 512, ns
    bd = _block_diag_ones()

    xp = x_prompt.reshape(np_, D_MODEL)
    xs = x_sample.reshape(ns, D_MODEL)
    mem = mem_prompt.reshape(bp * N_MEM, D_MODEL)
    mk_all, mv_all = _mem_kv(mem, g_mem[:, None, :], w_mem_kv.astype(BF16), bd,
                             jnp.tile(g_mk, (1, MEM_W // HEAD_D))[:, None, :], tile=512)

    mkt_cache = jnp.transpose(cache_mem_k, (0, 1, 3, 4, 2))
    mvt_cache = jnp.transpose(cache_mem_v, (0, 1, 3, 4, 2))

    outs = {}
    for l in range(depth):
        i = l // 2
        g_l = g_attn[l][None, :]
        gmq = _tile_gain(g_mq[l], 256 // HEAD_D)
        if l % 2 == 0:
            w = w_in_fox[i]
            w_pack = jnp.concatenate(
                [w[:, :4 * MIX_W], w[:, 4 * MIX_W + FOX_HEADS:], w[:, 4 * MIX_W:4 * MIX_W + FOX_HEADS],
                 jnp.zeros((D_MODEL, LANES - FOX_HEADS), F32)], axis=1).astype(BF16)
            bf = jnp.concatenate([b_f_fox[i], jnp.zeros((LANES - FOX_HEADS,), F32)])[None, :]
            gq = _tile_gain(g_q_fox[i], 256 // HEAD_D)
            gk = _tile_gain(g_k_fox[i], 256 // HEAD_D)
            fox = functools.partial(_fox_inproj, g=g_l, w=w_pack, bd=bd, gq=gq, gk=gk, gmq=gmq, bf=bf)
            qp, kp, kbp, vp, vbp, ogp, qmp, lfp, qxp, kxp = fox(
                xp, tile=tile_p, tiles_per_seq=sp // tile_p, act_dtype=BF16,
                q_scale=HEAD_D ** -0.5 * LOG2E, bias_scale=LOG2E)
            qs, ks, _, vs, _, ogs, qms, lfs, _, _ = fox(
                xs, tile=tile_s, tiles_per_seq=1, act_dtype=F32, q_scale=HEAD_D ** -0.5, bias_scale=1.0)
            mix_p = _fox_flash(qp, qxp, kbp, kxp, vbp, ogp, batch=bp, seq=sp, blk=256,
                               pairs=FOX_HEADS // 2)

            def pad_t(a):
                a = jnp.transpose(a.reshape(bs, ts, -1), (0, 2, 1))
                return jnp.pad(a, ((0, 0), (0, 0), (0, PAGE - ts)))

            knt = pad_t(ks).reshape(bs, FOX_HEADS, HEAD_D, PAGE)
            vnt = pad_t(vs).reshape(bs, FOX_HEADS, HEAD_D, PAGE)
            lfnt = pad_t(lfs)
            kt_pool = jnp.transpose(cache_fox_k[i], (0, 2, 3, 1))
            vt_pool = jnp.transpose(cache_fox_v[i], (0, 2, 3, 1))
            lf_pool = jnp.transpose(cache_fox_logf[i], (0, 2, 1))
            mix_s = _fox_decode(page_table, qs, knt, vnt, lfnt, ogs, kt_pool, vt_pool, lf_pool,
                                pages_per_step=16)
            outs.setdefault("fkp", []).append(kp.reshape(bp, sp, FOX_HEADS, HEAD_D))
            outs.setdefault("fvp", []).append(vp.reshape(bp, sp, FOX_HEADS, HEAD_D))
            outs.setdefault("flp", []).append(lfp.reshape(bp, sp, FOX_HEADS))
            outs.setdefault("fks", []).append(ks.reshape(bs, ts, FOX_HEADS, HEAD_D))
            outs.setdefault("fvs", []).append(vs.reshape(bs, ts, FOX_HEADS, HEAD_D))
            outs.setdefault("fls", []).append(lfs.reshape(bs, ts, FOX_HEADS))
        else:
            w = w_in_ret[i].astype(BF16)
            go = g_o_ret[i][None, :]
            cos_p, sin_p = _rotary_tables(jnp.arange(sp))
            cos_s, sin_s = _rotary_tables(past + jnp.arange(ts))
            ret = functools.partial(_ret_inproj, g=g_l, w=w, bd=bd, gmq=gmq)
            qp, kp, vp, gp, qmp = ret(xp, cos=cos_p, sin=sin_p, tile=tile_p,
                                      tiles_per_seq=sp // tile_p, act_dtype=BF16)
            qs, ks, vs, gs, qms = ret(xs, cos=jnp.tile(cos_s, (bs, 1)), sin=jnp.tile(sin_s, (bs, 1)),
                                      tile=tile_s, tiles_per_seq=1, act_dtype=F32)
            mix_p, sp_state = _retention(
                qp, kp, vp, gp, jnp.zeros((bp, RET_HEADS, RET_D, RET_D), F32),
                _retention_tables(RET_D), go, batch=bp, seq=sp, out_dtype=BF16, heads=3)
            pad_rows = lambda a: jnp.pad(a.reshape(bs, ts, MIX_W),
                                         ((0, 0), (0, RET_D - ts), (0, 0))).reshape(bs * RET_D, MIX_W)
            mix_s_pad, ss_state = _retention(
                pad_rows(qs), pad_rows(ks), pad_rows(vs), pad_rows(gs), state_ret[i],
                _retention_tables(ts), go, batch=bs, seq=RET_D, out_dtype=F32, heads=RET_HEADS)
            mix_s = mix_s_pad.reshape(bs, RET_D, MIX_W)[:, :ts].reshape(ns, MIX_W)
            outs.setdefault("rsp", []).append(sp_state)
            outs.setdefault("rss", []).append(ss_state)

        memo_p = _mem_attend(qmp, mk_all, mv_all, layer=l, batch=bp, seq=sp, tile=tile_p)
        memo_s = _mem_attend_t(qms, mkt_cache, mvt_cache, layer=l, batch=bs, tokens=ts)

        wo = w_out[l].astype(BF16)
        wr = jnp.concatenate([w_rg[l], w_re[l], jnp.zeros((D_MODEL, LANES - N_GROUPS - N_EXPERTS), F32)], axis=1)
        wr_hi = wr.astype(BF16)
        wr_lo = (wr - wr_hi.astype(F32)).astype(BF16)
        br = jnp.concatenate([b_rg[l], b_re[l], jnp.zeros((LANES - N_GROUPS - N_EXPERTS,), F32)])[None, :]
        route = functools.partial(_outproj_router, wa=wo[:MIX_W], wb=wo[MIX_W:], g=g_ffn[l][None, :],
                                  wr=jnp.concatenate([wr_hi, wr_lo], axis=1), br=br)
        xp, hp, rip, rwp, cntp = route(xp, mix_p, memo_p, tile=tile_p)
        xs, hs, ris, rws, cnts = route(xs, mix_s, memo_s, tile=tile_s)
        xp = _moe(xp, hp, rip, rwp, cntp, w1, w3, w2, layer=l, expert_tile=256, token_tile=tile_p)
        xs = _moe(xs, hs, ris, rws, cnts, w1, w3, w2, layer=l, expert_tile=128, token_tile=tile_s)

    mem_shape = (depth, bp, N_MEM, MEM_W // HEAD_D, HEAD_D)
    return (xp.reshape(bp, sp, D_MODEL), xs.reshape(bs, ts, D_MODEL),
            jnp.stack(outs["fkp"]), jnp.stack(outs["fvp"]), jnp.stack(outs["flp"]),
            jnp.stack(outs["fks"]), jnp.stack(outs["fvs"]), jnp.stack(outs["fls"]),
            jnp.stack(outs["rsp"]), jnp.stack(outs["rss"]),
            mk_all.reshape(mem_shape), mv_all.reshape(mem_shape))
```

```python
import functools

import jax
import jax.numpy as jnp
from jax import lax
from jax.experimental import pallas as pl
from jax.experimental.pallas import tpu as pltpu

F32 = jnp.float32
BF16 = jnp.bfloat16
I32 = jnp.int32

D_MODEL = 1024
MIX_W = 768
MEM_W = 256
HEAD_D = 64
FOX_HEADS = 12
RET_HEADS = 6
RET_D = 128
N_MEM = 256
N_EXPERTS = 32
N_GROUPS = 4
EPG = 8
D_EXPERT = 256
PAGE = 128
EPS = 1e-6
NEG = -1e30
LANES = 128
VMEM_LIMIT = 56 * 1024 * 1024
ROUTE_LANE0 = N_GROUPS
LOG2E = 1.4426950408889634
SUBLANES = 8


def _cparams(sem, vmem=VMEM_LIMIT):
    return pltpu.CompilerParams(dimension_semantics=sem, vmem_limit_bytes=vmem)


def _dot(a, b):
    return jnp.dot(a, b, preferred_element_type=F32)


def _dot_nt(a, b):
    return lax.dot_general(a, b, (((1,), (1,)), ((), ())), preferred_element_type=F32)


def _split3(x):
    a = x.astype(BF16)
    r = x - a.astype(F32)
    b = r.astype(BF16)
    c = (r - b.astype(F32)).astype(BF16)
    return a, b, c


def _dot_exact_rhs01(m01, x):
    a, b, c = _split3(x)
    return _dot(m01, a) + _dot(m01, b) + _dot(m01, c)


def _rms_rows(x, gain):
    return x * lax.rsqrt(jnp.mean(x * x, axis=-1, keepdims=True) + EPS) * gain


def _head_norm(y, bd, gain):
    ss = _dot((y * y).astype(BF16), bd)
    return y * lax.rsqrt(ss * (1.0 / HEAD_D) + EPS) * gain


def _sigmoid(x):
    return 1.0 / (1.0 + jnp.exp(-x))


def _fox_inproj_body(x_ref, g_ref, w_ref, bd_ref, gq_ref, gk_ref, gmq_ref, bf_ref, tri_ref,
                     sel_ref, ones_ref,
                     q_ref, k_ref, kb_ref, v_ref, vb_ref, og_ref, qm_ref, lf_ref, qx_ref, kx_ref,
                     carry_ref, *, tiles_per_seq, q_scale, bias_scale, kv_transposed):
    t = x_ref.shape[0]
    h = _rms_rows(x_ref[...], g_ref[...]).astype(BF16)
    bd = bd_ref[...]

    def put(ref, sl, val):
        if kv_transposed:
            ref[sl, :] = val.T
        else:
            ref[:, sl] = val

    for j in range(3):
        sl = slice(j * 256, (j + 1) * 256)
        y = _dot(h, w_ref[:, sl])
        q_ref[:, sl] = (_head_norm(y, bd, gq_ref[...]) * q_scale).astype(q_ref.dtype)
    for j in range(3):
        sl = slice(j * 256, (j + 1) * 256)
        y = _dot(h, w_ref[:, MIX_W + j * 256:MIX_W + (j + 1) * 256])
        kn = _head_norm(y, bd, gk_ref[...])
        put(k_ref, sl, kn)
        kb_ref[:, sl] = kn.astype(BF16)
    for j in range(3):
        sl = slice(j * 256, (j + 1) * 256)
        y = _dot(h, w_ref[:, 2 * MIX_W + j * 256:2 * MIX_W + (j + 1) * 256])
        put(v_ref, sl, y)
        vb_ref[:, sl] = y.astype(BF16)
    for j in range(3):
        sl = slice(j * 256, (j + 1) * 256)
        og_ref[:, sl] = _dot(h, w_ref[:, 3 * MIX_W + j * 256:3 * MIX_W + (j + 1) * 256])
    y = _dot(h, w_ref[:, 4 * MIX_W:4 * MIX_W + MEM_W])
    qm_ref[...] = (_head_norm(y, bd, gmq_ref[...]) * 0.125).astype(qm_ref.dtype)
    f = _dot(h, w_ref[:, 4 * MIX_W + MEM_W:]) + bf_ref[...]
    lf = jnp.minimum(f, 0.0) - jnp.log1p(jnp.exp(-jnp.abs(f)))
    lane = lax.broadcasted_iota(I32, (t, LANES), 1)
    lf = jnp.where(lane < FOX_HEADS, lf, 0.0)
    lf_ref[...] = lf[:, :FOX_HEADS]

    @pl.when(pl.program_id(0) % tiles_per_seq == 0)
    def _():
        carry_ref[...] = jnp.zeros_like(carry_ref)

    c = _dot_exact_rhs01(tri_ref[...], lf) + carry_ref[...]
    carry_ref[...] = c[t - 1:t, :]
    pieces = jnp.concatenate(_split3(c * bias_scale), axis=1)
    x = _dot(pieces, sel_ref[...]) + ones_ref[...]
    qx_ref[...] = x[:, :MIX_W].astype(BF16)
    kx_ref[...] = x[:, MIX_W:].astype(BF16)


def _bias_columns():
    piece = jnp.arange(3 * LANES) // LANES
    head = jnp.arange(3 * LANES) % LANES
    lane = jnp.arange(2 * MIX_W)
    half = lane // MIX_W
    in_pair = lane % LANES
    lane_head = 2 * ((lane % MIX_W) // LANES) + in_pair // 6
    slot = in_pair % 6
    live = in_pair < 12
    takes_c = live & jnp.where(half == 0, slot < 3, slot >= 3)
    sel = ((head[:, None] == lane_head[None, :]) & takes_c[None, :]
           & (piece[:, None] == (slot % 3)[None, :]))
    sel = jnp.where(sel, jnp.where(half == 0, 1.0, -1.0)[None, :], 0.0).astype(BF16)
    ones = (live & jnp.logical_not(takes_c)).astype(F32)[None, :]
    return sel, ones


def _fox_inproj(x, g, w, bd, gq, gk, gmq, bf, *, tile, tiles_per_seq, act_dtype, q_scale, bias_scale,
                kv_transposed):
    n = x.shape[0]
    tri = jnp.tril(jnp.ones((tile, tile), BF16))
    sel, ones = _bias_columns()
    full = lambda shape: pl.BlockSpec(shape, lambda i: (0,) * len(shape))
    row = lambda width: pl.BlockSpec((tile, width), lambda i: (i, 0))
    if kv_transposed:
        kv_shape = jax.ShapeDtypeStruct((n // (tile * tiles_per_seq), MIX_W, tile * tiles_per_seq), F32)
        kv_spec = pl.BlockSpec((None, MIX_W, tile), lambda i: (i // tiles_per_seq, 0, i % tiles_per_seq))
    else:
        kv_shape = jax.ShapeDtypeStruct((n, MIX_W), F32)
        kv_spec = row(MIX_W)
    outs = [
        jax.ShapeDtypeStruct((n, MIX_W), act_dtype),
        kv_shape,
        jax.ShapeDtypeStruct((n, MIX_W), BF16),
        kv_shape,
        jax.ShapeDtypeStruct((n, MIX_W), BF16),
        jax.ShapeDtypeStruct((n, MIX_W), F32),
        jax.ShapeDtypeStruct((n, MEM_W), act_dtype),
        jax.ShapeDtypeStruct((n, FOX_HEADS), F32),
        jax.ShapeDtypeStruct((n, MIX_W), BF16),
        jax.ShapeDtypeStruct((n, MIX_W), BF16),
    ]
    return pl.pallas_call(
        functools.partial(_fox_inproj_body, tiles_per_seq=tiles_per_seq, q_scale=q_scale,
                          bias_scale=bias_scale, kv_transposed=kv_transposed),
        out_shape=outs,
        grid=(n // tile,),
        in_specs=[row(D_MODEL), full((1, D_MODEL)), full(w.shape), full((256, 256)),
                  full((1, 256)), full((1, 256)), full((1, 256)), full((1, LANES)),
                  full((tile, tile)), full(sel.shape), full(ones.shape)],
        out_specs=[row(MIX_W), kv_spec, row(MIX_W), kv_spec, row(MIX_W), row(MIX_W),
                   row(MEM_W), row(FOX_HEADS), row(MIX_W), row(MIX_W)],
        scratch_shapes=[pltpu.VMEM((1, LANES), F32)],
        compiler_params=_cparams(("arbitrary",)),
        name="fox_inproj",
    )(x, g, w, bd, gq, gk, gmq, bf, tri, sel, ones)


def _ret_inproj_body(x_ref, g_ref, w_ref, bd_ref, gmq_ref, cos_ref, sin_ref,
                     q_ref, k_ref, v_ref, gate_ref, qm_ref):
    h = _rms_rows(x_ref[...], g_ref[...]).astype(BF16)
    cos = cos_ref[...]
    sin = sin_ref[...]
    rot = lambda y: y * cos + pltpu.roll(y, RET_D // 2, 1) * sin
    for j in range(3):
        y = _dot(h, w_ref[:, j * 256:(j + 1) * 256])
        for half in range(2):
            sl = slice(j * 256 + half * RET_D, j * 256 + (half + 1) * RET_D)
            q_ref[:, sl] = rot(y[:, half * RET_D:(half + 1) * RET_D])
    for j in range(3):
        y = _dot(h, w_ref[:, MIX_W + j * 256:MIX_W + (j + 1) * 256])
        for half in range(2):
            sl = slice(j * 256 + half * RET_D, j * 256 + (half + 1) * RET_D)
            k_ref[:, sl] = rot(y[:, half * RET_D:(half + 1) * RET_D]) * (RET_D ** -0.5)
    for j in range(3):
        sl = slice(j * 256, (j + 1) * 256)
        v_ref[:, sl] = _dot(h, w_ref[:, 2 * MIX_W + j * 256:2 * MIX_W + (j + 1) * 256])
    for j in range(3):
        sl = slice(j * 256, (j + 1) * 256)
        gate_ref[:, sl] = _dot(h, w_ref[:, 3 * MIX_W + j * 256:3 * MIX_W + (j + 1) * 256])
    y = _dot(h, w_ref[:, 4 * MIX_W:])
    qm_ref[...] = (_head_norm(y, bd_ref[...], gmq_ref[...]) * 0.125).astype(qm_ref.dtype)


def _ret_inproj(x, g, w, bd, gmq, cos, sin, *, tile, tiles_per_seq, act_dtype):
    n = x.shape[0]
    full = lambda shape: pl.BlockSpec(shape, lambda i: (0,) * len(shape))
    row = lambda width: pl.BlockSpec((tile, width), lambda i: (i, 0))
    tab = pl.BlockSpec((tile, RET_D), lambda i: (i % tiles_per_seq, 0))
    outs = [jax.ShapeDtypeStruct((n, MIX_W), F32)] * 4 + [jax.ShapeDtypeStruct((n, MEM_W), act_dtype)]
    return pl.pallas_call(
        _ret_inproj_body,
        out_shape=outs,
        grid=(n // tile,),
        in_specs=[row(D_MODEL), full((1, D_MODEL)), full(w.shape), full((256, 256)),
                  full((1, 256)), tab, tab],
        out_specs=[row(MIX_W)] * 4 + [row(MEM_W)],
        compiler_params=_cparams(("arbitrary",)),
        name="ret_inproj",
    )(x, g, w, bd, gmq, cos, sin)


def _fox_flash_body(q_ref, qx_ref, k_ref, kx_ref, v_ref, og_ref, o_ref, *, blk, pairs):
    qi = pl.program_id(2)
    heads = 2 * pairs
    lane = lax.broadcasted_iota(I32, (blk, LANES), 1)
    lo = lane < HEAD_D
    qs = []
    for h in range(heads):
        sl = slice((h // 2) * LANES, (h // 2 + 1) * LANES)
        q, qx = q_ref[:, sl], qx_ref[:, sl]
        keep = lo if h % 2 == 0 else jnp.logical_not(lo)
        keep_x = (lane >= 6 * (h % 2)) & (lane < 6 * (h % 2) + 6)
        qs.append(jnp.concatenate([jnp.where(keep, q, jnp.zeros_like(q)),
                                   jnp.where(keep_x, qx, jnp.zeros_like(qx))], axis=1))
    rows = lax.broadcasted_iota(I32, (blk, blk), 0)
    cols = lax.broadcasted_iota(I32, (blk, blk), 1)
    causal = rows >= cols

    def step(start, carry, diag):
        ms, ls, accs = carry
        new_m, new_l, new_acc = [], [], []
        for p in range(pairs):
            sl = slice(p * LANES, (p + 1) * LANES)
            k = jnp.concatenate([k_ref[pl.ds(start, blk), sl], kx_ref[pl.ds(start, blk), sl]], axis=1)
            v = v_ref[pl.ds(start, blk), sl]
            alphas, pvs = [], []
            for j in range(2):
                h = 2 * p + j
                s = _dot_nt(qs[h], k)
                if diag:
                    s = jnp.where(causal, s, NEG)
                m_new = jnp.maximum(ms[h], jnp.max(s, axis=-1, keepdims=True))
                alpha = jnp.exp2(ms[h] - m_new)
                e = jnp.exp2(s - m_new)
                new_l.append(alpha * ls[h] + jnp.sum(e, axis=-1, keepdims=True))
                new_m.append(m_new)
                alphas.append(alpha)
                pvs.append(_dot(e.astype(BF16), v))
            new_acc.append(accs[p] * jnp.where(lo, alphas[0], alphas[1]) + jnp.where(lo, pvs[0], pvs[1]))
        return tuple(new_m), tuple(new_l), tuple(new_acc)

    init = ((jnp.full((blk, 1), NEG, F32),) * heads, (jnp.zeros((blk, 1), F32),) * heads,
            (jnp.zeros((blk, LANES), F32),) * pairs)
    carry = lax.fori_loop(
        0, qi, lambda i, c: step(pl.multiple_of(i * blk, blk), c, False), init)
    _, ls, accs = step(pl.multiple_of(qi * blk, blk), carry, True)
    for p in range(pairs):
        sl = slice(p * LANES, (p + 1) * LANES)
        o = accs[p] / jnp.where(lo, ls[2 * p], ls[2 * p + 1])
        o_ref[:, sl] = (o * _sigmoid(og_ref[:, sl])).astype(o_ref.dtype)


def _fox_flash(q, qx, kb, kx, vb, og, *, batch, seq, blk, pairs):
    n = q.shape[0]
    nq = seq // blk
    w = pairs * LANES
    qblk = pl.BlockSpec((blk, w), lambda b, g, i: (b * nq + i, g))
    kvblk = pl.BlockSpec((seq, w), lambda b, g, i: (b, g))
    return pl.pallas_call(
        functools.partial(_fox_flash_body, blk=blk, pairs=pairs),
        out_shape=jax.ShapeDtypeStruct((n, MIX_W), BF16),
        grid=(batch, FOX_HEADS // (2 * pairs), nq),
        in_specs=[qblk, qblk, kvblk, kvblk, kvblk, qblk],
        out_specs=qblk,
        compiler_params=_cparams(("arbitrary",) * 3),
        name="fox_flash",
    )(q, qx, kb, kx, vb, og)


def _fox_decode_body(pt_ref, q_ref, kn_ref, vn_ref, lfn_ref, og_ref, kt_hbm, vt_hbm, lf_hbm, tri_ref,
                     o_ref, kbuf, vbuf, lfbuf, sems, qbd_ref, m_ref, l_ref, acc_ref, carry_ref, *,
                     pages_per_step):
    pp = pages_per_step
    j = pl.program_id(1)
    steps = pl.num_programs(1)
    step = pl.program_id(0) * steps + j
    slot = step % 2

    def page_copies(s, into):
        copies = []
        for i in range(pp):
            page = pt_ref[s * pp + i]
            copies += [pltpu.make_async_copy(kt_hbm.at[page], kbuf.at[into, i], sems.at[into]),
                       pltpu.make_async_copy(vt_hbm.at[page], vbuf.at[into, i], sems.at[into]),
                       pltpu.make_async_copy(lf_hbm.at[page], lfbuf.at[into, i], sems.at[into])]
        return copies

    def start_all(copies):
        for n, cp in enumerate(copies):
            cp.start(priority=1 if n % 3 == 1 else 0)

    @pl.when(step == 0)
    def _():
        start_all(page_copies(step, slot))

    @pl.when(step + 1 < pl.num_programs(0) * steps)
    def _():
        start_all(page_copies(step + 1, 1 - slot))

    for cp in page_copies(step, slot):
        cp.wait()

    nrow = FOX_HEADS * 8
    rowh = lax.broadcasted_iota(I32, (nrow, MIX_W), 0) // 8
    colh = lax.broadcasted_iota(I32, (nrow, MIX_W), 1) // HEAD_D
    diag = rowh == colh

    @pl.when(j == 0)
    def _():
        q = q_ref[...]
        qbd_ref[...] = jnp.where(diag, jnp.concatenate([q] * FOX_HEADS, axis=0), 0.0)
        m_ref[...] = jnp.full_like(m_ref, NEG)
        l_ref[...] = jnp.zeros_like(l_ref)
        acc_ref[...] = jnp.zeros_like(acc_ref)
        carry_ref[...] = jnp.zeros_like(carry_ref)

    qbd = qbd_ref[...]
    tri = tri_ref[...]

    def scores(kt, lf, carry):
        c = _dot_exact_rhs01_left(lf, tri) + carry
        bias = jnp.concatenate(
            [jnp.broadcast_to(c[h:h + 1, :], (8, PAGE)) for h in range(FOX_HEADS)], axis=0)
        return _dot(qbd, kt) - bias, c[:, PAGE - 1:PAGE]

    def update(ss, vts):
        m_old = m_ref[...]
        m_new = m_old
        for s in ss:
            m_new = jnp.maximum(m_new, jnp.max(s, axis=-1, keepdims=True))
        alpha = jnp.exp(m_old - m_new)
        l_new = alpha * l_ref[...]
        pv = None
        for s, vt in zip(ss, vts):
            p = jnp.exp(s - m_new)
            l_new = l_new + jnp.sum(p, axis=-1, keepdims=True)
            d = _dot_nt(p, vt)
            pv = d if pv is None else pv + d
        m_ref[...] = m_new
        l_ref[...] = l_new
        acc_ref[...] = alpha * acc_ref[...] + pv

    carry = carry_ref[...]
    ss, vts = [], []
    for i in range(pp):
        s, carry = scores(kbuf[slot, i].reshape(MIX_W, PAGE), lfbuf[slot, i], carry)
        ss.append(s)
        vts.append(vbuf[slot, i].reshape(MIX_W, PAGE))
    carry_ref[...] = carry
    update(ss, vts)

    @pl.when(j == pl.num_programs(1) - 1)
    def _():
        s, _ = scores(kn_ref[...].reshape(MIX_W, PAGE), lfn_ref[...], carry_ref[...])
        qpos = lax.broadcasted_iota(I32, (nrow, PAGE), 0) % 8
        kpos = lax.broadcasted_iota(I32, (nrow, PAGE), 1)
        s = jnp.where(qpos >= kpos, s, NEG)
        update([s], [vn_ref[...].reshape(MIX_W, PAGE)])
        o = jnp.where(diag, acc_ref[...] / l_ref[...], 0.0)
        out = o[0:8, :]
        for h in range(1, FOX_HEADS):
            out = out + o[h * 8:(h + 1) * 8, :]
        o_ref[...] = (out * _sigmoid(og_ref[...])).astype(o_ref.dtype)


def _dot_exact_rhs01_left(x, m01):
    a, b, c = _split3(x)
    return _dot(a, m01) + _dot(b, m01) + _dot(c, m01)


def _fox_decode(page_table, q, knt, vnt, lfnt, og, kt_pool, vt_pool, lf_pool, *, pages_per_step):
    nb, npages = page_table.shape
    pp = pages_per_step
    steps = npages // pp
    tri = jnp.triu(jnp.ones((PAGE, PAGE), BF16))
    pt = page_table.reshape(-1)

    per_b = lambda shape: pl.BlockSpec((None,) + shape, lambda b, j, pt_ref: (b,) + (0,) * len(shape))
    rows8 = lambda width: pl.BlockSpec((8, width), lambda b, j, pt_ref: (b, 0))
    pool = pl.BlockSpec(memory_space=pl.ANY)
    in_specs = [rows8(MIX_W), per_b((FOX_HEADS, HEAD_D, PAGE)), per_b((FOX_HEADS, HEAD_D, PAGE)),
                per_b((FOX_HEADS, PAGE)), rows8(MIX_W), pool, pool, pool,
                pl.BlockSpec((PAGE, PAGE), lambda b, j, pt_ref: (0, 0))]
    nrow = FOX_HEADS * 8
    return pl.pallas_call(
        functools.partial(_fox_decode_body, pages_per_step=pp),
        out_shape=jax.ShapeDtypeStruct((nb * 8, MIX_W), F32),
        grid_spec=pltpu.PrefetchScalarGridSpec(
            num_scalar_prefetch=1,
            grid=(nb, steps),
            in_specs=in_specs,
            out_specs=pl.BlockSpec((8, MIX_W), lambda b, j, pt_ref: (b, 0)),
            scratch_shapes=[pltpu.VMEM((2, pp, FOX_HEADS, HEAD_D, PAGE), F32),
                            pltpu.VMEM((2, pp, FOX_HEADS, HEAD_D, PAGE), F32),
                            pltpu.VMEM((2, pp, FOX_HEADS, PAGE), F32),
                            pltpu.SemaphoreType.DMA((2,)),
                            pltpu.VMEM((nrow, MIX_W), F32), pltpu.VMEM((nrow, 1), F32),
                            pltpu.VMEM((nrow, 1), F32), pltpu.VMEM((nrow, MIX_W), F32),
                            pltpu.VMEM((FOX_HEADS, 1), F32)]),
        compiler_params=_cparams(("arbitrary", "arbitrary")),
        name="fox_decode",
    )(pt, q, knt, vnt, lfnt, og, kt_pool, vt_pool, lf_pool, tri)


def _retention_body(q_ref, k_ref, v_ref, gate_ref, s0_ref, dmat_ref, qdec_ref, kdec_ref, gc_ref,
                    go_ref, mix_ref, state_ref, *, n_chunks, heads, unroll):
    go = go_ref[...]

    def chunk(n, states):
        rows = pl.ds(pl.multiple_of(n * RET_D, RET_D), RET_D)
        new_states = []
        for h in range(heads):
            cols = slice(h * RET_D, (h + 1) * RET_D)
            q = q_ref[rows, cols]
            k = k_ref[rows, cols]
            v = v_ref[rows, cols].astype(BF16)
            inner = _dot_nt(q.astype(BF16), k.astype(BF16)) * dmat_ref[h]
            o = (_dot(inner.astype(BF16), v)
                 + _dot((q * qdec_ref[h]).astype(BF16), states[h].astype(BF16)))
            new_states.append(gc_ref[h] * states[h] + _dot((k * kdec_ref[h]).T.astype(BF16), v))
            g = gate_ref[rows, cols]
            mix_ref[rows, cols] = (_rms_rows(o, go) * (g * _sigmoid(g))).astype(mix_ref.dtype)
        return tuple(new_states)

    states = lax.fori_loop(0, n_chunks, chunk, tuple(s0_ref[h] for h in range(heads)), unroll=unroll)
    for h in range(heads):
        state_ref[h] = states[h]


def _retention(q, k, v, gate, s0, tabs, go, *, batch, seq, out_dtype, heads):
    n = q.shape[0]
    n_chunks = seq // RET_D
    dmat, qdec, kdec, gc = tabs
    seq_blk = pl.BlockSpec((seq, heads * RET_D), lambda b, g: (b, g))
    tab = pl.BlockSpec((heads, RET_D, RET_D), lambda b, g: (g, 0, 0))
    st = pl.BlockSpec((None, heads, RET_D, RET_D), lambda b, g: (b, g, 0, 0))
    return pl.pallas_call(
        functools.partial(_retention_body, n_chunks=n_chunks, heads=heads,
                          unroll=4 if n_chunks % 4 == 0 else 1),
        out_shape=[jax.ShapeDtypeStruct((n, MIX_W), out_dtype),
                   jax.ShapeDtypeStruct((batch, RET_HEADS, RET_D, RET_D), F32)],
        grid=(batch, RET_HEADS // heads),
        in_specs=[seq_blk, seq_blk, seq_blk, seq_blk, st, tab, tab, tab, tab,
                  pl.BlockSpec((1, RET_D), lambda b, g: (0, 0))],
        out_specs=[seq_blk, st],
        compiler_params=_cparams(("arbitrary", "arbitrary")),
        name="retention",
    )(q, k, v, gate, s0, dmat, qdec, kdec, gc, go)


def _retention_tables(chunk):
    log_g = jnp.log1p(-jnp.exp2(-5.0 - jnp.arange(RET_HEADS, dtype=F32)))
    n = jnp.arange(RET_D, dtype=F32)
    diff = n[:, None] - n[None, :]
    valid = (n[:, None] < chunk) & (n[None, :] < chunk)
    dmat = jnp.where((diff >= 0) & valid,
                     jnp.exp(jnp.maximum(diff, 0.0)[None] * log_g[:, None, None]), 0.0)
    qdec = jnp.exp((n + 1.0)[None, :] * log_g[:, None])
    kdec = jnp.where(n[None, :] < chunk, jnp.exp((chunk - 1.0 - n)[None, :] * log_g[:, None]), 0.0)
    gc = jnp.exp(chunk * log_g)
    bc = lambda a: jnp.broadcast_to(a[:, :, None], (RET_HEADS, RET_D, RET_D))
    return (dmat, bc(qdec), bc(kdec),
            jnp.broadcast_to(gc[:, None, None], (RET_HEADS, RET_D, RET_D)))


def _mem_kv_body(mem_ref, g_ref, w_ref, bd_ref, gk_ref, k_ref, v_ref):
    h = _rms_rows(mem_ref[...], g_ref[...]).astype(BF16)
    k_ref[...] = _head_norm(_dot(h, w_ref[:, :MEM_W]), bd_ref[...], gk_ref[...])
    v_ref[...] = _dot(h, w_ref[:, MEM_W:])


def _mem_kv(mem, g, w, bd, gk, *, tile):
    depth = w.shape[0]
    n = mem.shape[0]
    per_layer = lambda shape: pl.BlockSpec((None,) + shape, lambda l, i: (l,) + (0,) * len(shape))
    out = pl.BlockSpec((None, tile, MEM_W), lambda l, i: (l, i, 0))
    return pl.pallas_call(
        _mem_kv_body,
        out_shape=[jax.ShapeDtypeStruct((depth, n, MEM_W), F32)] * 2,
        grid=(depth, n // tile),
        in_specs=[pl.BlockSpec((tile, D_MODEL), lambda l, i: (i, 0)),
                  per_layer((1, D_MODEL)), per_layer((D_MODEL, 2 * MEM_W)),
                  pl.BlockSpec((256, 256), lambda l, i: (0, 0)), per_layer((1, MEM_W))],
        out_specs=[out, out],
        compiler_params=_cparams(("arbitrary", "arbitrary")),
        name="mem_kv",
    )(mem, g, w, bd, gk)


def _mem_attend_body(q_ref, k_ref, v_ref, o_ref):
    t = q_ref.shape[0]
    lane = lax.broadcasted_iota(I32, (t, LANES), 1)
    lo = lane < HEAD_D
    for slab in range(MEM_W // LANES):
        sl = slice(slab * LANES, (slab + 1) * LANES)
        q = q_ref[:, sl]
        k = k_ref[:, sl].astype(BF16)
        v = v_ref[:, sl].astype(BF16)
        zq = jnp.zeros_like(q)
        outs = []
        for qh in (jnp.where(lo, q, zq), jnp.where(lo, zq, q)):
            s = _dot_nt(qh, k)
            p = jnp.exp(s - jnp.max(s, axis=-1, keepdims=True))
            p = p / jnp.sum(p, axis=-1, keepdims=True)
            outs.append(_dot(p.astype(BF16), v))
        o_ref[:, sl] = jnp.where(lo, outs[0], outs[1]).astype(o_ref.dtype)


def _mem_attend(qm, mk, mv, *, layer, batch, seq, tile):
    n = qm.shape[0]
    nt = seq // tile
    kv = pl.BlockSpec((None, N_MEM, MEM_W), lambda b, i: (layer, b, 0))
    return pl.pallas_call(
        _mem_attend_body,
        out_shape=jax.ShapeDtypeStruct((n, MEM_W), BF16),
        grid=(batch, nt),
        in_specs=[pl.BlockSpec((tile, MEM_W), lambda b, i: (b * nt + i, 0)), kv, kv],
        out_specs=pl.BlockSpec((tile, MEM_W), lambda b, i: (b * nt + i, 0)),
        compiler_params=_cparams(("arbitrary", "arbitrary")),
        name="mem_attend",
    )(qm, mk, mv)


def _mem_attend_t_body(q_ref, kt_ref, vt_ref, o_ref):
    t = q_ref.shape[0]
    lane = lax.broadcasted_iota(I32, (t, LANES), 1)
    lo = lane < HEAD_D
    for slab in range(MEM_W // LANES):
        sl = slice(slab * LANES, (slab + 1) * LANES)
        q = q_ref[:, sl]
        kt = kt_ref[2 * slab:2 * slab + 2].reshape(LANES, N_MEM)
        vt = vt_ref[2 * slab:2 * slab + 2].reshape(LANES, N_MEM)
        zq = jnp.zeros_like(q)
        outs = []
        for qh in (jnp.where(lo, q, zq), jnp.where(lo, zq, q)):
            s = _dot(qh, kt)
            p = jnp.exp(s - jnp.max(s, axis=-1, keepdims=True))
            p = p / jnp.sum(p, axis=-1, keepdims=True)
            outs.append(_dot_nt(p, vt))
        o_ref[:, sl] = jnp.where(lo, outs[0], outs[1])


def _mem_attend_t(qm, mkt, mvt, *, layer, batch, tokens):
    kv = pl.BlockSpec((None, None, MEM_W // HEAD_D, HEAD_D, N_MEM), lambda b: (layer, b, 0, 0, 0))
    return pl.pallas_call(
        _mem_attend_t_body,
        out_shape=jax.ShapeDtypeStruct((batch * tokens, MEM_W), F32),
        grid=(batch,),
        in_specs=[pl.BlockSpec((tokens, MEM_W), lambda b: (b, 0)), kv, kv],
        out_specs=pl.BlockSpec((tokens, MEM_W), lambda b: (b, 0)),
        compiler_params=_cparams(("arbitrary",)),
        name="mem_attend_t",
    )(qm, mkt, mvt)


def _outproj_router_body(x_ref, mix_ref, memo_ref, wa_ref, wb_ref, g_ref, wr_ref, br_ref, tri_ref,
                         xo_ref, h_ref, ri_ref, rw_ref, cnt_ref, carry_ref):
    sub = tri_ref.shape[0]

    @pl.when(pl.program_id(0) == 0)
    def _():
        carry_ref[...] = jnp.zeros_like(carry_ref)

    carry = carry_ref[...]
    for s0 in range(0, x_ref.shape[0], sub):
        rows = slice(s0, s0 + sub)
        x = (x_ref[rows, :] + _dot(mix_ref[rows, :].astype(BF16), wa_ref[...])
             + _dot(memo_ref[rows, :].astype(BF16), wb_ref[...]))
        xo_ref[rows, :] = x
        h = _rms_rows(x, g_ref[...])
        _to_token_major(h_ref.at[pl.ds(s0 * TOK_ROWS, sub * TOK_ROWS), :], h)
        carry = _route(h, wr_ref, br_ref[...], tri_ref[...], carry, ri_ref.at[rows, :], rw_ref.at[rows, :])
    carry_ref[...] = carry
    cnt_ref[...] = carry


def _route(h, wr_ref, br, tri, carry, ri_ref, rw_ref):
    t = h.shape[0]
    h_hi = h.astype(BF16)
    h_lo = (h - h_hi.astype(F32)).astype(BF16)
    d = _dot(h_hi, wr_ref[...])
    logits = d[:, :LANES] + d[:, LANES:] + _dot(h_lo, wr_ref[:, :LANES]) + br
    lane = lax.broadcasted_iota(I32, (t, LANES), 1)
    big = jnp.int32(1 << 20)
    rmax = lambda a: jnp.max(a, axis=-1, keepdims=True)
    rmin = lambda a: jnp.min(a, axis=-1, keepdims=True)
    rsum = lambda a: jnp.sum(a, axis=-1, keepdims=True)
    gmask = lane < N_GROUPS
    lg = jnp.where(gmask, logits, -jnp.inf)
    gmax = rmax(lg)
    gi = rmin(jnp.where(lg == gmax, lane, big))
    p_top = 1.0 / rsum(jnp.where(gmask, jnp.exp(logits - gmax), 0.0))
    first = ROUTE_LANE0 + EPG * gi
    emask = (lane >= first) & (lane < first + EPG)
    le = jnp.where(emask, logits, -jnp.inf)
    v1 = rmax(le)
    i1 = rmin(jnp.where(le == v1, lane, big))
    le2 = jnp.where(lane == i1, -jnp.inf, le)
    v2 = rmax(le2)
    i2 = rmin(jnp.where(le2 == v2, lane, big))
    e2 = jnp.exp(v2 - v1)
    w1 = p_top / (1.0 + e2)
    w2 = p_top * e2 / (1.0 + e2)

    onehot = ((lane == i1) | (lane == i2)).astype(F32)
    before = _dot(tri, onehot.astype(BF16)) + carry
    r1 = rsum(jnp.where(lane == i1, before, 0.0)).astype(I32)
    r2 = rsum(jnp.where(lane == i2, before, 0.0)).astype(I32)
    lane4 = lax.broadcasted_iota(I32, (t, 4), 1)
    ri_ref[...] = jnp.where(lane4 == 0, i1 - ROUTE_LANE0,
                            jnp.where(lane4 == 1, i2 - ROUTE_LANE0, jnp.where(lane4 == 2, r1, r2)))
    lane2 = lax.broadcasted_iota(I32, (t, 2), 1)
    rw_ref[...] = jnp.where(lane2 == 0, w1, w2)
    return carry + jnp.sum(onehot, axis=0, keepdims=True)


def _outproj_router(x, mix, memo, wa, wb, g, wr, br, *, tile):
    n = x.shape[0]
    sub = tile
    tri = jnp.tril(jnp.ones((sub, sub), BF16), -1)
    full = lambda shape: pl.BlockSpec(shape, lambda i: (0,) * len(shape))
    row = lambda width: pl.BlockSpec((tile, width), lambda i: (i, 0))
    return pl.pallas_call(
        _outproj_router_body,
        out_shape=[jax.ShapeDtypeStruct((n, D_MODEL), F32),
                   jax.ShapeDtypeStruct((n * TOK_ROWS, LANES), F32),
                   jax.ShapeDtypeStruct((n, 4), I32), jax.ShapeDtypeStruct((n, 2), F32),
                   jax.ShapeDtypeStruct((1, LANES), F32)],
        grid=(n // tile,),
        in_specs=[row(D_MODEL), row(MIX_W), row(MEM_W), full((MIX_W, D_MODEL)), full((MEM_W, D_MODEL)),
                  full((1, D_MODEL)), full((D_MODEL, 2 * LANES)), full((1, LANES)), full((sub, sub))],
        out_specs=[row(D_MODEL), pl.BlockSpec((tile * TOK_ROWS, LANES), lambda i: (i, 0)),
                   row(4), row(2), full((1, LANES))],
        scratch_shapes=[pltpu.VMEM((1, LANES), F32)],
        compiler_params=_cparams(("arbitrary",)),
        name="outproj_router",
    )(x, mix, memo, wa, wb, g, wr, br, tri)


TOK_ROWS = D_MODEL // LANES
DMA_UNROLL = 16


def _to_token_major(ref, x):
    t = x.shape[0]
    for s in range(TOK_ROWS):
        ref[pl.ds(s, t, stride=TOK_ROWS), :] = x[:, s * LANES:(s + 1) * LANES]


def _from_token_major(ref, t):
    return [ref[pl.ds(s, t, stride=TOK_ROWS), :] for s in range(TOK_ROWS)]


def _token_tile(ref, row):
    return ref.at[pl.ds(pl.multiple_of(row * TOK_ROWS, TOK_ROWS), TOK_ROWS), :]


def _moe_dispatch_body(eid_ref, rank_ref, off_ref, zt_ref, h_ref, xs_hbm, zero_ref, sem, *,
                       tile, expert_tile, n_expert_tiles):
    base = pl.program_id(0) * tile

    @pl.when(pl.program_id(0) == 0)
    def _():
        zero_ref[...] = jnp.zeros_like(zero_ref)
        rows = expert_tile * TOK_ROWS

        def fill(wait):
            def body(i, c):
                @pl.when(zt_ref[i] == 1)
                def _():
                    dst = xs_hbm.at[pl.ds(pl.multiple_of(i * rows, rows), rows), :]
                    cp = pltpu.make_async_copy(zero_ref, dst, sem)
                    cp.wait() if wait else cp.start()
                return c
            lax.fori_loop(0, n_expert_tiles, body, 0)

        fill(False)
        fill(True)

    def issue(t, c):
        for slot in range(2):
            a = 2 * (base + t) + slot
            dst = off_ref[eid_ref[a]] + rank_ref[a]
            pltpu.make_async_copy(_token_tile(h_ref, t), _token_tile(xs_hbm, dst), sem).start(priority=slot)
        return c

    lax.fori_loop(0, tile, issue, 0, unroll=DMA_UNROLL)

    group = pl.ds(0, DMA_UNROLL * TOK_ROWS)
    drain_group = pltpu.make_async_copy(h_ref.at[group, :], xs_hbm.at[group, :], sem)

    def drain(t, c):
        drain_group.wait()
        return c

    lax.fori_loop(0, 2 * tile // DMA_UNROLL, drain, 0)


def _moe_dispatch(eid, rank, off, zero_tile, h_tm, *, tile, expert_tile):
    n = h_tm.shape[0] // TOK_ROWS
    n_expert_tiles = zero_tile.shape[0]
    return pl.pallas_call(
        functools.partial(_moe_dispatch_body, tile=tile, expert_tile=expert_tile,
                          n_expert_tiles=n_expert_tiles),
        out_shape=jax.ShapeDtypeStruct((n_expert_tiles * expert_tile * TOK_ROWS, LANES), F32),
        grid_spec=pltpu.PrefetchScalarGridSpec(
            num_scalar_prefetch=4, grid=(n // tile,),
            in_specs=[pl.BlockSpec((tile * TOK_ROWS, LANES), lambda i, *_: (i, 0))],
            out_specs=pl.BlockSpec(memory_space=pl.ANY),
            scratch_shapes=[pltpu.VMEM((expert_tile * TOK_ROWS, LANES), F32),
                            pltpu.SemaphoreType.DMA(())]),
        compiler_params=_cparams(("arbitrary",)),
        name="moe_dispatch",
    )(eid, rank, off, zero_tile, h_tm)


def _moe_experts_body(te_ref, nu_ref, x_ref, w1_ref, w3_ref, w2_ref, y_ref, *, tile):
    del te_ref

    @pl.when(pl.program_id(0) < nu_ref[0])
    def _():
        x = jnp.concatenate(_from_token_major(x_ref, tile), axis=1).astype(BF16)
        a = _dot(x, w1_ref[...].astype(BF16))
        b = _dot(x, w3_ref[...].astype(BF16))
        hid = a * _sigmoid(a) * b
        _to_token_major(y_ref, _dot(hid.astype(BF16), w2_ref[...].astype(BF16)))

    @pl.when(pl.program_id(0) >= nu_ref[0])
    def _():
        y_ref[...] = jnp.zeros_like(y_ref)


def _moe_experts(tile_expert, n_used, xs, w1, w3, w2, *, tile, layer):
    rows = xs.shape[0] // TOK_ROWS
    wspec = lambda shape: pl.BlockSpec((None, None) + shape, lambda i, te, nu: (layer, te[i], 0, 0))
    blk = pl.BlockSpec((tile * TOK_ROWS, LANES), lambda i, te, nu: (i, 0))
    return pl.pallas_call(
        functools.partial(_moe_experts_body, tile=tile),
        out_shape=jax.ShapeDtypeStruct(xs.shape, F32),
        grid_spec=pltpu.PrefetchScalarGridSpec(
            num_scalar_prefetch=2, grid=(rows // tile,),
            in_specs=[blk, wspec((D_MODEL, D_EXPERT)), wspec((D_MODEL, D_EXPERT)),
                      wspec((D_EXPERT, D_MODEL))],
            out_specs=blk),
        compiler_params=_cparams(("arbitrary",)),
        name="moe_experts",
    )(tile_expert, n_used, xs, w1, w3, w2)


def _moe_combine_body(eid_ref, rank_ref, off_ref, x_ref, rw_ref, ys_hbm, o_ref, buf_ref, sem, *, tile):
    base = pl.program_id(0) * tile

    def issue(t, c):
        for slot in range(2):
            a = 2 * (base + t) + slot
            src = off_ref[eid_ref[a]] + rank_ref[a]
            pltpu.make_async_copy(_token_tile(ys_hbm, src), _token_tile(buf_ref.at[slot], t),
                                  sem).start(priority=slot)
        return c

    lax.fori_loop(0, tile, issue, 0, unroll=DMA_UNROLL)

    group = pl.ds(0, DMA_UNROLL * TOK_ROWS)
    drain_group = pltpu.make_async_copy(ys_hbm.at[group, :], buf_ref.at[0, group, :], sem)

    def drain(t, c):
        drain_group.wait()
        return c

    lax.fori_loop(0, 2 * tile // DMA_UNROLL, drain, 0)
    rw = rw_ref[...]
    y0 = _from_token_major(buf_ref.at[0], tile)
    y1 = _from_token_major(buf_ref.at[1], tile)
    for s in range(TOK_ROWS):
        sl = slice(s * LANES, (s + 1) * LANES)
        o_ref[:, sl] = x_ref[:, sl] + rw[:, 0:1] * y0[s] + rw[:, 1:2] * y1[s]


def _moe_combine(eid, rank, off, x, rw, ys, *, tile):
    n = x.shape[0]
    row = lambda width: pl.BlockSpec((tile, width), lambda i, *_: (i, 0))
    return pl.pallas_call(
        functools.partial(_moe_combine_body, tile=tile),
        out_shape=jax.ShapeDtypeStruct((n, D_MODEL), F32),
        grid_spec=pltpu.PrefetchScalarGridSpec(
            num_scalar_prefetch=3, grid=(n // tile,),
            in_specs=[row(D_MODEL), row(2), pl.BlockSpec(memory_space=pl.ANY)],
            out_specs=row(D_MODEL),
            scratch_shapes=[pltpu.VMEM((2, tile * TOK_ROWS, LANES), F32), pltpu.SemaphoreType.DMA(())]),
        compiler_params=_cparams(("arbitrary",)),
        name="moe_combine",
    )(eid, rank, off, x, rw, ys)


def _moe(x, h, ri, rw, cnt, w1, w3, w2, *, layer, expert_tile, token_tile):
    n = x.shape[0]
    n_tiles = (2 * n + N_EXPERTS * (expert_tile - 1)) // expert_tile + 1
    counts = cnt[0, ROUTE_LANE0:ROUTE_LANE0 + N_EXPERTS].astype(I32)
    padded = (counts + expert_tile - 1) // expert_tile * expert_tile
    ends = jnp.cumsum(padded)
    off = ends - padded
    n_used = (ends[-1] // expert_tile).reshape(1)
    tile_start = jnp.arange(n_tiles, dtype=I32) * expert_tile
    tile_expert = jnp.minimum(
        jnp.sum((tile_start[:, None] >= ends[None, :]).astype(I32), axis=1), N_EXPERTS - 1)
    last_of_expert = jnp.any((tile_start[:, None] + expert_tile == ends[None, :])
                             & (padded[None, :] > counts[None, :]), axis=1)
    zero_tile = (last_of_expert | (tile_start >= ends[-1])).astype(I32)
    eid = ri[:, 0:2].reshape(-1)
    rank = ri[:, 2:4].reshape(-1)
    xs = _moe_dispatch(eid, rank, off, zero_tile, h, tile=token_tile, expert_tile=expert_tile)
    ys = _moe_experts(tile_expert, n_used, xs, w1, w3, w2, tile=expert_tile, layer=layer)
    return _moe_combine(eid, rank, off, x, rw, ys, tile=token_tile)


def _block_diag_ones():
    r = jnp.arange(256) // HEAD_D
    return (r[:, None] == r[None, :]).astype(BF16)


def _tile_gain(g, reps):
    return jnp.tile(g.astype(F32), reps)[None, :]


def _rotary_tables(pos):
    half = RET_D // 2
    inv = 10000.0 ** (-jnp.arange(half, dtype=F32) / half)
    ang = pos.astype(F32)[:, None] * inv[None, :]
    cos, sin = jnp.cos(ang), jnp.sin(ang)
    return jnp.concatenate([cos, cos], axis=-1), jnp.concatenate([-sin, sin], axis=-1)


def kernel(x_prompt, x_sample, mem_prompt, cache_fox_k, cache_fox_v, cache_fox_logf, page_table,
           state_ret, cache_mem_k, cache_mem_v, w_in_fox, b_f_fox, g_q_fox, g_k_fox, w_in_ret, g_o_ret,
           g_attn, g_mem, w_mem_kv, g_mq, g_mk, w_out, g_ffn, w_rg, b_rg, w_re, b_re, w1, w3, w2):
    bp, sp, _ = x_prompt.shape
    bs, ts, _ = x_sample.shape
    depth = w_out.shape[0]
    n_pages = page_table.shape[1]
    past = n_pages * PAGE
    np_, ns = bp * sp, bs * ts
    tile_p, tile_s = 512, ns
    bd = _block_diag_ones()

    xp = x_prompt.reshape(np_, D_MODEL)
    xs = x_sample.reshape(ns, D_MODEL)
    mem = mem_prompt.reshape(bp * N_MEM, D_MODEL)
    mk_all, mv_all = _mem_kv(mem, g_mem[:, None, :], w_mem_kv.astype(BF16), bd,
                             jnp.tile(g_mk, (1, MEM_W // HEAD_D))[:, None, :], tile=512)

    mkt_cache = jnp.transpose(cache_mem_k, (0, 1, 3, 4, 2))
    mvt_cache = jnp.transpose(cache_mem_v, (0, 1, 3, 4, 2))

    outs = {}
    for l in range(depth):
        i = l // 2
        g_l = g_attn[l][None, :]
        gmq = _tile_gain(g_mq[l], 256 // HEAD_D)
        if l % 2 == 0:
            w = w_in_fox[i]
            w_pack = jnp.concatenate(
                [w[:, :4 * MIX_W], w[:, 4 * MIX_W + FOX_HEADS:], w[:, 4 * MIX_W:4 * MIX_W + FOX_HEADS],
                 jnp.zeros((D_MODEL, LANES - FOX_HEADS), F32)], axis=1).astype(BF16)
            bf = jnp.concatenate([b_f_fox[i], jnp.zeros((LANES - FOX_HEADS,), F32)])[None, :]
            gq = _tile_gain(g_q_fox[i], 256 // HEAD_D)
            gk = _tile_gain(g_k_fox[i], 256 // HEAD_D)
            fox = functools.partial(_fox_inproj, g=g_l, w=w_pack, bd=bd, gq=gq, gk=gk, gmq=gmq, bf=bf)
            qp, kp, kbp, vp, vbp, ogp, qmp, lfp, qxp, kxp = fox(
                xp, tile=tile_p, tiles_per_seq=sp // tile_p, act_dtype=BF16,
                q_scale=HEAD_D ** -0.5 * LOG2E, bias_scale=LOG2E, kv_transposed=True)
            qs, ks, _, vs, _, ogs, qms, lfs, _, _ = fox(
                xs, tile=tile_s, tiles_per_seq=1, act_dtype=F32, q_scale=HEAD_D ** -0.5, bias_scale=1.0,
                kv_transposed=False)
            mix_p = _fox_flash(qp, qxp, kbp, kxp, vbp, ogp, batch=bp, seq=sp, blk=256,
                               pairs=FOX_HEADS // 2)

            def pad_t(a):
                a = jnp.transpose(a.reshape(bs, ts, -1), (0, 2, 1))
                return jnp.pad(a, ((0, 0), (0, 0), (0, PAGE - ts)))

            knt = pad_t(ks).reshape(bs, FOX_HEADS, HEAD_D, PAGE)
            vnt = pad_t(vs).reshape(bs, FOX_HEADS, HEAD_D, PAGE)
            lfnt = pad_t(lfs)
            kt_pool = jnp.transpose(cache_fox_k[i], (0, 2, 3, 1))
            vt_pool = jnp.transpose(cache_fox_v[i], (0, 2, 3, 1))
            lf_pool = jnp.transpose(cache_fox_logf[i], (0, 2, 1))
            mix_s = _fox_decode(page_table, qs, knt, vnt, lfnt, ogs, kt_pool, vt_pool, lf_pool,
                                pages_per_step=16)
            to_rows = lambda a: jnp.transpose(a.reshape(bp, FOX_HEADS, HEAD_D, sp), (0, 3, 1, 2))
            outs.setdefault("fkp", []).append(to_rows(kp))
            outs.setdefault("fvp", []).append(to_rows(vp))
            outs.setdefault("flp", []).append(lfp.reshape(bp, sp, FOX_HEADS))
            outs.setdefault("fks", []).append(ks.reshape(bs, ts, FOX_HEADS, HEAD_D))
            outs.setdefault("fvs", []).append(vs.reshape(bs, ts, FOX_HEADS, HEAD_D))
            outs.setdefault("fls", []).append(lfs.reshape(bs, ts, FOX_HEADS))
        else:
            w = w_in_ret[i].astype(BF16)
            go = g_o_ret[i][None, :]
            cos_p, sin_p = _rotary_tables(jnp.arange(sp))
            cos_s, sin_s = _rotary_tables(past + jnp.arange(ts))
            ret = functools.partial(_ret_inproj, g=g_l, w=w, bd=bd, gmq=gmq)
            qp, kp, vp, gp, qmp = ret(xp, cos=cos_p, sin=sin_p, tile=tile_p,
                                      tiles_per_seq=sp // tile_p, act_dtype=BF16)
            qs, ks, vs, gs, qms = ret(xs, cos=jnp.tile(cos_s, (bs, 1)), sin=jnp.tile(sin_s, (bs, 1)),
                                      tile=tile_s, tiles_per_seq=1, act_dtype=F32)
            mix_p, sp_state = _retention(
                qp, kp, vp, gp, jnp.zeros((bp, RET_HEADS, RET_D, RET_D), F32),
                _retention_tables(RET_D), go, batch=bp, seq=sp, out_dtype=BF16, heads=3)
            pad_rows = lambda a: jnp.pad(a.reshape(bs, ts, MIX_W),
                                         ((0, 0), (0, RET_D - ts), (0, 0))).reshape(bs * RET_D, MIX_W)
            mix_s_pad, ss_state = _retention(
                pad_rows(qs), pad_rows(ks), pad_rows(vs), pad_rows(gs), state_ret[i],
                _retention_tables(ts), go, batch=bs, seq=RET_D, out_dtype=F32, heads=RET_HEADS)
            mix_s = mix_s_pad.reshape(bs, RET_D, MIX_W)[:, :ts].reshape(ns, MIX_W)
            outs.setdefault("rsp", []).append(sp_state)
            outs.setdefault("rss", []).append(ss_state)

        memo_p = _mem_attend(qmp, mk_all, mv_all, layer=l, batch=bp, seq=sp, tile=tile_p)
        memo_s = _mem_attend_t(qms, mkt_cache, mvt_cache, layer=l, batch=bs, tokens=ts)

        wo = w_out[l].astype(BF16)
        wr = jnp.concatenate([w_rg[l], w_re[l], jnp.zeros((D_MODEL, LANES - N_GROUPS - N_EXPERTS), F32)], axis=1)
        wr_hi = wr.astype(BF16)
        wr_lo = (wr - wr_hi.astype(F32)).astype(BF16)
        br = jnp.concatenate([b_rg[l], b_re[l], jnp.zeros((LANES - N_GROUPS - N_EXPERTS,), F32)])[None, :]
        route = functools.partial(_outproj_router, wa=wo[:MIX_W], wb=wo[MIX_W:], g=g_ffn[l][None, :],
                                  wr=jnp.concatenate([wr_hi, wr_lo], axis=1), br=br)
        xp, hp, rip, rwp, cntp = route(xp, mix_p, memo_p, tile=tile_p)
        xs, hs, ris, rws, cnts = route(xs, mix_s, memo_s, tile=tile_s)
        xp = _moe(xp, hp, rip, rwp, cntp, w1, w3, w2, layer=l, expert_tile=256, token_tile=tile_p)
        xs = _moe(xs, hs, ris, rws, cnts, w1, w3, w2, layer=l, expert_tile=128, token_tile=tile_s)

    mem_shape = (depth, bp, N_MEM, MEM_W // HEAD_D, HEAD_D)
    return (xp.reshape(bp, sp, D_MODEL), xs.reshape(bs, ts, D_MODEL),
            jnp.stack(outs["fkp"]), jnp.stack(outs["fvp"]), jnp.stack(outs["flp"]),
            jnp.stack(outs["fks"]), jnp.stack(outs["fvs"]), jnp.stack(outs["fls"]),
            jnp.stack(outs["rsp"]), jnp.stack(outs["rss"]),
            mk_all.reshape(mem_shape), mv_all.reshape(mem_shape))
```

```python
import functools

import jax
import jax.numpy as jnp
from jax import lax
from jax.experimental import pallas as pl
from jax.experimental.pallas import tpu as pltpu

F32 = jnp.float32
BF16 = jnp.bfloat16
I32 = jnp.int32

D_MODEL = 1024
MIX_W = 768
MEM_W = 256
HEAD_D = 64
FOX_HEADS = 12
RET_HEADS = 6
RET_D = 128
N_MEM = 256
N_EXPERTS = 32
N_GROUPS = 4
EPG = 8
D_EXPERT = 256
PAGE = 128
EPS = 1e-6
NEG = -1e30
LANES = 128
VMEM_LIMIT = 56 * 1024 * 1024
ROUTE_LANE0 = N_GROUPS
LOG2E = 1.4426950408889634
SUBLANES = 8


def _cparams(sem, vmem=VMEM_LIMIT):
    return pltpu.CompilerParams(dimension_semantics=sem, vmem_limit_bytes=vmem)


def _dot(a, b):
    return jnp.dot(a, b, preferred_element_type=F32)


def _dot_nt(a, b):
    return lax.dot_general(a, b, (((1,), (1,)), ((), ())), preferred_element_type=F32)


def _split3(x):
    a = x.astype(BF16)
    r = x - a.astype(F32)
    b = r.astype(BF16)
    c = (r - b.astype(F32)).astype(BF16)
    return a, b, c


def _dot_exact_rhs01(m01, x):
    a, b, c = _split3(x)
    return _dot(m01, a) + _dot(m01, b) + _dot(m01, c)


def _rms_rows(x, gain):
    return x * lax.rsqrt(jnp.mean(x * x, axis=-1, keepdims=True) + EPS) * gain


def _head_norm(y, bd, gain):
    ss = _dot((y * y).astype(BF16), bd)
    return y * lax.rsqrt(ss * (1.0 / HEAD_D) + EPS) * gain


def _sigmoid(x):
    return 1.0 / (1.0 + jnp.exp(-x))


def _fox_inproj_body(x_ref, g_ref, w_ref, bd_ref, gq_ref, gk_ref, gmq_ref, bf_ref, tri_ref,
                     sel_ref, ones_ref,
                     q_ref, k_ref, kb_ref, v_ref, vb_ref, og_ref, qm_ref, lf_ref, qx_ref, kx_ref,
                     carry_ref, *, tiles_per_seq, q_scale, bias_scale, kv_transposed):
    t = x_ref.shape[0]
    h = _rms_rows(x_ref[...], g_ref[...]).astype(BF16)
    bd = bd_ref[...]

    def put(ref, sl, val):
        if kv_transposed:
            ref[sl, :] = val.T
        else:
            ref[:, sl] = val

    for j in range(3):
        sl = slice(j * 256, (j + 1) * 256)
        y = _dot(h, w_ref[:, sl])
        q_ref[:, sl] = (_head_norm(y, bd, gq_ref[...]) * q_scale).astype(q_ref.dtype)
    for j in range(3):
        sl = slice(j * 256, (j + 1) * 256)
        y = _dot(h, w_ref[:, MIX_W + j * 256:MIX_W + (j + 1) * 256])
        kn = _head_norm(y, bd, gk_ref[...])
        put(k_ref, sl, kn)
        kb_ref[:, sl] = kn.astype(BF16)
    for j in range(3):
        sl = slice(j * 256, (j + 1) * 256)
        y = _dot(h, w_ref[:, 2 * MIX_W + j * 256:2 * MIX_W + (j + 1) * 256])
        put(v_ref, sl, y)
        vb_ref[:, sl] = y.astype(BF16)
    for j in range(3):
        sl = slice(j * 256, (j + 1) * 256)
        og_ref[:, sl] = _dot(h, w_ref[:, 3 * MIX_W + j * 256:3 * MIX_W + (j + 1) * 256])
    y = _dot(h, w_ref[:, 4 * MIX_W:4 * MIX_W + MEM_W])
    qm_ref[...] = (_head_norm(y, bd, gmq_ref[...]) * 0.125).astype(qm_ref.dtype)
    f = _dot(h, w_ref[:, 4 * MIX_W + MEM_W:]) + bf_ref[...]
    lf = jnp.minimum(f, 0.0) - jnp.log1p(jnp.exp(-jnp.abs(f)))
    lane = lax.broadcasted_iota(I32, (t, LANES), 1)
    lf = jnp.where(lane < FOX_HEADS, lf, 0.0)
    lf_ref[...] = lf[:, :FOX_HEADS]

    @pl.when(pl.program_id(0) % tiles_per_seq == 0)
    def _():
        carry_ref[...] = jnp.zeros_like(carry_ref)

    c = _dot_exact_rhs01(tri_ref[...], lf) + carry_ref[...]
    carry_ref[...] = c[t - 1:t, :]
    pieces = jnp.concatenate(_split3(c * bias_scale), axis=1)
    x = _dot(pieces, sel_ref[...]) + ones_ref[...]
    qx_ref[...] = x[:, :MIX_W].astype(BF16)
    kx_ref[...] = x[:, MIX_W:].astype(BF16)


def _bias_columns():
    piece = jnp.arange(3 * LANES) // LANES
    head = jnp.arange(3 * LANES) % LANES
    lane = jnp.arange(2 * MIX_W)
    half = lane // MIX_W
    in_pair = lane % LANES
    lane_head = 2 * ((lane % MIX_W) // LANES) + in_pair // 6
    slot = in_pair % 6
    live = in_pair < 12
    takes_c = live & jnp.where(half == 0, slot < 3, slot >= 3)
    sel = ((head[:, None] == lane_head[None, :]) & takes_c[None, :]
           & (piece[:, None] == (slot % 3)[None, :]))
    sel = jnp.where(sel, jnp.where(half == 0, 1.0, -1.0)[None, :], 0.0).astype(BF16)
    ones = (live & jnp.logical_not(takes_c)).astype(F32)[None, :]
    return sel, ones


def _fox_inproj(x, g, w, bd, gq, gk, gmq, bf, *, tile, tiles_per_seq, act_dtype, q_scale, bias_scale,
                kv_transposed):
    n = x.shape[0]
    tri = jnp.tril(jnp.ones((tile, tile), BF16))
    sel, ones = _bias_columns()
    full = lambda shape: pl.BlockSpec(shape, lambda i: (0,) * len(shape))
    row = lambda width: pl.BlockSpec((tile, width), lambda i: (i, 0))
    if kv_transposed:
        kv_shape = jax.ShapeDtypeStruct((n // (tile * tiles_per_seq), MIX_W, tile * tiles_per_seq), F32)
        kv_spec = pl.BlockSpec((None, MIX_W, tile), lambda i: (i // tiles_per_seq, 0, i % tiles_per_seq))
    else:
        kv_shape = jax.ShapeDtypeStruct((n, MIX_W), F32)
        kv_spec = row(MIX_W)
    outs = [
        jax.ShapeDtypeStruct((n, MIX_W), act_dtype),
        kv_shape,
        jax.ShapeDtypeStruct((n, MIX_W), BF16),
        kv_shape,
        jax.ShapeDtypeStruct((n, MIX_W), BF16),
        jax.ShapeDtypeStruct((n, MIX_W), F32),
        jax.ShapeDtypeStruct((n, MEM_W), act_dtype),
        jax.ShapeDtypeStruct((n, FOX_HEADS), F32),
        jax.ShapeDtypeStruct((n, MIX_W), BF16),
        jax.ShapeDtypeStruct((n, MIX_W), BF16),
    ]
    return pl.pallas_call(
        functools.partial(_fox_inproj_body, tiles_per_seq=tiles_per_seq, q_scale=q_scale,
                          bias_scale=bias_scale, kv_transposed=kv_transposed),
        out_shape=outs,
        grid=(n // tile,),
        in_specs=[row(D_MODEL), full((1, D_MODEL)), full(w.shape), full((256, 256)),
                  full((1, 256)), full((1, 256)), full((1, 256)), full((1, LANES)),
                  full((tile, tile)), full(sel.shape), full(ones.shape)],
        out_specs=[row(MIX_W), kv_spec, row(MIX_W), kv_spec, row(MIX_W), row(MIX_W),
                   row(MEM_W), row(FOX_HEADS), row(MIX_W), row(MIX_W)],
        scratch_shapes=[pltpu.VMEM((1, LANES), F32)],
        compiler_params=_cparams(("arbitrary",)),
        name="fox_inproj",
    )(x, g, w, bd, gq, gk, gmq, bf, tri, sel, ones)


def _ret_inproj_body(x_ref, g_ref, w_ref, bd_ref, gmq_ref, cos_ref, sin_ref,
                     q_ref, k_ref, v_ref, gate_ref, qm_ref):
    h = _rms_rows(x_ref[...], g_ref[...]).astype(BF16)
    cos = cos_ref[...]
    sin = sin_ref[...]
    rot = lambda y: y * cos + pltpu.roll(y, RET_D // 2, 1) * sin
    for j in range(3):
        y = _dot(h, w_ref[:, j * 256:(j + 1) * 256])
        for half in range(2):
            sl = slice(j * 256 + half * RET_D, j * 256 + (half + 1) * RET_D)
            q_ref[:, sl] = rot(y[:, half * RET_D:(half + 1) * RET_D])
    for j in range(3):
        y = _dot(h, w_ref[:, MIX_W + j * 256:MIX_W + (j + 1) * 256])
        for half in range(2):
            sl = slice(j * 256 + half * RET_D, j * 256 + (half + 1) * RET_D)
            k_ref[:, sl] = rot(y[:, half * RET_D:(half + 1) * RET_D]) * (RET_D ** -0.5)
    for j in range(3):
        sl = slice(j * 256, (j + 1) * 256)
        v_ref[:, sl] = _dot(h, w_ref[:, 2 * MIX_W + j * 256:2 * MIX_W + (j + 1) * 256])
    for j in range(3):
        sl = slice(j * 256, (j + 1) * 256)
        gate_ref[:, sl] = _dot(h, w_ref[:, 3 * MIX_W + j * 256:3 * MIX_W + (j + 1) * 256])
    y = _dot(h, w_ref[:, 4 * MIX_W:])
    qm_ref[...] = (_head_norm(y, bd_ref[...], gmq_ref[...]) * 0.125).astype(qm_ref.dtype)


def _ret_inproj(x, g, w, bd, gmq, cos, sin, *, tile, tiles_per_seq, act_dtype):
    n = x.shape[0]
    full = lambda shape: pl.BlockSpec(shape, lambda i: (0,) * len(shape))
    row = lambda width: pl.BlockSpec((tile, width), lambda i: (i, 0))
    tab = pl.BlockSpec((tile, RET_D), lambda i: (i % tiles_per_seq, 0))
    outs = [jax.ShapeDtypeStruct((n, MIX_W), F32)] * 4 + [jax.ShapeDtypeStruct((n, MEM_W), act_dtype)]
    return pl.pallas_call(
        _ret_inproj_body,
        out_shape=outs,
        grid=(n // tile,),
        in_specs=[row(D_MODEL), full((1, D_MODEL)), full(w.shape), full((256, 256)),
                  full((1, 256)), tab, tab],
        out_specs=[row(MIX_W)] * 4 + [row(MEM_W)],
        compiler_params=_cparams(("arbitrary",)),
        name="ret_inproj",
    )(x, g, w, bd, gmq, cos, sin)


def _fox_flash_body(q_ref, qx_ref, k_ref, kx_ref, v_ref, og_ref, o_ref, *, blk, pairs):
    qi = pl.program_id(2)
    heads = 2 * pairs
    lane = lax.broadcasted_iota(I32, (blk, LANES), 1)
    lo = lane < HEAD_D
    qs = []
    for h in range(heads):
        sl = slice((h // 2) * LANES, (h // 2 + 1) * LANES)
        q, qx = q_ref[:, sl], qx_ref[:, sl]
        keep = lo if h % 2 == 0 else jnp.logical_not(lo)
        keep_x = (lane >= 6 * (h % 2)) & (lane < 6 * (h % 2) + 6)
        qs.append(jnp.concatenate([jnp.where(keep, q, jnp.zeros_like(q)),
                                   jnp.where(keep_x, qx, jnp.zeros_like(qx))], axis=1))
    rows = lax.broadcasted_iota(I32, (blk, blk), 0)
    cols = lax.broadcasted_iota(I32, (blk, blk), 1)
    causal = rows >= cols

    def step(start, carry, diag):
        ms, ls, accs = carry
        new_m, new_l, new_acc = [], [], []
        for p in range(pairs):
            sl = slice(p * LANES, (p + 1) * LANES)
            k = jnp.concatenate([k_ref[pl.ds(start, blk), sl], kx_ref[pl.ds(start, blk), sl]], axis=1)
            v = v_ref[pl.ds(start, blk), sl]
            alphas, pvs = [], []
            for j in range(2):
                h = 2 * p + j
                s = _dot_nt(qs[h], k)
                if diag:
                    s = jnp.where(causal, s, NEG)
                m_new = jnp.maximum(ms[h], jnp.max(s, axis=-1, keepdims=True))
                alpha = jnp.exp2(ms[h] - m_new)
                e = jnp.exp2(s - m_new)
                new_l.append(alpha * ls[h] + jnp.sum(e, axis=-1, keepdims=True))
                new_m.append(m_new)
                alphas.append(alpha)
                pvs.append(_dot(e.astype(BF16), v))
            new_acc.append(accs[p] * jnp.where(lo, alphas[0], alphas[1]) + jnp.where(lo, pvs[0], pvs[1]))
        return tuple(new_m), tuple(new_l), tuple(new_acc)

    init = ((jnp.full((blk, 1), NEG, F32),) * heads, (jnp.zeros((blk, 1), F32),) * heads,
            (jnp.zeros((blk, LANES), F32),) * pairs)
    carry = lax.fori_loop(
        0, qi, lambda i, c: step(pl.multiple_of(i * blk, blk), c, False), init)
    _, ls, accs = step(pl.multiple_of(qi * blk, blk), carry, True)
    for p in range(pairs):
        sl = slice(p * LANES, (p + 1) * LANES)
        o = accs[p] / jnp.where(lo, ls[2 * p], ls[2 * p + 1])
        o_ref[:, sl] = (o * _sigmoid(og_ref[:, sl])).astype(o_ref.dtype)


def _fox_flash(q, qx, kb, kx, vb, og, *, batch, seq, blk, pairs):
    n = q.shape[0]
    nq = seq // blk
    w = pairs * LANES
    qblk = pl.BlockSpec((blk, w), lambda b, g, i: (b * nq + i, g))
    kvblk = pl.BlockSpec((seq, w), lambda b, g, i: (b, g))
    return pl.pallas_call(
        functools.partial(_fox_flash_body, blk=blk, pairs=pairs),
        out_shape=jax.ShapeDtypeStruct((n, MIX_W), BF16),
        grid=(batch, FOX_HEADS // (2 * pairs), nq),
        in_specs=[qblk, qblk, kvblk, kvblk, kvblk, qblk],
        out_specs=qblk,
        compiler_params=_cparams(("arbitrary",) * 3),
        name="fox_flash",
    )(q, qx, kb, kx, vb, og)


def _fox_decode_body(pt_ref, q_ref, kn_ref, vn_ref, lfn_ref, og_ref, kt_hbm, vt_hbm, lf_hbm, tri_ref,
                     o_ref, kbuf, vbuf, lfbuf, sems, qbd_ref, m_ref, l_ref, acc_ref, carry_ref, *,
                     pages_per_step):
    pp = pages_per_step
    j = pl.program_id(1)
    steps = pl.num_programs(1)
    step = pl.program_id(0) * steps + j
    slot = step % 2

    def page_copies(s, into):
        copies = []
        for i in range(pp):
            page = pt_ref[s * pp + i]
            copies += [pltpu.make_async_copy(kt_hbm.at[page], kbuf.at[into, i], sems.at[into]),
                       pltpu.make_async_copy(vt_hbm.at[page], vbuf.at[into, i], sems.at[into]),
                       pltpu.make_async_copy(lf_hbm.at[page], lfbuf.at[into, i], sems.at[into])]
        return copies

    def start_all(copies):
        for n, cp in enumerate(copies):
            cp.start(priority=1 if n % 3 == 1 else 0)

    @pl.when(step == 0)
    def _():
        start_all(page_copies(step, slot))

    @pl.when(step + 1 < pl.num_programs(0) * steps)
    def _():
        start_all(page_copies(step + 1, 1 - slot))

    for cp in page_copies(step, slot):
        cp.wait()

    nrow = FOX_HEADS * 8
    rowh = lax.broadcasted_iota(I32, (nrow, MIX_W), 0) // 8
    colh = lax.broadcasted_iota(I32, (nrow, MIX_W), 1) // HEAD_D
    diag = rowh == colh

    @pl.when(j == 0)
    def _():
        q = q_ref[...]
        qbd_ref[...] = jnp.where(diag, jnp.concatenate([q] * FOX_HEADS, axis=0), 0.0)
        m_ref[...] = jnp.full_like(m_ref, NEG)
        l_ref[...] = jnp.zeros_like(l_ref)
        acc_ref[...] = jnp.zeros_like(acc_ref)
        carry_ref[...] = jnp.zeros_like(carry_ref)

    qbd = qbd_ref[...]
    tri = tri_ref[...]

    def scores(kt, lf, carry):
        c = _dot_exact_rhs01_left(lf, tri) + carry
        bias = jnp.concatenate(
            [jnp.broadcast_to(c[h:h + 1, :], (8, PAGE)) for h in range(FOX_HEADS)], axis=0)
        return _dot(qbd, kt) - bias, c[:, PAGE - 1:PAGE]

    def update(ss, vts):
        m_old = m_ref[...]
        m_new = m_old
        for s in ss:
            m_new = jnp.maximum(m_new, jnp.max(s, axis=-1, keepdims=True))
        alpha = jnp.exp(m_old - m_new)
        l_new = alpha * l_ref[...]
        pv = None
        for s, vt in zip(ss, vts):
            p = jnp.exp(s - m_new)
            l_new = l_new + jnp.sum(p, axis=-1, keepdims=True)
            d = _dot_nt(p, vt)
            pv = d if pv is None else pv + d
        m_ref[...] = m_new
        l_ref[...] = l_new
        acc_ref[...] = alpha * acc_ref[...] + pv

    carry = carry_ref[...]
    ss, vts = [], []
    for i in range(pp):
        s, carry = scores(kbuf[slot, i].reshape(MIX_W, PAGE), lfbuf[slot, i], carry)
        ss.append(s)
        vts.append(vbuf[slot, i].reshape(MIX_W, PAGE))
    carry_ref[...] = carry
    update(ss, vts)

    @pl.when(j == pl.num_programs(1) - 1)
    def _():
        s, _ = scores(kn_ref[...].reshape(MIX_W, PAGE), lfn_ref[...], carry_ref[...])
        qpos = lax.broadcasted_iota(I32, (nrow, PAGE), 0) % 8
        kpos = lax.broadcasted_iota(I32, (nrow, PAGE), 1)
        s = jnp.where(qpos >= kpos, s, NEG)
        update([s], [vn_ref[...].reshape(MIX_W, PAGE)])
        o = jnp.where(diag, acc_ref[...] / l_ref[...], 0.0)
        out = o[0:8, :]
        for h in range(1, FOX_HEADS):
            out = out + o[h * 8:(h + 1) * 8, :]
        o_ref[...] = (out * _sigmoid(og_ref[...])).astype(o_ref.dtype)


def _dot_exact_rhs01_left(x, m01):
    a, b, c = _split3(x)
    return _dot(a, m01) + _dot(b, m01) + _dot(c, m01)


def _fox_decode(page_table, q, knt, vnt, lfnt, og, kt_pool, vt_pool, lf_pool, *, pages_per_step):
    nb, npages = page_table.shape
    pp = pages_per_step
    steps = npages // pp
    tri = jnp.triu(jnp.ones((PAGE, PAGE), BF16))
    pt = page_table.reshape(-1)

    per_b = lambda shape: pl.BlockSpec((None,) + shape, lambda b, j, pt_ref: (b,) + (0,) * len(shape))
    rows8 = lambda width: pl.BlockSpec((8, width), lambda b, j, pt_ref: (b, 0))
    pool = pl.BlockSpec(memory_space=pl.ANY)
    in_specs = [rows8(MIX_W), per_b((FOX_HEADS, HEAD_D, PAGE)), per_b((FOX_HEADS, HEAD_D, PAGE)),
                per_b((FOX_HEADS, PAGE)), rows8(MIX_W), pool, pool, pool,
                pl.BlockSpec((PAGE, PAGE), lambda b, j, pt_ref: (0, 0))]
    nrow = FOX_HEADS * 8
    return pl.pallas_call(
        functools.partial(_fox_decode_body, pages_per_step=pp),
        out_shape=jax.ShapeDtypeStruct((nb * 8, MIX_W), F32),
        grid_spec=pltpu.PrefetchScalarGridSpec(
            num_scalar_prefetch=1,
            grid=(nb, steps),
            in_specs=in_specs,
            out_specs=pl.BlockSpec((8, MIX_W), lambda b, j, pt_ref: (b, 0)),
            scratch_shapes=[pltpu.VMEM((2, pp, FOX_HEADS, HEAD_D, PAGE), F32),
                            pltpu.VMEM((2, pp, FOX_HEADS, HEAD_D, PAGE), F32),
                            pltpu.VMEM((2, pp, FOX_HEADS, PAGE), F32),
                            pltpu.SemaphoreType.DMA((2,)),
                            pltpu.VMEM((nrow, MIX_W), F32), pltpu.VMEM((nrow, 1), F32),
                            pltpu.VMEM((nrow, 1), F32), pltpu.VMEM((nrow, MIX_W), F32),
                            pltpu.VMEM((FOX_HEADS, 1), F32)]),
        compiler_params=_cparams(("arbitrary", "arbitrary")),
        name="fox_decode",
    )(pt, q, knt, vnt, lfnt, og, kt_pool, vt_pool, lf_pool, tri)


def _retention_body(q_ref, k_ref, v_ref, gate_ref, s0_ref, dmat_ref, qdec_ref, kdec_ref, gc_ref,
                    go_ref, mix_ref, state_ref, *, n_chunks, heads, unroll):
    go = go_ref[...]

    def chunk(n, states):
        rows = pl.ds(pl.multiple_of(n * RET_D, RET_D), RET_D)
        new_states = []
        for h in range(heads):
            cols = slice(h * RET_D, (h + 1) * RET_D)
            q = q_ref[rows, cols]
            k = k_ref[rows, cols]
            v = v_ref[rows, cols].astype(BF16)
            inner = _dot_nt(q.astype(BF16), k.astype(BF16)) * dmat_ref[h]
            o = (_dot(inner.astype(BF16), v)
                 + _dot((q * qdec_ref[h]).astype(BF16), states[h].astype(BF16)))
            new_states.append(gc_ref[h] * states[h] + _dot((k * kdec_ref[h]).T.astype(BF16), v))
            g = gate_ref[rows, cols]
            mix_ref[rows, cols] = (_rms_rows(o, go) * (g * _sigmoid(g))).astype(mix_ref.dtype)
        return tuple(new_states)

    states = lax.fori_loop(0, n_chunks, chunk, tuple(s0_ref[h] for h in range(heads)), unroll=unroll)
    for h in range(heads):
        state_ref[h] = states[h]


def _retention(q, k, v, gate, s0, tabs, go, *, batch, seq, out_dtype, heads):
    n = q.shape[0]
    n_chunks = seq // RET_D
    dmat, qdec, kdec, gc = tabs
    seq_blk = pl.BlockSpec((seq, heads * RET_D), lambda b, g: (b, g))
    tab = pl.BlockSpec((heads, RET_D, RET_D), lambda b, g: (g, 0, 0))
    st = pl.BlockSpec((None, heads, RET_D, RET_D), lambda b, g: (b, g, 0, 0))
    return pl.pallas_call(
        functools.partial(_retention_body, n_chunks=n_chunks, heads=heads,
                          unroll=4 if n_chunks % 4 == 0 else 1),
        out_shape=[jax.ShapeDtypeStruct((n, MIX_W), out_dtype),
                   jax.ShapeDtypeStruct((batch, RET_HEADS, RET_D, RET_D), F32)],
        grid=(batch, RET_HEADS // heads),
        in_specs=[seq_blk, seq_blk, seq_blk, seq_blk, st, tab, tab, tab, tab,
                  pl.BlockSpec((1, RET_D), lambda b, g: (0, 0))],
        out_specs=[seq_blk, st],
        compiler_params=_cparams(("arbitrary", "arbitrary")),
        name="retention",
    )(q, k, v, gate, s0, dmat, qdec, kdec, gc, go)


def _retention_tables(chunk):
    log_g = jnp.log1p(-jnp.exp2(-5.0 - jnp.arange(RET_HEADS, dtype=F32)))
    n = jnp.arange(RET_D, dtype=F32)
    diff = n[:, None] - n[None, :]
    valid = (n[:, None] < chunk) & (n[None, :] < chunk)
    dmat = jnp.where((diff >= 0) & valid,
                     jnp.exp(jnp.maximum(diff, 0.0)[None] * log_g[:, None, None]), 0.0)
    qdec = jnp.exp((n + 1.0)[None, :] * log_g[:, None])
    kdec = jnp.where(n[None, :] < chunk, jnp.exp((chunk - 1.0 - n)[None, :] * log_g[:, None]), 0.0)
    gc = jnp.exp(chunk * log_g)
    bc = lambda a: jnp.broadcast_to(a[:, :, None], (RET_HEADS, RET_D, RET_D))
    return (dmat, bc(qdec), bc(kdec),
            jnp.broadcast_to(gc[:, None, None], (RET_HEADS, RET_D, RET_D)))


def _mem_kv_body(mem_ref, g_ref, w_ref, bd_ref, gk_ref, k_ref, v_ref):
    h = _rms_rows(mem_ref[...], g_ref[...]).astype(BF16)
    k_ref[...] = _head_norm(_dot(h, w_ref[:, :MEM_W]), bd_ref[...], gk_ref[...])
    v_ref[...] = _dot(h, w_ref[:, MEM_W:])


def _mem_kv(mem, g, w, bd, gk, *, tile):
    depth = w.shape[0]
    n = mem.shape[0]
    per_layer = lambda shape: pl.BlockSpec((None,) + shape, lambda l, i: (l,) + (0,) * len(shape))
    out = pl.BlockSpec((None, tile, MEM_W), lambda l, i: (l, i, 0))
    return pl.pallas_call(
        _mem_kv_body,
        out_shape=[jax.ShapeDtypeStruct((depth, n, MEM_W), F32)] * 2,
        grid=(depth, n // tile),
        in_specs=[pl.BlockSpec((tile, D_MODEL), lambda l, i: (i, 0)),
                  per_layer((1, D_MODEL)), per_layer((D_MODEL, 2 * MEM_W)),
                  pl.BlockSpec((256, 256), lambda l, i: (0, 0)), per_layer((1, MEM_W))],
        out_specs=[out, out],
        compiler_params=_cparams(("arbitrary", "arbitrary")),
        name="mem_kv",
    )(mem, g, w, bd, gk)


def _mem_attend_body(q_ref, k_ref, v_ref, o_ref):
    t = q_ref.shape[0]
    lane = lax.broadcasted_iota(I32, (t, LANES), 1)
    lo = lane < HEAD_D
    for slab in range(MEM_W // LANES):
        sl = slice(slab * LANES, (slab + 1) * LANES)
        q = q_ref[:, sl]
        k = k_ref[:, sl].astype(BF16)
        v = v_ref[:, sl].astype(BF16)
        zq = jnp.zeros_like(q)
        outs = []
        for qh in (jnp.where(lo, q, zq), jnp.where(lo, zq, q)):
            s = _dot_nt(qh, k)
            p = jnp.exp(s - jnp.max(s, axis=-1, keepdims=True))
            p = p / jnp.sum(p, axis=-1, keepdims=True)
            outs.append(_dot(p.astype(BF16), v))
        o_ref[:, sl] = jnp.where(lo, outs[0], outs[1]).astype(o_ref.dtype)


def _mem_attend(qm, mk, mv, *, layer, batch, seq, tile):
    n = qm.shape[0]
    nt = seq // tile
    kv = pl.BlockSpec((None, N_MEM, MEM_W), lambda b, i: (layer, b, 0))
    return pl.pallas_call(
        _mem_attend_body,
        out_shape=jax.ShapeDtypeStruct((n, MEM_W), BF16),
        grid=(batch, nt),
        in_specs=[pl.BlockSpec((tile, MEM_W), lambda b, i: (b * nt + i, 0)), kv, kv],
        out_specs=pl.BlockSpec((tile, MEM_W), lambda b, i: (b * nt + i, 0)),
        compiler_params=_cparams(("arbitrary", "arbitrary")),
        name="mem_attend",
    )(qm, mk, mv)


def _mem_attend_t_body(q_ref, kt_ref, vt_ref, o_ref):
    t = q_ref.shape[0]
    lane = lax.broadcasted_iota(I32, (t, LANES), 1)
    lo = lane < HEAD_D
    for slab in range(MEM_W // LANES):
        sl = slice(slab * LANES, (slab + 1) * LANES)
        q = q_ref[:, sl]
        kt = kt_ref[2 * slab:2 * slab + 2].reshape(LANES, N_MEM)
        vt = vt_ref[2 * slab:2 * slab + 2].reshape(LANES, N_MEM)
        zq = jnp.zeros_like(q)
        outs = []
        for qh in (jnp.where(lo, q, zq), jnp.where(lo, zq, q)):
            s = _dot(qh, kt)
            p = jnp.exp(s - jnp.max(s, axis=-1, keepdims=True))
            p = p / jnp.sum(p, axis=-1, keepdims=True)
            outs.append(_dot_nt(p, vt))
        o_ref[:, sl] = jnp.where(lo, outs[0], outs[1])


def _mem_attend_t(qm, mkt, mvt, *, layer, batch, tokens):
    kv = pl.BlockSpec((None, None, MEM_W // HEAD_D, HEAD_D, N_MEM), lambda b: (layer, b, 0, 0, 0))
    return pl.pallas_call(
        _mem_attend_t_body,
        out_shape=jax.ShapeDtypeStruct((batch * tokens, MEM_W), F32),
        grid=(batch,),
        in_specs=[pl.BlockSpec((tokens, MEM_W), lambda b: (b, 0)), kv, kv],
        out_specs=pl.BlockSpec((tokens, MEM_W), lambda b: (b, 0)),
        compiler_params=_cparams(("arbitrary",)),
        name="mem_attend_t",
    )(qm, mkt, mvt)


def _outproj_router_body(x_ref, mix_ref, memo_ref, wa_ref, wb_ref, g_ref, wr_ref, br_ref, tri_ref,
                         xo_ref, h_ref, ri_ref, rw_ref, cnt_ref, carry_ref):
    sub = tri_ref.shape[0]

    @pl.when(pl.program_id(0) == 0)
    def _():
        carry_ref[...] = jnp.zeros_like(carry_ref)

    carry = carry_ref[...]
    for s0 in range(0, x_ref.shape[0], sub):
        rows = slice(s0, s0 + sub)
        x = (x_ref[rows, :] + _dot(mix_ref[rows, :].astype(BF16), wa_ref[...])
             + _dot(memo_ref[rows, :].astype(BF16), wb_ref[...]))
        xo_ref[rows, :] = x
        h = _rms_rows(x, g_ref[...])
        _to_token_major(h_ref.at[pl.ds(s0 * TOK_ROWS, sub * TOK_ROWS), :], h)
        carry = _route(h, wr_ref, br_ref[...], tri_ref[...], carry, ri_ref.at[rows, :], rw_ref.at[rows, :])
    carry_ref[...] = carry
    cnt_ref[...] = carry


def _route(h, wr_ref, br, tri, carry, ri_ref, rw_ref):
    t = h.shape[0]
    h_hi = h.astype(BF16)
    h_lo = (h - h_hi.astype(F32)).astype(BF16)
    d = _dot(h_hi, wr_ref[...])
    logits = d[:, :LANES] + d[:, LANES:] + _dot(h_lo, wr_ref[:, :LANES]) + br
    lane = lax.broadcasted_iota(I32, (t, LANES), 1)
    big = jnp.int32(1 << 20)
    rmax = lambda a: jnp.max(a, axis=-1, keepdims=True)
    rmin = lambda a: jnp.min(a, axis=-1, keepdims=True)
    rsum = lambda a: jnp.sum(a, axis=-1, keepdims=True)
    gmask = lane < N_GROUPS
    lg = jnp.where(gmask, logits, -jnp.inf)
    gmax = rmax(lg)
    gi = rmin(jnp.where(lg == gmax, lane, big))
    p_top = 1.0 / rsum(jnp.where(gmask, jnp.exp(logits - gmax), 0.0))
    first = ROUTE_LANE0 + EPG * gi
    emask = (lane >= first) & (lane < first + EPG)
    le = jnp.where(emask, logits, -jnp.inf)
    v1 = rmax(le)
    i1 = rmin(jnp.where(le == v1, lane, big))
    le2 = jnp.where(lane == i1, -jnp.inf, le)
    v2 = rmax(le2)
    i2 = rmin(jnp.where(le2 == v2, lane, big))
    e2 = jnp.exp(v2 - v1)
    w1 = p_top / (1.0 + e2)
    w2 = p_top * e2 / (1.0 + e2)

    onehot = ((lane == i1) | (lane == i2)).astype(F32)
    before = _dot(tri, onehot.astype(BF16)) + carry
    r1 = rsum(jnp.where(lane == i1, before, 0.0)).astype(I32)
    r2 = rsum(jnp.where(lane == i2, before, 0.0)).astype(I32)
    lane4 = lax.broadcasted_iota(I32, (t, 4), 1)
    ri_ref[...] = jnp.where(lane4 == 0, i1 - ROUTE_LANE0,
                            jnp.where(lane4 == 1, i2 - ROUTE_LANE0, jnp.where(lane4 == 2, r1, r2)))
    lane2 = lax.broadcasted_iota(I32, (t, 2), 1)
    rw_ref[...] = jnp.where(lane2 == 0, w1, w2)
    return carry + jnp.sum(onehot, axis=0, keepdims=True)


def _outproj_router(x, mix, memo, wa, wb, g, wr, br, *, tile):
    n = x.shape[0]
    sub = tile
    tri = jnp.tril(jnp.ones((sub, sub), BF16), -1)
    full = lambda shape: pl.BlockSpec(shape, lambda i: (0,) * len(shape))
    row = lambda width: pl.BlockSpec((tile, width), lambda i: (i, 0))
    return pl.pallas_call(
        _outproj_router_body,
        out_shape=[jax.ShapeDtypeStruct((n, D_MODEL), F32),
                   jax.ShapeDtypeStruct((n * TOK_ROWS, LANES), F32),
                   jax.ShapeDtypeStruct((n, 4), I32), jax.ShapeDtypeStruct((n, 2), F32),
                   jax.ShapeDtypeStruct((1, LANES), F32)],
        grid=(n // tile,),
        in_specs=[row(D_MODEL), row(MIX_W), row(MEM_W), full((MIX_W, D_MODEL)), full((MEM_W, D_MODEL)),
                  full((1, D_MODEL)), full((D_MODEL, 2 * LANES)), full((1, LANES)), full((sub, sub))],
        out_specs=[row(D_MODEL), pl.BlockSpec((tile * TOK_ROWS, LANES), lambda i: (i, 0)),
                   row(4), row(2), full((1, LANES))],
        scratch_shapes=[pltpu.VMEM((1, LANES), F32)],
        compiler_params=_cparams(("arbitrary",)),
        name="outproj_router",
    )(x, mix, memo, wa, wb, g, wr, br, tri)


TOK_ROWS = D_MODEL // LANES
DMA_UNROLL = 16


def _to_token_major(ref, x):
    t = x.shape[0]
    for s in range(TOK_ROWS):
        ref[pl.ds(s, t, stride=TOK_ROWS), :] = x[:, s * LANES:(s + 1) * LANES]


def _from_token_major(ref, t):
    return [ref[pl.ds(s, t, stride=TOK_ROWS), :] for s in range(TOK_ROWS)]


def _token_tile(ref, row):
    return ref.at[pl.ds(pl.multiple_of(row * TOK_ROWS, TOK_ROWS), TOK_ROWS), :]


def _moe_dispatch_body(eid_ref, rank_ref, off_ref, zt_ref, h_ref, xs_hbm, zero_ref, sem, *,
                       tile, expert_tile, n_expert_tiles):
    base = pl.program_id(0) * tile

    @pl.when(pl.program_id(0) == 0)
    def _():
        zero_ref[...] = jnp.zeros_like(zero_ref)
        rows = expert_tile * TOK_ROWS

        def fill(wait):
            def body(i, c):
                @pl.when(zt_ref[i] == 1)
                def _():
                    dst = xs_hbm.at[pl.ds(pl.multiple_of(i * rows, rows), rows), :]
                    cp = pltpu.make_async_copy(zero_ref, dst, sem)
                    cp.wait() if wait else cp.start()
                return c
            lax.fori_loop(0, n_expert_tiles, body, 0)

        fill(False)
        fill(True)

    def issue(t, c):
        for slot in range(2):
            a = 2 * (base + t) + slot
            dst = off_ref[eid_ref[a]] + rank_ref[a]
            pltpu.make_async_copy(_token_tile(h_ref, t), _token_tile(xs_hbm, dst), sem).start(priority=slot)
        return c

    lax.fori_loop(0, tile, issue, 0, unroll=DMA_UNROLL)

    group = pl.ds(0, DMA_UNROLL * TOK_ROWS)
    drain_group = pltpu.make_async_copy(h_ref.at[group, :], xs_hbm.at[group, :], sem)

    def drain(t, c):
        drain_group.wait()
        return c

    lax.fori_loop(0, 2 * tile // DMA_UNROLL, drain, 0)


def _moe_dispatch(eid, rank, off, zero_tile, h_tm, *, tile, expert_tile):
    n = h_tm.shape[0] // TOK_ROWS
    n_expert_tiles = zero_tile.shape[0]
    return pl.pallas_call(
        functools.partial(_moe_dispatch_body, tile=tile, expert_tile=expert_tile,
                          n_expert_tiles=n_expert_tiles),
        out_shape=jax.ShapeDtypeStruct((n_expert_tiles * expert_tile * TOK_ROWS, LANES), F32),
        grid_spec=pltpu.PrefetchScalarGridSpec(
            num_scalar_prefetch=4, grid=(n // tile,),
            in_specs=[pl.BlockSpec((tile * TOK_ROWS, LANES), lambda i, *_: (i, 0))],
            out_specs=pl.BlockSpec(memory_space=pl.ANY),
            scratch_shapes=[pltpu.VMEM((expert_tile * TOK_ROWS, LANES), F32),
                            pltpu.SemaphoreType.DMA(())]),
        compiler_params=_cparams(("arbitrary",)),
        name="moe_dispatch",
    )(eid, rank, off, zero_tile, h_tm)


X_SLOTS = 3


def _moe_experts_body(te_ref, nu_ref, xs_hbm, w1_ref, w3_ref, w2_ref, y_ref, xbuf, sems, *, tile):
    del te_ref
    i = pl.program_id(0)
    n = pl.num_programs(0)
    rows = tile * TOK_ROWS

    def x_copy(t):
        src = xs_hbm.at[pl.ds(pl.multiple_of(t * rows, rows), rows), :]
        return pltpu.make_async_copy(src, xbuf.at[t % X_SLOTS], sems.at[t % X_SLOTS])

    @pl.when(i == 0)
    def _():
        for t in range(X_SLOTS - 1):
            @pl.when(t < n)
            def _():
                x_copy(t).start()

    @pl.when(i + X_SLOTS - 1 < n)
    def _():
        x_copy(i + X_SLOTS - 1).start()

    x_copy(i).wait()
    x_ref = xbuf.at[i % X_SLOTS]

    @pl.when(pl.program_id(0) < nu_ref[0])
    def _():
        x = jnp.concatenate(_from_token_major(x_ref, tile), axis=1).astype(BF16)
        a = _dot(x, w1_ref[...].astype(BF16))
        b = _dot(x, w3_ref[...].astype(BF16))
        hid = a * _sigmoid(a) * b
        _to_token_major(y_ref, _dot(hid.astype(BF16), w2_ref[...].astype(BF16)))

    @pl.when(pl.program_id(0) >= nu_ref[0])
    def _():
        y_ref[...] = jnp.zeros_like(y_ref)


def _moe_experts(tile_expert, n_used, xs, w1, w3, w2, *, tile, layer):
    rows = xs.shape[0] // TOK_ROWS
    wspec = lambda shape: pl.BlockSpec((None, None) + shape, lambda i, te, nu: (layer, te[i], 0, 0))
    blk = pl.BlockSpec((tile * TOK_ROWS, LANES), lambda i, te, nu: (i, 0))
    return pl.pallas_call(
        functools.partial(_moe_experts_body, tile=tile),
        out_shape=jax.ShapeDtypeStruct(xs.shape, F32),
        grid_spec=pltpu.PrefetchScalarGridSpec(
            num_scalar_prefetch=2, grid=(rows // tile,),
            in_specs=[pl.BlockSpec(memory_space=pl.ANY), wspec((D_MODEL, D_EXPERT)),
                      wspec((D_MODEL, D_EXPERT)), wspec((D_EXPERT, D_MODEL))],
            out_specs=blk,
            scratch_shapes=[pltpu.VMEM((X_SLOTS, tile * TOK_ROWS, LANES), F32),
                            pltpu.SemaphoreType.DMA((X_SLOTS,))]),
        compiler_params=_cparams(("arbitrary",)),
        name="moe_experts",
    )(tile_expert, n_used, xs, w1, w3, w2)


def _moe_combine_body(eid_ref, rank_ref, off_ref, x_ref, rw_ref, ys_hbm, o_ref, buf_ref, sem, *, tile):
    base = pl.program_id(0) * tile

    def issue(t, c):
        for slot in range(2):
            a = 2 * (base + t) + slot
            src = off_ref[eid_ref[a]] + rank_ref[a]
            pltpu.make_async_copy(_token_tile(ys_hbm, src), _token_tile(buf_ref.at[slot], t),
                                  sem).start(priority=slot)
        return c

    lax.fori_loop(0, tile, issue, 0, unroll=DMA_UNROLL)

    group = pl.ds(0, DMA_UNROLL * TOK_ROWS)
    drain_group = pltpu.make_async_copy(ys_hbm.at[group, :], buf_ref.at[0, group, :], sem)

    def drain(t, c):
        drain_group.wait()
        return c

    lax.fori_loop(0, 2 * tile // DMA_UNROLL, drain, 0)
    rw = rw_ref[...]
    y0 = _from_token_major(buf_ref.at[0], tile)
    y1 = _from_token_major(buf_ref.at[1], tile)
    for s in range(TOK_ROWS):
        sl = slice(s * LANES, (s + 1) * LANES)
        o_ref[:, sl] = x_ref[:, sl] + rw[:, 0:1] * y0[s] + rw[:, 1:2] * y1[s]


def _moe_combine(eid, rank, off, x, rw, ys, *, tile):
    n = x.shape[0]
    row = lambda width: pl.BlockSpec((tile, width), lambda i, *_: (i, 0))
    return pl.pallas_call(
        functools.partial(_moe_combine_body, tile=tile),
        out_shape=jax.ShapeDtypeStruct((n, D_MODEL), F32),
        grid_spec=pltpu.PrefetchScalarGridSpec(
            num_scalar_prefetch=3, grid=(n // tile,),
            in_specs=[row(D_MODEL), row(2), pl.BlockSpec(memory_space=pl.ANY)],
            out_specs=row(D_MODEL),
            scratch_shapes=[pltpu.VMEM((2, tile * TOK_ROWS, LANES), F32), pltpu.SemaphoreType.DMA(())]),
        compiler_params=_cparams(("arbitrary",)),
        name="moe_combine",
    )(eid, rank, off, x, rw, ys)


def _moe(x, h, ri, rw, cnt, w1, w3, w2, *, layer, expert_tile, token_tile):
    n = x.shape[0]
    n_tiles = (2 * n + N_EXPERTS * (expert_tile - 1)) // expert_tile + 1
    counts = cnt[0, ROUTE_LANE0:ROUTE_LANE0 + N_EXPERTS].astype(I32)
    padded = (counts + expert_tile - 1) // expert_tile * expert_tile
    ends = jnp.cumsum(padded)
    off = ends - padded
    n_used = (ends[-1] // expert_tile).reshape(1)
    tile_start = jnp.arange(n_tiles, dtype=I32) * expert_tile
    tile_expert = jnp.minimum(
        jnp.sum((tile_start[:, None] >= ends[None, :]).astype(I32), axis=1), N_EXPERTS - 1)
    last_of_expert = jnp.any((tile_start[:, None] + expert_tile == ends[None, :])
                             & (padded[None, :] > counts[None, :]), axis=1)
    zero_tile = (last_of_expert | (tile_start >= ends[-1])).astype(I32)
    eid = ri[:, 0:2].reshape(-1)
    rank = ri[:, 2:4].reshape(-1)
    xs = _moe_dispatch(eid, rank, off, zero_tile, h, tile=token_tile, expert_tile=expert_tile)
    ys = _moe_experts(tile_expert, n_used, xs, w1, w3, w2, tile=expert_tile, layer=layer)
    return _moe_combine(eid, rank, off, x, rw, ys, tile=token_tile)


def _block_diag_ones():
    r = jnp.arange(256) // HEAD_D
    return (r[:, None] == r[None, :]).astype(BF16)


def _tile_gain(g, reps):
    return jnp.tile(g.astype(F32), reps)[None, :]


def _rotary_tables(pos):
    half = RET_D // 2
    inv = 10000.0 ** (-jnp.arange(half, dtype=F32) / half)
    ang = pos.astype(F32)[:, None] * inv[None, :]
    cos, sin = jnp.cos(ang), jnp.sin(ang)
    return jnp.concatenate([cos, cos], axis=-1), jnp.concatenate([-sin, sin], axis=-1)


def kernel(x_prompt, x_sample, mem_prompt, cache_fox_k, cache_fox_v, cache_fox_logf, page_table,
           state_ret, cache_mem_k, cache_mem_v, w_in_fox, b_f_fox, g_q_fox, g_k_fox, w_in_ret, g_o_ret,
           g_attn, g_mem, w_mem_kv, g_mq, g_mk, w_out, g_ffn, w_rg, b_rg, w_re, b_re, w1, w3, w2):
    bp, sp, _ = x_prompt.shape
    bs, ts, _ = x_sample.shape
    depth = w_out.shape[0]
    n_pages = page_table.shape[1]
    past = n_pages * PAGE
    np_, ns = bp * sp, bs * ts
    tile_p, tile_s = 512, ns
    bd = _block_diag_ones()

    xp = x_prompt.reshape(np_, D_MODEL)
    xs = x_sample.reshape(ns, D_MODEL)
    mem = mem_prompt.reshape(bp * N_MEM, D_MODEL)
    mk_all, mv_all = _mem_kv(mem, g_mem[:, None, :], w_mem_kv.astype(BF16), bd,
                             jnp.tile(g_mk, (1, MEM_W // HEAD_D))[:, None, :], tile=512)

    mkt_cache = jnp.transpose(cache_mem_k, (0, 1, 3, 4, 2))
    mvt_cache = jnp.transpose(cache_mem_v, (0, 1, 3, 4, 2))

    outs = {}
    for l in range(depth):
        i = l // 2
        g_l = g_attn[l][None, :]
        gmq = _tile_gain(g_mq[l], 256 // HEAD_D)
        if l % 2 == 0:
            w = w_in_fox[i]
            w_pack = jnp.concatenate(
                [w[:, :4 * MIX_W], w[:, 4 * MIX_W + FOX_HEADS:], w[:, 4 * MIX_W:4 * MIX_W + FOX_HEADS],
                 jnp.zeros((D_MODEL, LANES - FOX_HEADS), F32)], axis=1).astype(BF16)
            bf = jnp.concatenate([b_f_fox[i], jnp.zeros((LANES - FOX_HEADS,), F32)])[None, :]
            gq = _tile_gain(g_q_fox[i], 256 // HEAD_D)
            gk = _tile_gain(g_k_fox[i], 256 // HEAD_D)
            fox = functools.partial(_fox_inproj, g=g_l, w=w_pack, bd=bd, gq=gq, gk=gk, gmq=gmq, bf=bf)
            qp, kp, kbp, vp, vbp, ogp, qmp, lfp, qxp, kxp = fox(
                xp, tile=tile_p, tiles_per_seq=sp // tile_p, act_dtype=BF16,
                q_scale=HEAD_D ** -0.5 * LOG2E, bias_scale=LOG2E, kv_transposed=True)
            qs, ks, _, vs, _, ogs, qms, lfs, _, _ = fox(
                xs, tile=tile_s, tiles_per_seq=1, act_dtype=F32, q_scale=HEAD_D ** -0.5, bias_scale=1.0,
                kv_transposed=False)
            mix_p = _fox_flash(qp, qxp, kbp, kxp, vbp, ogp, batch=bp, seq=sp, blk=256,
                               pairs=FOX_HEADS // 2)

            def pad_t(a):
                a = jnp.transpose(a.reshape(bs, ts, -1), (0, 2, 1))
                return jnp.pad(a, ((0, 0), (0, 0), (0, PAGE - ts)))

            knt = pad_t(ks).reshape(bs, FOX_HEADS, HEAD_D, PAGE)
            vnt = pad_t(vs).reshape(bs, FOX_HEADS, HEAD_D, PAGE)
            lfnt = pad_t(lfs)
            kt_pool = jnp.transpose(cache_fox_k[i], (0, 2, 3, 1))
            vt_pool = jnp.transpose(cache_fox_v[i], (0, 2, 3, 1))
            lf_pool = jnp.transpose(cache_fox_logf[i], (0, 2, 1))
            mix_s = _fox_decode(page_table, qs, knt, vnt, lfnt, ogs, kt_pool, vt_pool, lf_pool,
                                pages_per_step=16)
            to_rows = lambda a: jnp.transpose(a.reshape(bp, FOX_HEADS, HEAD_D, sp), (0, 3, 1, 2))
            outs.setdefault("fkp", []).append(to_rows(kp))
            outs.setdefault("fvp", []).append(to_rows(vp))
            outs.setdefault("flp", []).append(lfp.reshape(bp, sp, FOX_HEADS))
            outs.setdefault("fks", []).append(ks.reshape(bs, ts, FOX_HEADS, HEAD_D))
            outs.setdefault("fvs", []).append(vs.reshape(bs, ts, FOX_HEADS, HEAD_D))
            outs.setdefault("fls", []).append(lfs.reshape(bs, ts, FOX_HEADS))
        else:
            w = w_in_ret[i].astype(BF16)
            go = g_o_ret[i][None, :]
            cos_p, sin_p = _rotary_tables(jnp.arange(sp))
            cos_s, sin_s = _rotary_tables(past + jnp.arange(ts))
            ret = functools.partial(_ret_inproj, g=g_l, w=w, bd=bd, gmq=gmq)
            qp, kp, vp, gp, qmp = ret(xp, cos=cos_p, sin=sin_p, tile=tile_p,
                                      tiles_per_seq=sp // tile_p, act_dtype=BF16)
            qs, ks, vs, gs, qms = ret(xs, cos=jnp.tile(cos_s, (bs, 1)), sin=jnp.tile(sin_s, (bs, 1)),
                                      tile=tile_s, tiles_per_seq=1, act_dtype=F32)
            mix_p, sp_state = _retention(
                qp, kp, vp, gp, jnp.zeros((bp, RET_HEADS, RET_D, RET_D), F32),
                _retention_tables(RET_D), go, batch=bp, seq=sp, out_dtype=BF16, heads=3)
            pad_rows = lambda a: jnp.pad(a.reshape(bs, ts, MIX_W),
                                         ((0, 0), (0, RET_D - ts), (0, 0))).reshape(bs * RET_D, MIX_W)
            mix_s_pad, ss_state = _retention(
                pad_rows(qs), pad_rows(ks), pad_rows(vs), pad_rows(gs), state_ret[i],
                _retention_tables(ts), go, batch=bs, seq=RET_D, out_dtype=F32, heads=RET_HEADS)
            mix_s = mix_s_pad.reshape(bs, RET_D, MIX_W)[:, :ts].reshape(ns, MIX_W)
            outs.setdefault("rsp", []).append(sp_state)
            outs.setdefault("rss", []).append(ss_state)

        memo_p = _mem_attend(qmp, mk_all, mv_all, layer=l, batch=bp, seq=sp, tile=tile_p)
        memo_s = _mem_attend_t(qms, mkt_cache, mvt_cache, layer=l, batch=bs, tokens=ts)

        wo = w_out[l].astype(BF16)
        wr = jnp.concatenate([w_rg[l], w_re[l], jnp.zeros((D_MODEL, LANES - N_GROUPS - N_EXPERTS), F32)], axis=1)
        wr_hi = wr.astype(BF16)
        wr_lo = (wr - wr_hi.astype(F32)).astype(BF16)
        br = jnp.concatenate([b_rg[l], b_re[l], jnp.zeros((LANES - N_GROUPS - N_EXPERTS,), F32)])[None, :]
        route = functools.partial(_outproj_router, wa=wo[:MIX_W], wb=wo[MIX_W:], g=g_ffn[l][None, :],
                                  wr=jnp.concatenate([wr_hi, wr_lo], axis=1), br=br)
        xp, hp, rip, rwp, cntp = route(xp, mix_p, memo_p, tile=tile_p)
        xs, hs, ris, rws, cnts = route(xs, mix_s, memo_s, tile=tile_s)
        xp = _moe(xp, hp, rip, rwp, cntp, w1, w3, w2, layer=l, expert_tile=256, token_tile=tile_p)
        xs = _moe(xs, hs, ris, rws, cnts, w1, w3, w2, layer=l, expert_tile=128, token_tile=tile_s)

    mem_shape = (depth, bp, N_MEM, MEM_W // HEAD_D, HEAD_D)
    return (xp.reshape(bp, sp, D_MODEL), xs.reshape(bs, ts, D_MODEL),
            jnp.stack(outs["fkp"]), jnp.stack(outs["fvp"]), jnp.stack(outs["flp"]),
            jnp.stack(outs["fks"]), jnp.stack(outs["fvs"]), jnp.stack(outs["fls"]),
            jnp.stack(outs["rsp"]), jnp.stack(outs["rss"]),
            mk_all.reshape(mem_shape), mv_all.reshape(mem_shape))
```
